```python
import math
import jax, jax.numpy as jnp
from jax import lax
import numpy as np

D_MODEL = 1024
BATCH = 1
SEQ = 16384
DEPTH = 2
DEC_BATCH = 8
DEC_SEQ = 64
PAST_LEN = 2048

CHUNK = 64
N_MIXERS = 2
N_A_LAYERS = (DEPTH + 1) // 2
N_B_LAYERS = DEPTH // 2
EPS = 1e-6
GDN_HEADS = 8
GDN_DK = 128
GDN_DV = 128
GDN_CONV = 4
GDN_QKV = GDN_HEADS * (2 * GDN_DK + GDN_DV)
GDN_VW = GDN_HEADS * GDN_DV
SWA_HEADS = 16
SWA_KV_HEADS = 4
SWA_HD = 64
WINDOW = 128
N_MEM = 256
MEM_HEADS = 4
MEM_HD = 128
MEM_W = MEM_HEADS * MEM_HD
IN_A = GDN_QKV + GDN_VW + 2 * GDN_HEADS + MEM_W
IN_B = (SWA_HEADS + 2 * SWA_KV_HEADS) * SWA_HD + MEM_W
SPLIT_A = (GDN_QKV, GDN_QKV + GDN_VW, GDN_QKV + GDN_VW + GDN_HEADS, GDN_QKV + GDN_VW + 2 * GDN_HEADS)
SPLIT_B = (SWA_HEADS * SWA_HD, (SWA_HEADS + SWA_KV_HEADS) * SWA_HD, (SWA_HEADS + 2 * SWA_KV_HEADS) * SWA_HD)
MIX_A = GDN_VW + MEM_W
MIX_B = SWA_HEADS * SWA_HD + MEM_W
D_FF = -(-8 * D_MODEL // (3 * 256)) * 256

kernel_name = 'hybrid_gdn_swa_stream_step'


def rms_norm(x, g):
    xf = x.astype(jnp.float32)
    y = xf * lax.rsqrt(jnp.mean(xf * xf, axis=-1, keepdims=True) + EPS)
    return (y * g.astype(jnp.float32)).astype(x.dtype)


def l2_norm(x):
    return x * lax.rsqrt(jnp.sum(x * x, axis=-1, keepdims=True) + EPS)


def causal_dwconv(x, conv_state, w):
    t = x.shape[1]
    xp = jnp.concatenate([conv_state.astype(x.dtype), x], axis=1)
    y = xp[:, 0:t] * w[0]
    for i in range(1, GDN_CONV):
        y = y + xp[:, i:i + t] * w[i]
    return y, xp[:, xp.shape[1] - (GDN_CONV - 1):]


def gated_delta_rule(q, k, v, g, beta, s0):
    b, t, h, _ = q.shape
    dv = v.shape[-1]
    c = min(CHUNK, t)
    n = t // c
    to_blk = lambda a: jnp.moveaxis(a.reshape((b, n, c, h) + a.shape[3:]), 3, 1)
    qb, kb, vb = to_blk(q), to_blk(k), to_blk(v)
    gb, bb = to_blk(g), to_blk(beta)
    cum = jnp.cumsum(gb, axis=-1)
    causal = jnp.tril(jnp.ones((c, c), dtype=bool))
    strict = jnp.tril(jnp.ones((c, c), dtype=bool), -1)
    decay = jnp.exp(jnp.where(causal, cum[..., :, None] - cum[..., None, :], -jnp.inf))
    k_beta = kb * bb[..., None]
    a_low = jnp.where(strict, jnp.einsum('bhnid,bhnjd->bhnij', k_beta, kb) * decay, 0.0)
    rhs = jnp.concatenate([vb * bb[..., None], k_beta * jnp.exp(cum)[..., None]], axis=-1)
    sol = lax.linalg.triangular_solve(a_low + jnp.eye(c, dtype=jnp.float32), rhs,
                                      left_side=True, lower=True, unit_diagonal=True)
    u_base, w_cum = sol[..., :dv], sol[..., dv:]
    qk = jnp.einsum('bhnid,bhnjd->bhnij', qb, kb) * decay
    q_dec = qb * jnp.exp(cum)[..., None]
    k_dec = kb * jnp.exp(cum[..., -1:] - cum)[..., None]
    a_chunk = jnp.exp(cum[..., -1])

    def step(s, xs):
        u0, wc, qkc, qd, kd, ac = xs
        u = u0 - jnp.einsum('bhck,bhkv->bhcv', wc, s)
        o = jnp.einsum('bhck,bhkv->bhcv', qd, s) + jnp.einsum('bhij,bhjv->bhiv', qkc, u)
        s = s * ac[..., None, None] + jnp.einsum('bhck,bhcv->bhkv', kd, u)
        return s, o

    xs = tuple(jnp.moveaxis(a, 2, 0) for a in (u_base, w_cum, qk, q_dec, k_dec, a_chunk))
    s_fin, o = lax.scan(step, s0, xs)
    o = jnp.transpose(o, (1, 0, 3, 2, 4)).reshape(b, t, h, dv)
    return o, s_fin


def gdn_mixer(qkv, z, b_raw, a_raw, conv_state, s0, conv_w, a_log, dt_bias, o_gain):
    bsz, t, _ = qkv.shape
    qkv, conv_new = causal_dwconv(qkv, conv_state, conv_w)
    qkv = jax.nn.silu(qkv).astype(jnp.float32)
    q, k, v = jnp.split(qkv, [GDN_HEADS * GDN_DK, 2 * GDN_HEADS * GDN_DK], axis=-1)
    q = l2_norm(q.reshape(bsz, t, GDN_HEADS, GDN_DK)) * (GDN_DK ** -0.5)
    k = l2_norm(k.reshape(bsz, t, GDN_HEADS, GDN_DK))
    v = v.reshape(bsz, t, GDN_HEADS, GDN_DV)
    beta = jax.nn.sigmoid(b_raw.astype(jnp.float32))
    g = -jnp.exp(a_log.astype(jnp.float32)) * jax.nn.softplus(a_raw.astype(jnp.float32) + dt_bias.astype(jnp.float32))
    o, s_new = gated_delta_rule(q, k, v, g, beta, s0.astype(jnp.float32))
    o = rms_norm(o.astype(z.dtype), o_gain) * jax.nn.silu(z.reshape(bsz, t, GDN_HEADS, GDN_DV))
    return o.reshape(bsz, t, GDN_VW), conv_new, s_new.astype(s0.dtype)


def swa_mixer(q, k, v, k_hist, v_hist, hist_valid, q_gain, k_gain, sinks):
    bsz, t, _ = q.shape
    grp = SWA_HEADS // SWA_KV_HEADS
    q = rms_norm(q.reshape(bsz, t, SWA_HEADS, SWA_HD), q_gain)
    k = rms_norm(k.reshape(bsz, t, SWA_KV_HEADS, SWA_HD), k_gain)
    v = v.reshape(bsz, t, SWA_KV_HEADS, SWA_HD)
    lh = k_hist.shape[1]
    c = min(CHUNK, t)
    n = t // c
    kp = jnp.concatenate([k_hist.astype(k.dtype), k], axis=1)
    vp = jnp.concatenate([v_hist.astype(v.dtype), v], axis=1)
    idx = (jnp.arange(n) * c)[:, None] + jnp.arange(lh + c)[None, :]
    kb, vb = kp[:, idx], vp[:, idx]
    valid = idx >= (0 if hist_valid else lh)
    qb = q.reshape(bsz, n, c, SWA_KV_HEADS, grp, SWA_HD)
    s = jnp.einsum('bnqhgd,bnkhd->bnhgqk', qb, kb).astype(jnp.float32) * (SWA_HD ** -0.5)
    s = jnp.where(valid[None, :, None, None, None, :], s, -jnp.inf)
    sink = sinks.astype(jnp.float32).reshape(1, 1, SWA_KV_HEADS, grp, 1, 1)
    m = jnp.maximum(jnp.max(s, axis=-1, keepdims=True), sink)
    p = jnp.exp(s - m)
    p = p / (jnp.sum(p, axis=-1, keepdims=True) + jnp.exp(sink - m))
    o = jnp.einsum('bnhgqk,bnkhd->bnqhgd', p.astype(v.dtype), vb)
    return o.reshape(bsz, t, SWA_HEADS * SWA_HD), k, v


def memory_kv(mem, norm_g, w_kv, k_gain):
    bsz, nm, _ = mem.shape
    kv = rms_norm(mem, norm_g) @ w_kv
    mk, mv = jnp.split(kv, 2, axis=-1)
    mk = rms_norm(mk.reshape(bsz, nm, MEM_HEADS, MEM_HD), k_gain)
    return mk, mv.reshape(bsz, nm, MEM_HEADS, MEM_HD)


def memory_attend(qm, mk, mv, q_gain):
    bsz, t, _ = qm.shape
    qm = rms_norm(qm.reshape(bsz, t, MEM_HEADS, MEM_HD), q_gain)
    s = jnp.einsum('bqhd,bkhd->bhqk', qm, mk.astype(qm.dtype)).astype(jnp.float32) * (MEM_HD ** -0.5)
    p = jax.nn.softmax(s, axis=-1).astype(qm.dtype)
    return jnp.einsum('bhqk,bkhd->bqhd', p, mv.astype(qm.dtype)).reshape(bsz, t, MEM_W)


def trunk(x, mem_k, mem_v, gdn_conv, gdn_state, swa_k_hist, swa_v_hist, hist_valid, p):
    conv_out, state_out, k_out, v_out = [], [], [], []
    for i in range(DEPTH):
        j = i // N_MIXERS
        h = rms_norm(x, p['norm_mix'][i])
        if i % N_MIXERS == 0:
            proj = h @ p['w_in_a'][j]
            qkv, z, b_raw, a_raw, qm = jnp.split(proj, SPLIT_A, axis=-1)
            o_mix, conv_new, s_new = gdn_mixer(qkv, z, b_raw, a_raw, gdn_conv[j], gdn_state[j], p['conv_w_a'][j],
                                               p['a_log'][j], p['dt_bias'][j], p['o_norm_a'][j])
            conv_out.append(conv_new)
            state_out.append(s_new)
            w_out = p['w_out_a'][j]
        else:
            proj = h @ p['w_in_b'][j]
            q, k, v, qm = jnp.split(proj, SPLIT_B, axis=-1)
            o_mix, k_new, v_new = swa_mixer(q, k, v, swa_k_hist[j], swa_v_hist[j], hist_valid,
                                            p['q_norm_b'][j], p['k_norm_b'][j], p['sinks_b'][j])
            k_out.append(k_new)
            v_out.append(v_new)
            w_out = p['w_out_b'][j]
        o_mem = memory_attend(qm, mem_k[i], mem_v[i], p['mem_q_norm'][i])
        x = x + jnp.concatenate([o_mix, o_mem.astype(o_mix.dtype)], axis=-1) @ w_out
        h = rms_norm(x, p['norm_ffn'][i])
        gate, up = jnp.split(h @ p['w_gate_up'][i], 2, axis=-1)
        x = x + (jax.nn.silu(gate) * up) @ p['w_down'][i]
    return x, jnp.stack(conv_out), jnp.stack(state_out), jnp.stack(k_out), jnp.stack(v_out)


def setup_inputs(seed: int = 0) -> dict:
    key = jax.random.key(seed)
    ks = iter(jax.random.split(key, 32))
    nrm = lambda shape, scale: jax.random.normal(next(ks), shape, jnp.float32) * scale
    gain = lambda shape: 1.0 + nrm(shape, 0.02)
    swa_rows = min(WINDOW, PAST_LEN)
    dt0 = jnp.exp(jax.random.uniform(next(ks), (N_A_LAYERS, GDN_HEADS), jnp.float32, math.log(1e-3), math.log(1e-1)))
    return {
        'x_prompt': nrm((BATCH, SEQ, D_MODEL), 1.0),
        'x_sample': nrm((DEC_BATCH, DEC_SEQ, D_MODEL), 1.0),
        'mem_prompt': nrm((BATCH, N_MEM, D_MODEL), 1.0),
        'cache_mem_k': nrm((DEPTH, DEC_BATCH, N_MEM, MEM_HEADS, MEM_HD), 1.0),
        'cache_mem_v': nrm((DEPTH, DEC_BATCH, N_MEM, MEM_HEADS, MEM_HD), 1.0),
        'state_gdn': nrm((N_A_LAYERS, DEC_BATCH, GDN_HEADS, GDN_DK, GDN_DV), 0.1),
        'state_gdn_conv': nrm((N_A_LAYERS, DEC_BATCH, GDN_CONV - 1, GDN_QKV), 1.0),
        'cache_swa_k': nrm((N_B_LAYERS, DEC_BATCH, swa_rows, SWA_KV_HEADS, SWA_HD), 1.0),
        'cache_swa_v': nrm((N_B_LAYERS, DEC_BATCH, swa_rows, SWA_KV_HEADS, SWA_HD), 1.0),
        'norm_mix': gain((DEPTH, D_MODEL)),
        'norm_ffn': gain((DEPTH, D_MODEL)),
        'mem_norm': gain((DEPTH, D_MODEL)),
        'w_mem_kv': nrm((DEPTH, D_MODEL, 2 * MEM_W), D_MODEL ** -0.5),
        'mem_q_norm': gain((DEPTH, MEM_HD)),
        'mem_k_norm': gain((DEPTH, MEM_HD)),
        'w_in_a': nrm((N_A_LAYERS, D_MODEL, IN_A), D_MODEL ** -0.5),
        'conv_w_a': nrm((N_A_LAYERS, GDN_CONV, GDN_QKV), GDN_CONV ** -0.5),
        'a_log': jnp.log(jax.random.uniform(next(ks), (N_A_LAYERS, GDN_HEADS), jnp.float32, 1.0, 16.0)),
        'dt_bias': dt0 + jnp.log(-jnp.expm1(-dt0)),
        'o_norm_a': gain((N_A_LAYERS, GDN_DV)),
        'w_out_a': nrm((N_A_LAYERS, MIX_A, D_MODEL), MIX_A ** -0.5),
        'w_in_b': nrm((N_B_LAYERS, D_MODEL, IN_B), D_MODEL ** -0.5),
        'q_norm_b': gain((N_B_LAYERS, SWA_HD)),
        'k_norm_b': gain((N_B_LAYERS, SWA_HD)),
        'sinks_b': nrm((N_B_LAYERS, SWA_HEADS), 1.0),
        'w_out_b': nrm((N_B_LAYERS, MIX_B, D_MODEL), MIX_B ** -0.5),
        'w_gate_up': nrm((DEPTH, D_MODEL, 2 * D_FF), D_MODEL ** -0.5),
        'w_down': nrm((DEPTH, D_FF, D_MODEL), D_FF ** -0.5),
    }


def reference(x_prompt, x_sample, mem_prompt, cache_mem_k, cache_mem_v, state_gdn, state_gdn_conv,
              cache_swa_k, cache_swa_v, norm_mix, norm_ffn, mem_norm, w_mem_kv, mem_q_norm, mem_k_norm,
              w_in_a, conv_w_a, a_log, dt_bias, o_norm_a, w_out_a, w_in_b, q_norm_b, k_norm_b, sinks_b,
              w_out_b, w_gate_up, w_down):
    p = {'norm_mix': norm_mix, 'norm_ffn': norm_ffn, 'mem_q_norm': mem_q_norm,
         'w_in_a': w_in_a, 'conv_w_a': conv_w_a, 'a_log': a_log, 'dt_bias': dt_bias,
         'o_norm_a': o_norm_a, 'w_out_a': w_out_a, 'w_in_b': w_in_b, 'q_norm_b': q_norm_b,
         'k_norm_b': k_norm_b, 'sinks_b': sinks_b, 'w_out_b': w_out_b,
         'w_gate_up': w_gate_up, 'w_down': w_down}
    bsz = x_prompt.shape[0]
    dt = x_prompt.dtype
    mk_list, mv_list = [], []
    for i in range(DEPTH):
        mk, mv = memory_kv(mem_prompt, mem_norm[i], w_mem_kv[i], mem_k_norm[i])
        mk_list.append(mk)
        mv_list.append(mv)
    new_mem_k_prompt = jnp.stack(mk_list)
    new_mem_v_prompt = jnp.stack(mv_list)
    zero_conv = jnp.zeros((N_A_LAYERS, bsz, GDN_CONV - 1, GDN_QKV), dt)
    zero_state = jnp.zeros((N_A_LAYERS, bsz, GDN_HEADS, GDN_DK, GDN_DV), dt)
    zero_hist = jnp.zeros((N_B_LAYERS, bsz, WINDOW, SWA_KV_HEADS, SWA_HD), dt)
    y_prompt, conv_p, state_p, k_p, v_p = trunk(x_prompt, new_mem_k_prompt, new_mem_v_prompt, zero_conv,
                                                zero_state, zero_hist, zero_hist, False, p)
    keep = min(WINDOW, x_prompt.shape[1])
    y_sample, conv_s, state_s, k_s, v_s = trunk(x_sample, cache_mem_k, cache_mem_v, state_gdn_conv, state_gdn,
                                                cache_swa_k, cache_swa_v, True, p)
    return (y_prompt, y_sample, state_p, conv_p, state_s, conv_s,
            k_p[:, :, -keep:], v_p[:, :, -keep:], k_s, v_s, new_mem_k_prompt, new_mem_v_prompt)
```

```python
import functools

import jax
import jax.numpy as jnp
from jax import lax
from jax.experimental import pallas as pl
from jax.experimental.pallas import tpu as pltpu

F32 = jnp.float32
BF16 = jnp.bfloat16

D_MODEL = 1024
CHUNK = 64
EPS = 1e-6
GDN_HEADS = 8
GDN_DK = 128
GDN_DV = 128
GDN_CONV = 4
GDN_QKV = GDN_HEADS * (2 * GDN_DK + GDN_DV)
GDN_VW = GDN_HEADS * GDN_DV
SWA_HEADS = 16
SWA_KV_HEADS = 4
SWA_HD = 64
SWA_GROUP = SWA_HEADS // SWA_KV_HEADS
WINDOW = 128
N_MEM = 256
MEM_HEADS = 4
MEM_HD = 128
MEM_W = MEM_HEADS * MEM_HD
LANES = 128
SUBLANES = 8
VMEM_LIMIT = 56 * 1024 * 1024

_HI = lax.Precision.HIGHEST


def _mm(a, b):
    return jnp.dot(a.astype(BF16), b.astype(BF16), preferred_element_type=F32)


def _mm_nt(a, b):
    return lax.dot_general(a.astype(BF16), b.astype(BF16), (((1,), (1,)), ((), ())),
                           preferred_element_type=F32)


def _mm_tn(a, b):
    return lax.dot_general(a.astype(BF16), b.astype(BF16), (((0,), (0,)), ((), ())),
                           preferred_element_type=F32)


def _rms(x, g):
    return x * lax.rsqrt(jnp.mean(x * x, axis=-1, keepdims=True) + EPS) * g


def _sigmoid(x):
    return 1.0 / (1.0 + jnp.exp(-x))


def _silu(x):
    return x * _sigmoid(x)


def _softplus(x):
    return jnp.maximum(x, 0.0) + jnp.log1p(jnp.exp(-jnp.abs(x)))


def _const_spec(shape):
    nd = len(shape)
    return pl.BlockSpec(shape, lambda *_: (0,) * nd, pipeline_mode=pl.Buffered(1))


def _params(sem):
    return pltpu.CompilerParams(dimension_semantics=sem, vmem_limit_bytes=VMEM_LIMIT)


def _memkv_body(mem_ref, g_ref, w_ref, kg_ref, mk_ref, mv_ref):
    h = _rms(mem_ref[...], g_ref[...])
    kv = _mm(h, w_ref[...])
    kg = kg_ref[...]
    for hd in range(MEM_HEADS):
        sl = slice(hd * MEM_HD, (hd + 1) * MEM_HD)
        mk_ref[:, sl] = _rms(kv[:, sl], kg)
    mv_ref[...] = kv[:, MEM_W:]


def _memkv_call(mem, mem_norm, w_mem_kv, mem_k_norm):
    depth = w_mem_kv.shape[0]
    out = jax.ShapeDtypeStruct((depth, N_MEM, MEM_W), F32)
    return pl.pallas_call(
        _memkv_body,
        grid=(depth,),
        in_specs=[
            pl.BlockSpec((N_MEM, D_MODEL), lambda i: (0, 0)),
            pl.BlockSpec((None, 1, D_MODEL), lambda i: (i, 0, 0)),
            pl.BlockSpec((None, D_MODEL, 2 * MEM_W), lambda i: (i, 0, 0)),
            pl.BlockSpec((None, 1, MEM_HD), lambda i: (i, 0, 0)),
        ],
        out_specs=[pl.BlockSpec((None, N_MEM, MEM_W), lambda i: (i, 0, 0))] * 2,
        out_shape=[out, out],
        compiler_params=_params(("arbitrary",)),
        name="memkv",
    )(mem, mem_norm.reshape(depth, 1, D_MODEL), w_mem_kv, mem_k_norm.reshape(depth, 1, MEM_HD))


def _proj_body(n_out, x_ref, g_ref, *refs):
    w_refs, o_refs = refs[:n_out], refs[n_out:]
    h = _rms(x_ref[...], g_ref[...]).astype(BF16)
    for w_ref, o_ref in zip(w_refs, o_refs):
        o_ref[...] = jnp.dot(h, w_ref[...], preferred_element_type=F32)


def _proj_call(x, g, ws, tm):
    rows = x.shape[0]
    n_out = len(ws)
    return pl.pallas_call(
        functools.partial(_proj_body, n_out),
        grid=(rows // tm,),
        in_specs=[pl.BlockSpec((tm, D_MODEL), lambda i: (i, 0)), _const_spec((1, D_MODEL))]
        + [_const_spec(w.shape) for w in ws],
        out_specs=[pl.BlockSpec((tm, w.shape[1]), lambda i: (i, 0)) for w in ws],
        out_shape=[jax.ShapeDtypeStruct((rows, w.shape[1]), F32) for w in ws],
        compiler_params=_params(("arbitrary",)),
        name="proj",
    )(x, g.reshape(1, D_MODEL), *ws)


def _gdn_body(qkv_ref, z_ref, ba_ref, conv0_ref, s0_ref, cw_ref, alog_ref, dtb_ref, og_ref,
              o_ref, convn_ref, sn_ref, xp_ref, s_ref):
    t = pl.program_id(1)
    c = CHUNK

    @pl.when(t == 0)
    def _():
        xp_ref[0:SUBLANES, :] = conv0_ref[...]
        s_ref[...] = s0_ref[...]

    xp_ref[SUBLANES:SUBLANES + c, :] = qkv_ref[...]

    row = lax.broadcasted_iota(jnp.int32, (c, c), 0)
    col = lax.broadcasted_iota(jnp.int32, (c, c), 1)
    causal = row >= col
    strict = row > col
    tril = causal.astype(F32)
    eye_l = (lax.broadcasted_iota(jnp.int32, (LANES, LANES), 0)
             == lax.broadcasted_iota(jnp.int32, (LANES, LANES), 1)).astype(F32)

    ba = ba_ref[...]
    beta_all = _sigmoid(ba)
    g_all = -jnp.exp(alog_ref[...]) * _softplus(ba + dtb_ref[...])
    cum_all = jnp.dot(tril, g_all, precision=_HI, preferred_element_type=F32)
    cum_t = lax.dot_general(eye_l, cum_all, (((1,), (1,)), ((), ())), precision=_HI,
                            preferred_element_type=F32)

    def conv_act(lo):
        acc = xp_ref[SUBLANES - 3:SUBLANES - 3 + c, lo:lo + LANES] * cw_ref[0:1, lo:lo + LANES]
        for i in range(1, GDN_CONV):
            acc = acc + (xp_ref[SUBLANES - 3 + i:SUBLANES - 3 + i + c, lo:lo + LANES]
                         * cw_ref[i:i + 1, lo:lo + LANES])
        return _silu(acc)

    og = og_ref[...]
    for h in range(GDN_HEADS):
        q = conv_act(h * GDN_DK)
        k = conv_act(GDN_HEADS * GDN_DK + h * GDN_DK)
        v = conv_act(2 * GDN_HEADS * GDN_DK + h * GDN_DV)
        q = q * lax.rsqrt(jnp.sum(q * q, axis=-1, keepdims=True) + EPS) * (GDN_DK ** -0.5)
        k = k * lax.rsqrt(jnp.sum(k * k, axis=-1, keepdims=True) + EPS)

        beta = beta_all[:, h:h + 1]
        cum_col = cum_all[:, GDN_HEADS + h:GDN_HEADS + h + 1]
        cum_row = cum_t[GDN_HEADS + h:GDN_HEADS + h + 1, :]
        tot = cum_all[c - 1:c, GDN_HEADS + h:GDN_HEADS + h + 1]
        decay = jnp.exp(jnp.where(causal, cum_col - cum_row, -jnp.inf))
        e_cum = jnp.exp(cum_col)

        kb = k * beta
        gram = _mm_nt(jnp.concatenate([kb, q], axis=0), k)
        a_neg = jnp.where(strict, -(gram[:c] * decay), 0.0)
        qk = gram[c:] * decay

        n_acc = a_neg
        p = a_neg
        for _ in range(5):
            p = _mm(p, p)
            n_acc = n_acc + p + _mm(n_acc, p)

        rhs = jnp.concatenate([v * beta, kb * e_cum], axis=1)
        sol = rhs + _mm(n_acc, rhs)
        u0, wc = sol[:, :GDN_DV], sol[:, GDN_DV:]
        qd = q * e_cum
        kd = k * jnp.exp(tot - cum_col)

        s = s_ref[h]
        r1 = _mm(jnp.concatenate([wc, qd], axis=0), s)
        u = u0 - r1[:c]
        o = r1[c:] + _mm(qk, u)
        s_ref[h] = s * jnp.exp(tot) + _mm_tn(kd, u)

        zg = _silu(z_ref[:, h * GDN_DV:(h + 1) * GDN_DV])
        o_ref[:, h * GDN_DV:(h + 1) * GDN_DV] = _rms(o, og) * zg

    tail = xp_ref[c:c + SUBLANES, :]
    xp_ref[0:SUBLANES, :] = tail
    convn_ref[...] = tail
    sn_ref[...] = s_ref[...]


def _gdn_call(qkv, z, ba, conv0, s0, conv_w, alog_row, dtb_row, o_gain):
    ns, t, _ = qkv.shape
    grid = (ns, t // CHUNK)
    seq = lambda width: pl.BlockSpec((None, CHUNK, width), lambda s, i: (s, i, 0))
    return pl.pallas_call(
        _gdn_body,
        grid=grid,
        in_specs=[
            seq(GDN_QKV), seq(GDN_VW), seq(LANES),
            pl.BlockSpec((None, SUBLANES, GDN_QKV), lambda s, i: (s, 0, 0)),
            pl.BlockSpec((None, GDN_HEADS, GDN_DK, GDN_DV), lambda s, i: (s, 0, 0, 0)),
            _const_spec((GDN_CONV, GDN_QKV)), _const_spec((1, LANES)), _const_spec((1, LANES)),
            _const_spec((1, GDN_DV)),
        ],
        out_specs=[
            seq(GDN_VW),
            pl.BlockSpec((None, SUBLANES, GDN_QKV), lambda s, i: (s, 0, 0)),
            pl.BlockSpec((None, GDN_HEADS, GDN_DK, GDN_DV), lambda s, i: (s, 0, 0, 0)),
        ],
        out_shape=[
            jax.ShapeDtypeStruct((ns, t, GDN_VW), F32),
            jax.ShapeDtypeStruct((ns, SUBLANES, GDN_QKV), F32),
            jax.ShapeDtypeStruct((ns, GDN_HEADS, GDN_DK, GDN_DV), F32),
        ],
        scratch_shapes=[
            pltpu.VMEM((SUBLANES + CHUNK, GDN_QKV), F32),
            pltpu.VMEM((GDN_HEADS, GDN_DK, GDN_DV), F32),
        ],
        compiler_params=_params(("arbitrary", "arbitrary")),
        name="gdn",
    )(qkv, z, ba, conv0, s0, conv_w, alog_row, dtb_row, o_gain.reshape(1, GDN_DV))


def _swa_body(hist_valid, q_ref, kv_ref, kh_ref, vh_ref, qg_ref, kg_ref, sink_ref,
              o_ref, kn_ref, kwin_ref, vwin_ref):
    t = pl.program_id(1)
    c = CHUNK
    kvw = SWA_KV_HEADS * SWA_HD

    @pl.when(t == 0)
    def _():
        kwin_ref[0:WINDOW, :] = kh_ref[...]
        vwin_ref[0:WINDOW, :] = vh_ref[...]

    qg = qg_ref[...] * (SWA_HD ** -0.5)
    kg = kg_ref[...]
    for h in range(SWA_KV_HEADS):
        sl = slice(h * SWA_HD, (h + 1) * SWA_HD)
        kwin_ref[WINDOW:WINDOW + c, sl] = _rms(kv_ref[:, sl], kg)
    vwin_ref[WINDOW:WINDOW + c, :] = kv_ref[:, kvw:]
    kn_ref[...] = kwin_ref[WINDOW:WINDOW + c, :]

    if not hist_valid:
        key_pos = lax.broadcasted_iota(jnp.int32, (SWA_GROUP * c, WINDOW + c), 1) + (t * c - WINDOW)
        key_ok = key_pos >= 0

    for h in range(SWA_KV_HEADS):
        sl = slice(h * SWA_HD, (h + 1) * SWA_HD)
        heads = [h * SWA_GROUP + g for g in range(SWA_GROUP)]
        qs = jnp.concatenate(
            [_rms(q_ref[:, hq * SWA_HD:(hq + 1) * SWA_HD], qg) for hq in heads], axis=0)
        s = _mm_nt(qs, kwin_ref[:, sl])
        if not hist_valid:
            s = jnp.where(key_ok, s, -jnp.inf)
        sink = jnp.concatenate([jnp.full((c, 1), sink_ref[0, hq], F32) for hq in heads], axis=0)
        m = jnp.maximum(jnp.max(s, axis=-1, keepdims=True), sink)
        p = jnp.exp(s - m)
        p = p / (jnp.sum(p, axis=-1, keepdims=True) + jnp.exp(sink - m))
        o = _mm(p, vwin_ref[:, sl])
        for g, hq in enumerate(heads):
            o_ref[:, hq * SWA_HD:(hq + 1) * SWA_HD] = o[g * c:(g + 1) * c]

    for ref in (kwin_ref, vwin_ref):
        ref[0:c, :] = ref[c:2 * c, :]
        ref[c:2 * c, :] = ref[2 * c:3 * c, :]


def _swa_call(q, kv, k_hist, v_hist, q_gain, k_gain, sinks, hist_valid):
    ns, t, _ = q.shape
    kvw = SWA_KV_HEADS * SWA_HD
    grid = (ns, t // CHUNK)
    return pl.pallas_call(
        functools.partial(_swa_body, hist_valid),
        grid=grid,
        in_specs=[
            pl.BlockSpec((None, CHUNK, SWA_HEADS * SWA_HD), lambda s, i: (s, i, 0)),
            pl.BlockSpec((None, CHUNK, 2 * kvw), lambda s, i: (s, i, 0)),
            pl.BlockSpec((None, WINDOW, kvw), lambda s, i: (s, 0, 0)),
            pl.BlockSpec((None, WINDOW, kvw), lambda s, i: (s, 0, 0)),
            _const_spec((1, SWA_HD)), _const_spec((1, SWA_HD)),
            pl.BlockSpec(memory_space=pltpu.SMEM),
        ],
        out_specs=[
            pl.BlockSpec((None, CHUNK, SWA_HEADS * SWA_HD), lambda s, i: (s, i, 0)),
            pl.BlockSpec((None, CHUNK, kvw), lambda s, i: (s, i, 0)),
        ],
        out_shape=[
            jax.ShapeDtypeStruct((ns, t, SWA_HEADS * SWA_HD), F32),
            jax.ShapeDtypeStruct((ns, t, kvw), F32),
        ],
        scratch_shapes=[pltpu.VMEM((WINDOW + CHUNK, kvw), F32)] * 2,
        compiler_params=_params(("arbitrary", "arbitrary")),
        name="swa",
    )(q, kv, k_hist, v_hist, q_gain.reshape(1, SWA_HD), k_gain.reshape(1, SWA_HD),
      sinks.reshape(1, SWA_HEADS))


def _tail_body(seqs, x_ref, om_ref, qm_ref, mk_ref, mv_ref, mqg_ref, wo_ref, gf_ref, wgu_ref,
               wd_ref, y_ref):
    tm = x_ref.shape[0]
    rows = tm // seqs
    d_ff = wd_ref.shape[0]
    mqg = mqg_ref[...] * (MEM_HD ** -0.5)

    mem_parts = []
    for sq in range(seqs):
        heads = []
        for hd in range(MEM_HEADS):
            sl = slice(hd * MEM_HD, (hd + 1) * MEM_HD)
            qh = _rms(qm_ref[sq * rows:(sq + 1) * rows, sl], mqg)
            s = _mm_nt(qh, mk_ref[sq, :, sl])
            p = jnp.exp(s - jnp.max(s, axis=-1, keepdims=True))
            p = p / jnp.sum(p, axis=-1, keepdims=True)
            heads.append(_mm(p, mv_ref[sq, :, sl]))
        mem_parts.append(jnp.concatenate(heads, axis=1))
    o_mem = mem_parts[0] if seqs == 1 else jnp.concatenate(mem_parts, axis=0)

    mix_w = om_ref.shape[1]
    x = x_ref[...] + _mm(om_ref[...], wo_ref[0:mix_w, :]) + _mm(o_mem, wo_ref[mix_w:, :])
    h = _rms(x, gf_ref[...])
    gu = _mm(h, wgu_ref[...])
    act = _silu(gu[:, :d_ff]) * gu[:, d_ff:]
    y_ref[...] = x + _mm(act, wd_ref[...])


def _tail_call(x, o_mix, qm, mk, mv, mem_q_gain, w_out, g_ffn, w_gate_up, w_down, tm, seqs):
    rows = x.shape[0]
    mix_w = o_mix.shape[1]
    tile = lambda width: pl.BlockSpec((tm, width), lambda i: (i, 0))
    tiles_per_seq = rows // (tm * mk.shape[0]) if seqs == 1 else 1
    mem = pl.BlockSpec((seqs, N_MEM, MEM_W), lambda i: (i // tiles_per_seq, 0, 0))
    return pl.pallas_call(
        functools.partial(_tail_body, seqs),
        grid=(rows // tm,),
        in_specs=[
            tile(D_MODEL), tile(mix_w), tile(MEM_W), mem, mem,
            _const_spec((1, MEM_HD)), _const_spec(w_out.shape), _const_spec((1, D_MODEL)),
            _const_spec(w_gate_up.shape), _const_spec(w_down.shape),
        ],
        out_specs=tile(D_MODEL),
        out_shape=jax.ShapeDtypeStruct((rows, D_MODEL), F32),
        compiler_params=_params(("arbitrary",)),
        name="tail",
    )(x, o_mix, qm, mk, mv, mem_q_gain.reshape(1, MEM_HD), w_out, g_ffn.reshape(1, D_MODEL),
      w_gate_up, w_down)


def _lane_row(vals, offset):
    return jnp.zeros((1, LANES), F32).at[0, offset:offset + vals.shape[0]].set(vals)


def _trunk(x, mem_k, mem_v, gdn_conv, gdn_state, swa_k_hist, swa_v_hist, hist_valid, w, tm, tail_seqs):
    ns, t, _ = x.shape
    rows = ns * t
    x2 = x.reshape(rows, D_MODEL)

    qkv, z, ba, qm = _proj_call(x2, w["norm_mix"][0], w["in_a"], tm)
    conv0 = jnp.pad(gdn_conv, ((0, 0), (SUBLANES - (GDN_CONV - 1), 0), (0, 0)))
    o_mix, conv_new, s_new = _gdn_call(
        qkv.reshape(ns, t, GDN_QKV), z.reshape(ns, t, GDN_VW), ba.reshape(ns, t, LANES), conv0,
        gdn_state, w["conv_w_a"], w["alog_row"], w["dtb_row"], w["o_norm_a"])
    x2 = _tail_call(x2, o_mix.reshape(rows, GDN_VW), qm, mem_k[0], mem_v[0], w["mem_q_norm"][0],
                    w["out_a"], w["norm_ffn"][0], w["gate_up"][0], w["down"][0], tm, tail_seqs)

    kvw = SWA_KV_HEADS * SWA_HD
    q, kv, qm = _proj_call(x2, w["norm_mix"][1], w["in_b"], tm)
    o_mix, k_new = _swa_call(q.reshape(ns, t, SWA_HEADS * SWA_HD), kv.reshape(ns, t, 2 * kvw),
                             swa_k_hist, swa_v_hist, w["q_norm_b"], w["k_norm_b"], w["sinks_b"],
                             hist_valid)
    v_new = kv.reshape(ns, t, 2 * kvw)[:, :, kvw:]
    x2 = _tail_call(x2, o_mix.reshape(rows, SWA_HEADS * SWA_HD), qm, mem_k[1], mem_v[1],
                    w["mem_q_norm"][1], w["out_b"], w["norm_ffn"][1], w["gate_up"][1], w["down"][1],
                    tm, tail_seqs)

    conv_new = conv_new[:, SUBLANES - (GDN_CONV - 1):, :]
    return x2.reshape(ns, t, D_MODEL), conv_new, s_new, k_new, v_new


def kernel(x_prompt, x_sample, mem_prompt, cache_mem_k, cache_mem_v, state_gdn, state_gdn_conv, cache_swa_k, cache_swa_v, norm_mix, norm_ffn, mem_norm, w_mem_kv, mem_q_norm, mem_k_norm, w_in_a, conv_w_a, a_log, dt_bias, o_norm_a, w_out_a, w_in_b, q_norm_b, k_norm_b, sinks_b, w_out_b, w_gate_up, w_down):
    bsz = x_prompt.shape[0]
    dec = x_sample.shape[0]
    kvw = SWA_KV_HEADS * SWA_HD
    wa = w_in_a[0]
    ba_cols = jnp.pad(wa[:, GDN_QKV + GDN_VW:GDN_QKV + GDN_VW + 2 * GDN_HEADS],
                      ((0, 0), (0, LANES - 2 * GDN_HEADS)))
    wb = w_in_b[0]
    w = {
        "norm_mix": norm_mix, "norm_ffn": norm_ffn, "mem_q_norm": mem_q_norm,
        "in_a": [wa[:, :GDN_QKV].astype(BF16), wa[:, GDN_QKV:GDN_QKV + GDN_VW].astype(BF16),
                 ba_cols.astype(BF16), wa[:, GDN_QKV + GDN_VW + 2 * GDN_HEADS:].astype(BF16)],
        "in_b": [wb[:, :SWA_HEADS * SWA_HD].astype(BF16),
                 wb[:, SWA_HEADS * SWA_HD:SWA_HEADS * SWA_HD + 2 * kvw].astype(BF16),
                 wb[:, SWA_HEADS * SWA_HD + 2 * kvw:].astype(BF16)],
        "conv_w_a": conv_w_a[0], "o_norm_a": o_norm_a[0],
        "alog_row": _lane_row(a_log[0], GDN_HEADS), "dtb_row": _lane_row(dt_bias[0], GDN_HEADS),
        "out_a": w_out_a[0].astype(BF16), "out_b": w_out_b[0].astype(BF16),
        "q_norm_b": q_norm_b[0], "k_norm_b": k_norm_b[0], "sinks_b": sinks_b[0],
        "gate_up": w_gate_up.astype(BF16), "down": w_down.astype(BF16),
    }

    mk, mv = _memkv_call(mem_prompt[0], mem_norm, w_mem_kv, mem_k_norm)
    depth = mk.shape[0]
    new_mem_k = mk.reshape(depth, bsz, N_MEM, MEM_HEADS, MEM_HD)
    new_mem_v = mv.reshape(depth, bsz, N_MEM, MEM_HEADS, MEM_HD)

    zero_conv = jnp.zeros((bsz, GDN_CONV - 1, GDN_QKV), F32)
    zero_state = jnp.zeros((bsz, GDN_HEADS, GDN_DK, GDN_DV), F32)
    zero_hist = jnp.zeros((bsz, WINDOW, kvw), F32)
    y_p, conv_p, state_p, k_p, v_p = _trunk(
        x_prompt, mk.reshape(depth, bsz, N_MEM, MEM_W), mv.reshape(depth, bsz, N_MEM, MEM_W),
        zero_conv, zero_state, zero_hist, zero_hist, False, w, 256, 1)

    y_s, conv_s, state_s, k_s, v_s = _trunk(
        x_sample, cache_mem_k.reshape(depth, dec, N_MEM, MEM_W),
        cache_mem_v.reshape(depth, dec, N_MEM, MEM_W), state_gdn_conv[0], state_gdn[0],
        cache_swa_k[0].reshape(dec, WINDOW, kvw), cache_swa_v[0].reshape(dec, WINDOW, kvw),
        True, w, 256, 256 // x_sample.shape[1])

    keep = min(WINDOW, x_prompt.shape[1])
    kv_shape = lambda a: a.reshape(a.shape[0], a.shape[1], SWA_KV_HEADS, SWA_HD)[None]
    return (y_p, y_s, state_p[None], conv_p[None], state_s[None], conv_s[None],
            kv_shape(k_p[:, -keep:]), kv_shape(v_p[:, -keep:]), kv_shape(k_s), kv_shape(v_s),
            new_mem_k, new_mem_v)
```

```python
import functools

import jax
import jax.numpy as jnp
from jax import lax
from jax.experimental import pallas as pl
from jax.experimental.pallas import tpu as pltpu

F32 = jnp.float32
BF16 = jnp.bfloat16

D_MODEL = 1024
CHUNK = 64
EPS = 1e-6
GDN_HEADS = 8
GDN_DK = 128
GDN_DV = 128
GDN_CONV = 4
GDN_QKV = GDN_HEADS * (2 * GDN_DK + GDN_DV)
GDN_VW = GDN_HEADS * GDN_DV
SWA_HEADS = 16
SWA_KV_HEADS = 4
SWA_HD = 64
SWA_GROUP = SWA_HEADS // SWA_KV_HEADS
WINDOW = 128
N_MEM = 256
MEM_HEADS = 4
MEM_HD = 128
MEM_W = MEM_HEADS * MEM_HD
LANES = 128
SUBLANES = 8
VMEM_LIMIT = 56 * 1024 * 1024
SWA_TILE_ROWS = 4 * CHUNK

_HI = lax.Precision.HIGHEST


def _mm(a, b):
    return jnp.dot(a.astype(BF16), b.astype(BF16), preferred_element_type=F32)


def _mm_nt(a, b):
    return lax.dot_general(a.astype(BF16), b.astype(BF16), (((1,), (1,)), ((), ())),
                           preferred_element_type=F32)


def _mm_tn(a, b):
    return lax.dot_general(a.astype(BF16), b.astype(BF16), (((0,), (0,)), ((), ())),
                           preferred_element_type=F32)


def _rms(x, g):
    return x * lax.rsqrt(jnp.mean(x * x, axis=-1, keepdims=True) + EPS) * g


def _sigmoid(x):
    return 1.0 / (1.0 + jnp.exp(-x))


def _silu(x):
    return x * _sigmoid(x)


def _softplus(x):
    return jnp.maximum(x, 0.0) + jnp.log1p(jnp.exp(-jnp.abs(x)))


def _const_spec(shape):
    nd = len(shape)
    return pl.BlockSpec(shape, lambda *_: (0,) * nd, pipeline_mode=pl.Buffered(1))


def _params(sem):
    return pltpu.CompilerParams(dimension_semantics=sem, vmem_limit_bytes=VMEM_LIMIT)


def _memkv_body(mem_ref, g_ref, w_ref, kg_ref, mk_ref, mv_ref):
    h = _rms(mem_ref[...], g_ref[...])
    kv = _mm(h, w_ref[...])
    kg = kg_ref[...]
    for hd in range(MEM_HEADS):
        sl = slice(hd * MEM_HD, (hd + 1) * MEM_HD)
        mk_ref[:, sl] = _rms(kv[:, sl], kg)
    mv_ref[...] = kv[:, MEM_W:]


def _memkv_call(mem, mem_norm, w_mem_kv, mem_k_norm):
    depth = w_mem_kv.shape[0]
    out = jax.ShapeDtypeStruct((depth, N_MEM, MEM_W), F32)
    return pl.pallas_call(
        _memkv_body,
        grid=(depth,),
        in_specs=[
            pl.BlockSpec((N_MEM, D_MODEL), lambda i: (0, 0)),
            pl.BlockSpec((None, 1, D_MODEL), lambda i: (i, 0, 0)),
            pl.BlockSpec((None, D_MODEL, 2 * MEM_W), lambda i: (i, 0, 0)),
            pl.BlockSpec((None, 1, MEM_HD), lambda i: (i, 0, 0)),
        ],
        out_specs=[pl.BlockSpec((None, N_MEM, MEM_W), lambda i: (i, 0, 0))] * 2,
        out_shape=[out, out],
        compiler_params=_params(("arbitrary",)),
        name="memkv",
    )(mem, mem_norm.reshape(depth, 1, D_MODEL), w_mem_kv, mem_k_norm.reshape(depth, 1, MEM_HD))


def _proj_body(n_out, x_ref, g_ref, *refs):
    w_refs, o_refs = refs[:n_out], refs[n_out:]
    h = _rms(x_ref[...], g_ref[...]).astype(BF16)
    for w_ref, o_ref in zip(w_refs, o_refs):
        o_ref[...] = jnp.dot(h, w_ref[...], preferred_element_type=F32)


def _proj_call(x, g, ws, tm):
    rows = x.shape[0]
    n_out = len(ws)
    return pl.pallas_call(
        functools.partial(_proj_body, n_out),
        grid=(rows // tm,),
        in_specs=[pl.BlockSpec((tm, D_MODEL), lambda i: (i, 0)), _const_spec((1, D_MODEL))]
        + [_const_spec(w.shape) for w in ws],
        out_specs=[pl.BlockSpec((tm, w.shape[1]), lambda i: (i, 0)) for w in ws],
        out_shape=[jax.ShapeDtypeStruct((rows, w.shape[1]), F32) for w in ws],
        compiler_params=_params(("arbitrary",)),
        name="proj",
    )(x, g.reshape(1, D_MODEL), *ws)


def _gdn_body(qkv_ref, z_ref, ba_ref, conv0_ref, s0_ref, cw_ref, alog_ref, dtb_ref, og_ref,
              o_ref, convn_ref, sn_ref, xp_ref, s_ref):
    t = pl.program_id(1)
    c = CHUNK

    @pl.when(t == 0)
    def _():
        xp_ref[0:SUBLANES, :] = conv0_ref[...]
        s_ref[...] = s0_ref[...]

    xp_ref[SUBLANES:SUBLANES + c, :] = qkv_ref[...]

    row = lax.broadcasted_iota(jnp.int32, (c, c), 0)
    col = lax.broadcasted_iota(jnp.int32, (c, c), 1)
    causal = row >= col
    strict = row > col
    tril = causal.astype(F32)
    eye_l = (lax.broadcasted_iota(jnp.int32, (LANES, LANES), 0)
             == lax.broadcasted_iota(jnp.int32, (LANES, LANES), 1)).astype(F32)

    ba = ba_ref[...]
    beta_all = _sigmoid(ba)
    g_all = -jnp.exp(alog_ref[...]) * _softplus(ba + dtb_ref[...])
    cum_all = jnp.dot(tril, g_all, precision=_HI, preferred_element_type=F32)
    cum_t = lax.dot_general(eye_l, cum_all, (((1,), (1,)), ((), ())), precision=_HI,
                            preferred_element_type=F32)

    def conv_act(lo):
        acc = xp_ref[SUBLANES - 3:SUBLANES - 3 + c, lo:lo + LANES] * cw_ref[0:1, lo:lo + LANES]
        for i in range(1, GDN_CONV):
            acc = acc + (xp_ref[SUBLANES - 3 + i:SUBLANES - 3 + i + c, lo:lo + LANES]
                         * cw_ref[i:i + 1, lo:lo + LANES])
        return _silu(acc)

    heads = range(GDN_HEADS)
    qs, ks, vs = [], [], []
    for h in heads:
        q = conv_act(h * GDN_DK)
        k = conv_act(GDN_HEADS * GDN_DK + h * GDN_DK)
        qs.append(q * lax.rsqrt(jnp.sum(q * q, axis=-1, keepdims=True) + EPS) * (GDN_DK ** -0.5))
        ks.append(k * lax.rsqrt(jnp.sum(k * k, axis=-1, keepdims=True) + EPS))
        vs.append(conv_act(2 * GDN_HEADS * GDN_DK + h * GDN_DV))

    betas = [beta_all[:, h:h + 1] for h in heads]
    cum_cols = [cum_all[:, GDN_HEADS + h:GDN_HEADS + h + 1] for h in heads]
    tots = [cum_all[c - 1:c, GDN_HEADS + h:GDN_HEADS + h + 1] for h in heads]
    decays = [jnp.exp(jnp.where(causal, cum_cols[h] - cum_t[GDN_HEADS + h:GDN_HEADS + h + 1, :],
                                -jnp.inf)) for h in heads]
    e_cums = [jnp.exp(cc) for cc in cum_cols]
    kbs = [ks[h] * betas[h] for h in heads]

    grams = [_mm_nt(jnp.concatenate([kbs[h], qs[h]], axis=0), ks[h]) for h in heads]
    ps = [jnp.where(strict, -(grams[h][:c] * decays[h]), 0.0) for h in heads]
    qks = [grams[h][c:] * decays[h] for h in heads]

    ns = ps
    for _ in range(5):
        ps = [_mm(p, p) for p in ps]
        ns = [n + p + _mm(n, p) for n, p in zip(ns, ps)]

    rhss = [jnp.concatenate([vs[h] * betas[h], kbs[h] * e_cums[h]], axis=1) for h in heads]
    sols = [rhs + _mm(n, rhs) for n, rhs in zip(ns, rhss)]
    qds = [qs[h] * e_cums[h] for h in heads]
    kds = [ks[h] * jnp.exp(tots[h] - cum_cols[h]) for h in heads]

    ss = [s_ref[h] for h in heads]
    r1s = [_mm(jnp.concatenate([sols[h][:, GDN_DV:], qds[h]], axis=0), ss[h]) for h in heads]
    us = [sols[h][:, :GDN_DV] - r1s[h][:c] for h in heads]
    os_ = [r1s[h][c:] + _mm(qks[h], us[h]) for h in heads]
    for h in heads:
        s_ref[h] = ss[h] * jnp.exp(tots[h]) + _mm_tn(kds[h], us[h])

    og = og_ref[...]
    for h in heads:
        zg = _silu(z_ref[:, h * GDN_DV:(h + 1) * GDN_DV])
        o_ref[:, h * GDN_DV:(h + 1) * GDN_DV] = _rms(os_[h], og) * zg

    tail = xp_ref[c:c + SUBLANES, :]
    xp_ref[0:SUBLANES, :] = tail
    convn_ref[...] = tail
    sn_ref[...] = s_ref[...]


def _gdn_call(qkv, z, ba, conv0, s0, conv_w, alog_row, dtb_row, o_gain):
    ns, t, _ = qkv.shape
    grid = (ns, t // CHUNK)
    seq = lambda width: pl.BlockSpec((None, CHUNK, width), lambda s, i: (s, i, 0))
    return pl.pallas_call(
        _gdn_body,
        grid=grid,
        in_specs=[
            seq(GDN_QKV), seq(GDN_VW), seq(LANES),
            pl.BlockSpec((None, SUBLANES, GDN_QKV), lambda s, i: (s, 0, 0)),
            pl.BlockSpec((None, GDN_HEADS, GDN_DK, GDN_DV), lambda s, i: (s, 0, 0, 0)),
            _const_spec((GDN_CONV, GDN_QKV)), _const_spec((1, LANES)), _const_spec((1, LANES)),
            _const_spec((1, GDN_DV)),
        ],
        out_specs=[
            seq(GDN_VW),
            pl.BlockSpec((None, SUBLANES, GDN_QKV), lambda s, i: (s, 0, 0)),
            pl.BlockSpec((None, GDN_HEADS, GDN_DK, GDN_DV), lambda s, i: (s, 0, 0, 0)),
        ],
        out_shape=[
            jax.ShapeDtypeStruct((ns, t, GDN_VW), F32),
            jax.ShapeDtypeStruct((ns, SUBLANES, GDN_QKV), F32),
            jax.ShapeDtypeStruct((ns, GDN_HEADS, GDN_DK, GDN_DV), F32),
        ],
        scratch_shapes=[
            pltpu.VMEM((SUBLANES + CHUNK, GDN_QKV), F32),
            pltpu.VMEM((GDN_HEADS, GDN_DK, GDN_DV), F32),
        ],
        compiler_params=_params(("arbitrary", "arbitrary")),
        name="gdn",
    )(qkv, z, ba, conv0, s0, conv_w, alog_row, dtb_row, o_gain.reshape(1, GDN_DV))


def _swa_body(hist_valid, q_ref, kv_ref, kh_ref, vh_ref, qg_ref, kg_ref, sink_ref,
              o_ref, kn_ref, kwin_ref, vwin_ref):
    t = pl.program_id(1)
    c = CHUNK
    tr = q_ref.shape[0]
    kvw = SWA_KV_HEADS * SWA_HD

    @pl.when(t == 0)
    def _():
        kwin_ref[0:WINDOW, :] = kh_ref[...]
        vwin_ref[0:WINDOW, :] = vh_ref[...]

    qg = qg_ref[...] * (SWA_HD ** -0.5)
    kg = kg_ref[...]
    kv_heads = range(SWA_KV_HEADS)
    lanes = [slice(h * SWA_HD, (h + 1) * SWA_HD) for h in kv_heads]
    groups = [[h * SWA_GROUP + g for g in range(SWA_GROUP)] for h in kv_heads]
    for h in kv_heads:
        kwin_ref[WINDOW:WINDOW + tr, lanes[h]] = _rms(kv_ref[:, lanes[h]], kg)
    vwin_ref[WINDOW:WINDOW + tr, :] = kv_ref[:, kvw:]
    kn_ref[...] = kwin_ref[WINDOW:WINDOW + tr, :]
    qn = [_rms(q_ref[:, hq * SWA_HD:(hq + 1) * SWA_HD], qg) for hq in range(SWA_HEADS)]
    sinks = [jnp.concatenate([jnp.full((c, 1), sink_ref[0, hq], F32) for hq in groups[h]], axis=0)
             for h in kv_heads]

    pairs = [(j, h) for j in range(tr // c) for h in kv_heads]
    qss = [jnp.concatenate([qn[hq][j * c:(j + 1) * c] for hq in groups[h]], axis=0)
           for j, h in pairs]
    ss = [_mm_nt(qs, kwin_ref[j * c:j * c + WINDOW + c, lanes[h]])
          for qs, (j, h) in zip(qss, pairs)]
    if not hist_valid:
        key_col = lax.broadcasted_iota(jnp.int32, (SWA_GROUP * c, WINDOW + c), 1)
        ss = [jnp.where(key_col + (t * tr + j * c - WINDOW) >= 0, s, -jnp.inf)
              if j * c < WINDOW else s for s, (j, h) in zip(ss, pairs)]
    ps = []
    for s, (j, h) in zip(ss, pairs):
        m = jnp.maximum(jnp.max(s, axis=-1, keepdims=True), sinks[h])
        p = jnp.exp(s - m)
        ps.append(p / (jnp.sum(p, axis=-1, keepdims=True) + jnp.exp(sinks[h] - m)))
    os_ = [_mm(p, vwin_ref[j * c:j * c + WINDOW + c, lanes[h]]) for p, (j, h) in zip(ps, pairs)]
    for o, (j, h) in zip(os_, pairs):
        for g, hq in enumerate(groups[h]):
            o_ref[j * c:(j + 1) * c, hq * SWA_HD:(hq + 1) * SWA_HD] = o[g * c:(g + 1) * c]

    for ref in (kwin_ref, vwin_ref):
        ref[0:WINDOW, :] = ref[tr:tr + WINDOW, :]


def _swa_call(q, kv, k_hist, v_hist, q_gain, k_gain, sinks, hist_valid, tr):
    ns, t, _ = q.shape
    kvw = SWA_KV_HEADS * SWA_HD
    grid = (ns, t // tr)
    return pl.pallas_call(
        functools.partial(_swa_body, hist_valid),
        grid=grid,
        in_specs=[
            pl.BlockSpec((None, tr, SWA_HEADS * SWA_HD), lambda s, i: (s, i, 0)),
            pl.BlockSpec((None, tr, 2 * kvw), lambda s, i: (s, i, 0)),
            pl.BlockSpec((None, WINDOW, kvw), lambda s, i: (s, 0, 0)),
            pl.BlockSpec((None, WINDOW, kvw), lambda s, i: (s, 0, 0)),
            _const_spec((1, SWA_HD)), _const_spec((1, SWA_HD)),
            pl.BlockSpec(memory_space=pltpu.SMEM),
        ],
        out_specs=[
            pl.BlockSpec((None, tr, SWA_HEADS * SWA_HD), lambda s, i: (s, i, 0)),
            pl.BlockSpec((None, tr, kvw), lambda s, i: (s, i, 0)),
        ],
        out_shape=[
            jax.ShapeDtypeStruct((ns, t, SWA_HEADS * SWA_HD), F32),
            jax.ShapeDtypeStruct((ns, t, kvw), F32),
        ],
        scratch_shapes=[pltpu.VMEM((WINDOW + tr, kvw), F32)] * 2,
        compiler_params=_params(("arbitrary", "arbitrary")),
        name="swa",
    )(q, kv, k_hist, v_hist, q_gain.reshape(1, SWA_HD), k_gain.reshape(1, SWA_HD),
      sinks.reshape(1, SWA_HEADS))


def _tail_body(seqs, x_ref, om_ref, qm_ref, mk_ref, mv_ref, mqg_ref, wo_ref, gf_ref, wgu_ref,
               wd_ref, y_ref):
    tm = x_ref.shape[0]
    rows = tm // seqs
    d_ff = wd_ref.shape[0]
    mqg = mqg_ref[...] * (MEM_HD ** -0.5)

    mem_parts = []
    for sq in range(seqs):
        heads = []
        for hd in range(MEM_HEADS):
            sl = slice(hd * MEM_HD, (hd + 1) * MEM_HD)
            qh = _rms(qm_ref[sq * rows:(sq + 1) * rows, sl], mqg)
            s = _mm_nt(qh, mk_ref[sq, :, sl])
            p = jnp.exp(s - jnp.max(s, axis=-1, keepdims=True))
            p = p / jnp.sum(p, axis=-1, keepdims=True)
            heads.append(_mm(p, mv_ref[sq, :, sl]))
        mem_parts.append(jnp.concatenate(heads, axis=1))
    o_mem = mem_parts[0] if seqs == 1 else jnp.concatenate(mem_parts, axis=0)

    mix_w = om_ref.shape[1]
    x = x_ref[...] + _mm(om_ref[...], wo_ref[0:mix_w, :]) + _mm(o_mem, wo_ref[mix_w:, :])
    h = _rms(x, gf_ref[...])
    gu = _mm(h, wgu_ref[...])
    act = _silu(gu[:, :d_ff]) * gu[:, d_ff:]
    y_ref[...] = x + _mm(act, wd_ref[...])


def _tail_call(x, o_mix, qm, mk, mv, mem_q_gain, w_out, g_ffn, w_gate_up, w_down, tm, seqs):
    rows = x.shape[0]
    mix_w = o_mix.shape[1]
    tile = lambda width: pl.BlockSpec((tm, width), lambda i: (i, 0))
    tiles_per_seq = rows // (tm * mk.shape[0]) if seqs == 1 else 1
    mem = pl.BlockSpec((seqs, N_MEM, MEM_W), lambda i: (i // tiles_per_seq, 0, 0))
    return pl.pallas_call(
        functools.partial(_tail_body, seqs),
        grid=(rows // tm,),
        in_specs=[
            tile(D_MODEL), tile(mix_w), tile(MEM_W), mem, mem,
            _const_spec((1, MEM_HD)), _const_spec(w_out.shape), _const_spec((1, D_MODEL)),
            _const_spec(w_gate_up.shape), _const_spec(w_down.shape),
        ],
        out_specs=tile(D_MODEL),
        out_shape=jax.ShapeDtypeStruct((rows, D_MODEL), F32),
        compiler_params=_params(("arbitrary",)),
        name="tail",
    )(x, o_mix, qm, mk, mv, mem_q_gain.reshape(1, MEM_HD), w_out, g_ffn.reshape(1, D_MODEL),
      w_gate_up, w_down)


def _lane_row(vals, offset):
    return jnp.zeros((1, LANES), F32).at[0, offset:offset + vals.shape[0]].set(vals)


def _trunk(x, mem_k, mem_v, gdn_conv, gdn_state, swa_k_hist, swa_v_hist, hist_valid, w, tm, tail_seqs):
    ns, t, _ = x.shape
    rows = ns * t
    x2 = x.reshape(rows, D_MODEL)

    qkv, z, ba, qm = _proj_call(x2, w["norm_mix"][0], w["in_a"], tm)
    conv0 = jnp.pad(gdn_conv, ((0, 0), (SUBLANES - (GDN_CONV - 1), 0), (0, 0)))
    o_mix, conv_new, s_new = _gdn_call(
        qkv.reshape(ns, t, GDN_QKV), z.reshape(ns, t, GDN_VW), ba.reshape(ns, t, LANES), conv0,
        gdn_state, w["conv_w_a"], w["alog_row"], w["dtb_row"], w["o_norm_a"])
    x2 = _tail_call(x2, o_mix.reshape(rows, GDN_VW), qm, mem_k[0], mem_v[0], w["mem_q_norm"][0],
                    w["out_a"], w["norm_ffn"][0], w["gate_up"][0], w["down"][0], tm, tail_seqs)

    kvw = SWA_KV_HEADS * SWA_HD
    q, kv, qm = _proj_call(x2, w["norm_mix"][1], w["in_b"], tm)
    o_mix, k_new = _swa_call(q.reshape(ns, t, SWA_HEADS * SWA_HD), kv.reshape(ns, t, 2 * kvw),
                             swa_k_hist, swa_v_hist, w["q_norm_b"], w["k_norm_b"], w["sinks_b"],
                             hist_valid, min(t, SWA_TILE_ROWS))
    v_new = kv.reshape(ns, t, 2 * kvw)[:, :, kvw:]
    x2 = _tail_call(x2, o_mix.reshape(rows, SWA_HEADS * SWA_HD), qm, mem_k[1], mem_v[1],
                    w["mem_q_norm"][1], w["out_b"], w["norm_ffn"][1], w["gate_up"][1], w["down"][1],
                    tm, tail_seqs)

    conv_new = conv_new[:, SUBLANES - (GDN_CONV - 1):, :]
    return x2.reshape(ns, t, D_MODEL), conv_new, s_new, k_new, v_new


def kernel(x_prompt, x_sample, mem_prompt, cache_mem_k, cache_mem_v, state_gdn, state_gdn_conv, cache_swa_k, cache_swa_v, norm_mix, norm_ffn, mem_norm, w_mem_kv, mem_q_norm, mem_k_norm, w_in_a, conv_w_a, a_log, dt_bias, o_norm_a, w_out_a, w_in_b, q_norm_b, k_norm_b, sinks_b, w_out_b, w_gate_up, w_down):
    bsz = x_prompt.shape[0]
    dec = x_sample.shape[0]
    kvw = SWA_KV_HEADS * SWA_HD
    wa = w_in_a[0]
    ba_cols = jnp.pad(wa[:, GDN_QKV + GDN_VW:GDN_QKV + GDN_VW + 2 * GDN_HEADS],
                      ((0, 0), (0, LANES - 2 * GDN_HEADS)))
    wb = w_in_b[0]
    w = {
        "norm_mix": norm_mix, "norm_ffn": norm_ffn, "mem_q_norm": mem_q_norm,
        "in_a": [wa[:, :GDN_QKV].astype(BF16), wa[:, GDN_QKV:GDN_QKV + GDN_VW].astype(BF16),
                 ba_cols.astype(BF16), wa[:, GDN_QKV + GDN_VW + 2 * GDN_HEADS:].astype(BF16)],
        "in_b": [wb[:, :SWA_HEADS * SWA_HD].astype(BF16),
                 wb[:, SWA_HEADS * SWA_HD:SWA_HEADS * SWA_HD + 2 * kvw].astype(BF16),
                 wb[:, SWA_HEADS * SWA_HD + 2 * kvw:].astype(BF16)],
        "conv_w_a": conv_w_a[0], "o_norm_a": o_norm_a[0],
        "alog_row": _lane_row(a_log[0], GDN_HEADS), "dtb_row": _lane_row(dt_bias[0], GDN_HEADS),
        "out_a": w_out_a[0].astype(BF16), "out_b": w_out_b[0].astype(BF16),
        "q_norm_b": q_norm_b[0], "k_norm_b": k_norm_b[0], "sinks_b": sinks_b[0],
        "gate_up": w_gate_up.astype(BF16), "down": w_down.astype(BF16),
    }

    mk, mv = _memkv_call(mem_prompt[0], mem_norm, w_mem_kv, mem_k_norm)
    depth = mk.shape[0]
    new_mem_k = mk.reshape(depth, bsz, N_MEM, MEM_HEADS, MEM_HD)
    new_mem_v = mv.reshape(depth, bsz, N_MEM, MEM_HEADS, MEM_HD)

    zero_conv = jnp.zeros((bsz, GDN_CONV - 1, GDN_QKV), F32)
    zero_state = jnp.zeros((bsz, GDN_HEADS, GDN_DK, GDN_DV), F32)
    zero_hist = jnp.zeros((bsz, WINDOW, kvw), F32)
    y_p, conv_p, state_p, k_p, v_p = _trunk(
        x_prompt, mk.reshape(depth, bsz, N_MEM, MEM_W), mv.reshape(depth, bsz, N_MEM, MEM_W),
        zero_conv, zero_state, zero_hist, zero_hist, False, w, 256, 1)

    y_s, conv_s, state_s, k_s, v_s = _trunk(
        x_sample, cache_mem_k.reshape(depth, dec, N_MEM, MEM_W),
        cache_mem_v.reshape(depth, dec, N_MEM, MEM_W), state_gdn_conv[0], state_gdn[0],
        cache_swa_k[0].reshape(dec, WINDOW, kvw), cache_swa_v[0].reshape(dec, WINDOW, kvw),
        True, w, 256, 256 // x_sample.shape[1])

    keep = min(WINDOW, x_prompt.shape[1])
    kv_shape = lambda a: a.reshape(a.shape[0], a.shape[1], SWA_KV_HEADS, SWA_HD)[None]
    return (y_p, y_s, state_p[None], conv_p[None], state_s[None], conv_s[None],
            kv_shape(k_p[:, -keep:]), kv_shape(v_p[:, -keep:]), kv_shape(k_s), kv_shape(v_s),
            new_mem_k, new_mem_v)
```

```python
import functools

import jax
import jax.numpy as jnp
from jax import lax
from jax.experimental import pallas as pl
from jax.experimental.pallas import tpu as pltpu

F32 = jnp.float32
BF16 = jnp.bfloat16

D_MODEL = 1024
CHUNK = 64
EPS = 1e-6
GDN_HEADS = 8
GDN_DK = 128
GDN_DV = 128
GDN_CONV = 4
GDN_QKV = GDN_HEADS * (2 * GDN_DK + GDN_DV)
GDN_VW = GDN_HEADS * GDN_DV
SWA_HEADS = 16
SWA_KV_HEADS = 4
SWA_HD = 64
SWA_GROUP = SWA_HEADS // SWA_KV_HEADS
SWA_QW = SWA_HEADS * SWA_HD
SWA_KVW = SWA_KV_HEADS * SWA_HD
WINDOW = 128
N_MEM = 256
MEM_HEADS = 4
MEM_HD = 128
MEM_W = MEM_HEADS * MEM_HD
LANES = 128
SUBLANES = 8
VMEM_LIMIT = 56 * 1024 * 1024
ROW_TILE = 256
SWA_TILE_ROWS = 4 * CHUNK
GDN_TILE_ROWS = 4 * CHUNK
CONV_HIST = SUBLANES - (GDN_CONV - 1)

_HI = lax.Precision.HIGHEST


def _mm(a, b):
    return jnp.dot(a.astype(BF16), b.astype(BF16), preferred_element_type=F32)


def _mm_nt(a, b):
    return lax.dot_general(a.astype(BF16), b.astype(BF16), (((1,), (1,)), ((), ())),
                           preferred_element_type=F32)


def _mm_tn(a, b):
    return lax.dot_general(a.astype(BF16), b.astype(BF16), (((0,), (0,)), ((), ())),
                           preferred_element_type=F32)


def _rms(x, g):
    return x * lax.rsqrt(jnp.mean(x * x, axis=-1, keepdims=True) + EPS) * g


def _sigmoid(x):
    return 1.0 / (1.0 + jnp.exp(-x))


def _silu(x):
    return x * _sigmoid(x)


def _softplus(x):
    return jnp.maximum(x, 0.0) + jnp.log1p(jnp.exp(-jnp.abs(x)))


def _const_spec(shape):
    nd = len(shape)
    return pl.BlockSpec(shape, lambda *_: (0,) * nd, pipeline_mode=pl.Buffered(1))


def _params(sem):
    return pltpu.CompilerParams(dimension_semantics=sem, vmem_limit_bytes=VMEM_LIMIT)


def _memkv_body(mem_ref, g_ref, w_ref, kg_ref, mk_ref, mv_ref):
    h = _rms(mem_ref[...], g_ref[...])
    kv = _mm(h, w_ref[...])
    kg = kg_ref[...]
    for hd in range(MEM_HEADS):
        sl = slice(hd * MEM_HD, (hd + 1) * MEM_HD)
        mk_ref[:, sl] = _rms(kv[:, sl], kg)
    mv_ref[...] = kv[:, MEM_W:]


def _memkv_call(mem, mem_norm, w_mem_kv, mem_k_norm):
    depth = w_mem_kv.shape[0]
    out = jax.ShapeDtypeStruct((depth, N_MEM, MEM_W), F32)
    return pl.pallas_call(
        _memkv_body,
        grid=(depth,),
        in_specs=[
            pl.BlockSpec((N_MEM, D_MODEL), lambda i: (0, 0)),
            pl.BlockSpec((None, 1, D_MODEL), lambda i: (i, 0, 0)),
            pl.BlockSpec((None, D_MODEL, 2 * MEM_W), lambda i: (i, 0, 0)),
            pl.BlockSpec((None, 1, MEM_HD), lambda i: (i, 0, 0)),
        ],
        out_specs=[pl.BlockSpec((None, N_MEM, MEM_W), lambda i: (i, 0, 0))] * 2,
        out_shape=[out, out],
        compiler_params=_params(("arbitrary",)),
        name="memkv",
    )(mem, mem_norm.reshape(depth, 1, D_MODEL), w_mem_kv, mem_k_norm.reshape(depth, 1, MEM_HD))


def _proj_a_body(seqs, tiles_per_seq, x_ref, g_ref, wqkv_ref, wz_ref, wba_ref, wqm_ref, conv0_ref,
                 cw_ref, alog_ref, dtb_ref,
                 q_ref, k_ref, kb_ref, qd_ref, kd_ref, vb_ref, kbe_ref, cum_ref, cumt_ref, z_ref,
                 qm_ref, convn_ref, xp_ref):
    c = CHUNK
    tm = x_ref.shape[0]
    rows = tm // seqs
    h = _rms(x_ref[...], g_ref[...]).astype(BF16)
    z_ref[...] = jnp.dot(h, wz_ref[...], preferred_element_type=F32)
    qm_ref[...] = jnp.dot(h, wqm_ref[...], preferred_element_type=F32)
    ba = jnp.dot(h, wba_ref[...], preferred_element_type=F32)
    qkv = jnp.dot(h, wqkv_ref[...], preferred_element_type=F32)

    if seqs == 1:
        @pl.when(pl.program_id(0) % tiles_per_seq == 0)
        def _():
            xp_ref[0, 0:SUBLANES, :] = conv0_ref[0]
    for s in range(seqs):
        if seqs > 1:
            xp_ref[s, 0:SUBLANES, :] = conv0_ref[s]
        xp_ref[s, SUBLANES:SUBLANES + rows, :] = qkv[s * rows:(s + 1) * rows]

    tril = (lax.broadcasted_iota(jnp.int32, (c, c), 0)
            >= lax.broadcasted_iota(jnp.int32, (c, c), 1)).astype(F32)
    eye_l = (lax.broadcasted_iota(jnp.int32, (LANES, LANES), 0)
             == lax.broadcasted_iota(jnp.int32, (LANES, LANES), 1)).astype(F32)

    beta_all = _sigmoid(ba)
    g_all = -jnp.exp(alog_ref[...]) * _softplus(ba + dtb_ref[...])

    for j in range(tm // c):
        rs = slice(j * c, (j + 1) * c)
        s, lr = (j * c) // rows, (j * c) % rows
        cum = jnp.dot(tril, g_all[rs], precision=_HI, preferred_element_type=F32)
        cum_ref[rs, :] = cum
        cumt_ref[j] = lax.dot_general(eye_l, cum, (((1,), (1,)), ((), ())), precision=_HI,
                                      preferred_element_type=F32)
        e_cum = jnp.exp(cum)
        e_rest = jnp.exp(cum[c - 1:c, :] - cum)
        beta = beta_all[rs]

        def conv_act(lo):
            base = CONV_HIST + lr
            acc = xp_ref[s, base:base + c, lo:lo + LANES] * cw_ref[0:1, lo:lo + LANES]
            for i in range(1, GDN_CONV):
                acc = acc + (xp_ref[s, base + i:base + i + c, lo:lo + LANES]
                             * cw_ref[i:i + 1, lo:lo + LANES])
            return _silu(acc)

        for hd in range(GDN_HEADS):
            hs = slice(hd * GDN_DK, (hd + 1) * GDN_DK)
            b_col = beta[:, hd:hd + 1]
            ec_col = e_cum[:, GDN_HEADS + hd:GDN_HEADS + hd + 1]
            er_col = e_rest[:, GDN_HEADS + hd:GDN_HEADS + hd + 1]
            q = conv_act(hd * GDN_DK)
            q = q * lax.rsqrt(jnp.sum(q * q, axis=-1, keepdims=True) + EPS) * (GDN_DK ** -0.5)
            k = conv_act(GDN_HEADS * GDN_DK + hd * GDN_DK)
            k = k * lax.rsqrt(jnp.sum(k * k, axis=-1, keepdims=True) + EPS)
            v = conv_act(2 * GDN_HEADS * GDN_DK + hd * GDN_DV)
            kb = k * b_col
            q_ref[rs, hs] = q.astype(BF16)
            qd_ref[rs, hs] = (q * ec_col).astype(BF16)
            k_ref[rs, hs] = k.astype(BF16)
            kb_ref[rs, hs] = kb.astype(BF16)
            kd_ref[rs, hs] = (k * er_col).astype(BF16)
            kbe_ref[rs, hs] = kb * ec_col
            vb_ref[rs, hs] = v * b_col

    for s in range(seqs):
        tail = xp_ref[s, rows:rows + SUBLANES, :]
        xp_ref[s, 0:SUBLANES, :] = tail
        convn_ref[s] = tail


def _proj_a_call(x, g, w, conv0, tm, seqs):
    rows = x.shape[0]
    ns = conv0.shape[0]
    tiles_per_seq = rows // (ns * tm) if seqs == 1 else 1
    seq_of = (lambda i: i // tiles_per_seq) if seqs == 1 else (lambda i: i)
    tile = lambda width: pl.BlockSpec((tm, width), lambda i: (i, 0))
    hist = pl.BlockSpec((seqs, SUBLANES, GDN_QKV), lambda i: (seq_of(i), 0, 0))
    wide_bf = jax.ShapeDtypeStruct((rows, GDN_VW), BF16)
    wide_f32 = jax.ShapeDtypeStruct((rows, GDN_VW), F32)
    return pl.pallas_call(
        functools.partial(_proj_a_body, seqs, tiles_per_seq),
        grid=(rows // tm,),
        in_specs=[tile(D_MODEL), _const_spec((1, D_MODEL))]
        + [_const_spec(w[name].shape) for name in ("a_qkv", "a_z", "a_ba", "a_qm")]
        + [hist, _const_spec((GDN_CONV, GDN_QKV)), _const_spec((1, LANES)), _const_spec((1, LANES))],
        out_specs=[tile(GDN_VW)] * 7
        + [tile(LANES), pl.BlockSpec((tm // CHUNK, LANES, CHUNK), lambda i: (i, 0, 0)),
           tile(GDN_VW), tile(MEM_W), hist],
        out_shape=[wide_bf] * 5 + [wide_f32] * 2
        + [jax.ShapeDtypeStruct((rows, LANES), F32),
           jax.ShapeDtypeStruct((rows // CHUNK, LANES, CHUNK), F32),
           wide_f32, jax.ShapeDtypeStruct((rows, MEM_W), F32),
           jax.ShapeDtypeStruct((ns, SUBLANES, GDN_QKV), F32)],
        scratch_shapes=[pltpu.VMEM((seqs, SUBLANES + tm // seqs, GDN_QKV), F32)],
        compiler_params=_params(("arbitrary",)),
        name="proj_a",
    )(x, g.reshape(1, D_MODEL), w["a_qkv"], w["a_z"], w["a_ba"], w["a_qm"], conv0, w["conv_w_a"],
      w["alog_row"], w["dtb_row"])


def _gdn_body(q_ref, k_ref, kb_ref, qd_ref, kd_ref, vb_ref, kbe_ref, cum_ref, cumt_ref, s0_ref,
              o_ref, sn_ref, s_ref):
    c = CHUNK
    nc = q_ref.shape[0] // c

    @pl.when(pl.program_id(1) == 0)
    def _():
        s_ref[...] = s0_ref[...]

    row = lax.broadcasted_iota(jnp.int32, (c, c), 0)
    col = lax.broadcasted_iota(jnp.int32, (c, c), 1)
    causal = row >= col
    strict = row > col

    pairs = [(j, hd) for j in range(nc) for hd in range(GDN_HEADS)]
    rs = lambda j: slice(j * c, (j + 1) * c)
    hs = lambda hd: slice(hd * GDN_DK, (hd + 1) * GDN_DK)
    gl = lambda hd: slice(GDN_HEADS + hd, GDN_HEADS + hd + 1)

    decays = [jnp.exp(jnp.where(causal, cum_ref[rs(j), gl(hd)] - cumt_ref[j, gl(hd), :], -jnp.inf))
              for j, hd in pairs]
    grams = [_mm_nt(jnp.concatenate([kb_ref[rs(j), hs(hd)], q_ref[rs(j), hs(hd)]], axis=0),
                    k_ref[rs(j), hs(hd)]) for j, hd in pairs]
    ps = [jnp.where(strict, -(g[:c] * d), 0.0) for g, d in zip(grams, decays)]
    qks = [g[c:] * d for g, d in zip(grams, decays)]

    ns = ps
    for _ in range(5):
        ps = [_mm(p, p) for p in ps]
        ns = [n + p + _mm(n, p) for n, p in zip(ns, ps)]

    rhss = [jnp.concatenate([vb_ref[rs(j), hs(hd)], kbe_ref[rs(j), hs(hd)]], axis=1)
            for j, hd in pairs]
    sols = [rhs + _mm(n, rhs) for n, rhs in zip(ns, rhss)]

    ss = [s_ref[hd] for hd in range(GDN_HEADS)]
    for j in range(nc):
        sol_j = sols[j * GDN_HEADS:(j + 1) * GDN_HEADS]
        qk_j = qks[j * GDN_HEADS:(j + 1) * GDN_HEADS]
        r1s = [_mm(jnp.concatenate([sol_j[hd][:, GDN_DV:].astype(BF16), qd_ref[rs(j), hs(hd)]],
                                   axis=0), ss[hd]) for hd in range(GDN_HEADS)]
        us = [sol_j[hd][:, :GDN_DV] - r1s[hd][:c] for hd in range(GDN_HEADS)]
        for hd in range(GDN_HEADS):
            o_ref[rs(j), hs(hd)] = r1s[hd][c:] + _mm(qk_j[hd], us[hd])
        ss = [ss[hd] * jnp.exp(cum_ref[j * c + c - 1:j * c + c, gl(hd)])
              + _mm_tn(kd_ref[rs(j), hs(hd)], us[hd]) for hd in range(GDN_HEADS)]
    for hd in range(GDN_HEADS):
        s_ref[hd] = ss[hd]
        sn_ref[hd] = ss[hd]


def _gdn_call(ops, s0, ns, tr):
    t = ops[0].shape[0] // ns
    seq = lambda width: pl.BlockSpec((None, tr, width), lambda s, i: (s, i, 0))
    state = pl.BlockSpec((None, GDN_HEADS, GDN_DK, GDN_DV), lambda s, i: (s, 0, 0, 0))
    args = [a.reshape(ns, t, a.shape[-1]) for a in ops[:8]]
    args.append(ops[8].reshape(ns, t // CHUNK, LANES, CHUNK))
    return pl.pallas_call(
        _gdn_body,
        grid=(ns, t // tr),
        in_specs=[seq(GDN_VW)] * 7
        + [seq(LANES), pl.BlockSpec((None, tr // CHUNK, LANES, CHUNK), lambda s, i: (s, i, 0, 0)),
           state],
        out_specs=[seq(GDN_VW), state],
        out_shape=[jax.ShapeDtypeStruct((ns, t, GDN_VW), F32),
                   jax.ShapeDtypeStruct((ns, GDN_HEADS, GDN_DK, GDN_DV), F32)],
        scratch_shapes=[pltpu.VMEM((GDN_HEADS, GDN_DK, GDN_DV), F32)],
        compiler_params=_params(("arbitrary", "arbitrary")),
        name="gdn",
    )(*args, s0)


def _proj_b_body(x_ref, g_ref, wq_ref, wkv_ref, wqm_ref, qg_ref, kg_ref,
                 qlo_ref, qhi_ref, kd_ref, vd_ref, kc_ref, vc_ref, qm_ref):
    h = _rms(x_ref[...], g_ref[...]).astype(BF16)
    qm_ref[...] = jnp.dot(h, wqm_ref[...], preferred_element_type=F32)
    q = jnp.dot(h, wq_ref[...], preferred_element_type=F32)
    kv = jnp.dot(h, wkv_ref[...], preferred_element_type=F32)
    lo = lax.broadcasted_iota(jnp.int32, (1, LANES), 1) < SWA_HD
    qg = qg_ref[...] * (SWA_HD ** -0.5)
    for p in range(SWA_QW // LANES):
        sl = slice(p * LANES, (p + 1) * LANES)
        x = q[:, sl]
        x2 = x * x
        m_lo = jnp.sum(jnp.where(lo, x2, 0.0), axis=-1, keepdims=True) * (1.0 / SWA_HD)
        m_hi = jnp.sum(jnp.where(lo, 0.0, x2), axis=-1, keepdims=True) * (1.0 / SWA_HD)
        qn = x * jnp.where(lo, lax.rsqrt(m_lo + EPS), lax.rsqrt(m_hi + EPS)) * qg
        qlo_ref[:, sl] = jnp.where(lo, qn, 0.0).astype(BF16)
        qhi_ref[:, sl] = jnp.where(lo, 0.0, qn).astype(BF16)
    kg = kg_ref[...]
    for p in range(SWA_KV_HEADS):
        sl = slice(p * LANES, (p + 1) * LANES)
        kn = _rms(kv[:, sl], kg)
        kc_ref[:, sl] = kn
        kd_ref[:, sl] = kn.astype(BF16)
    v = kv[:, SWA_KV_HEADS * LANES:]
    vc_ref[...] = v
    vd_ref[...] = v.astype(BF16)


def _proj_b_call(x, g, w, tm, keep_all):
    rows = x.shape[0]
    dup_w = SWA_KV_HEADS * LANES
    tile = lambda width: pl.BlockSpec((tm, width), lambda i: (i, 0))
    cache = tile(dup_w) if keep_all else pl.BlockSpec((tm, dup_w), lambda i: (0, 0))
    cache_shape = jax.ShapeDtypeStruct((rows if keep_all else tm, dup_w), F32)
    return pl.pallas_call(
        _proj_b_body,
        grid=(rows // tm,),
        in_specs=[tile(D_MODEL), _const_spec((1, D_MODEL))]
        + [_const_spec(w[name].shape) for name in ("b_q", "b_kv", "b_qm")]
        + [_const_spec((1, LANES))] * 2,
        out_specs=[tile(SWA_QW)] * 2 + [tile(dup_w)] * 2 + [cache] * 2 + [tile(MEM_W)],
        out_shape=[jax.ShapeDtypeStruct((rows, SWA_QW), BF16)] * 2
        + [jax.ShapeDtypeStruct((rows, dup_w), BF16)] * 2 + [cache_shape] * 2
        + [jax.ShapeDtypeStruct((rows, MEM_W), F32)],
        compiler_params=_params(("arbitrary",)),
        name="proj_b",
    )(x, g.reshape(1, D_MODEL), w["b_q"], w["b_kv"], w["b_qm"], w["q_gain2"], w["k_gain2"])


def _swa_body(hist_valid, qlo_ref, qhi_ref, k_ref, v_ref, kh_ref, vh_ref, sink_ref,
              o_ref, kwin_ref, vwin_ref):
    t = pl.program_id(1)
    c = CHUNK
    tr = qlo_ref.shape[0]

    @pl.when(t == 0)
    def _():
        kwin_ref[0:WINDOW, :] = kh_ref[...]
        vwin_ref[0:WINDOW, :] = vh_ref[...]

    kwin_ref[WINDOW:WINDOW + tr, :] = k_ref[...]
    vwin_ref[WINDOW:WINDOW + tr, :] = v_ref[...]

    kv_heads = range(SWA_KV_HEADS)
    sinks = [jnp.concatenate([jnp.full((c, 1), sink_ref[0, h * SWA_GROUP + g], F32)
                              for g in range(SWA_GROUP)], axis=0) for h in kv_heads]
    lo = lax.broadcasted_iota(jnp.int32, (1, LANES), 1) < SWA_HD
    zero = jnp.zeros((), BF16)

    pairs = [(j, h) for j in range(tr // c) for h in kv_heads]
    slab = lambda p: slice(p * LANES, (p + 1) * LANES)
    rows = lambda j: slice(j * c, (j + 1) * c)
    win = lambda j: slice(j * c, j * c + WINDOW + c)
    qss = [jnp.concatenate([qlo_ref[rows(j), slab(2 * h)], qhi_ref[rows(j), slab(2 * h)],
                            qlo_ref[rows(j), slab(2 * h + 1)], qhi_ref[rows(j), slab(2 * h + 1)]],
                           axis=0) for j, h in pairs]
    ss = [_mm_nt(qs, kwin_ref[win(j), slab(h)]) for qs, (j, h) in zip(qss, pairs)]
    if not hist_valid:
        key_col = lax.broadcasted_iota(jnp.int32, (SWA_GROUP * c, WINDOW + c), 1)
        ss = [jnp.where(key_col + (t * tr + j * c - WINDOW) >= 0, s, -jnp.inf)
              if j * c < WINDOW else s for s, (j, h) in zip(ss, pairs)]
    ps = []
    for s, (j, h) in zip(ss, pairs):
        m = jnp.maximum(jnp.max(s, axis=-1, keepdims=True), sinks[h])
        p = jnp.exp(s - m)
        ps.append((p / (jnp.sum(p, axis=-1, keepdims=True) + jnp.exp(sinks[h] - m))).astype(BF16))
    for p, (j, h) in zip(ps, pairs):
        v = vwin_ref[win(j), slab(h)]
        v_lo = jnp.where(lo, v, zero)
        v_hi = jnp.where(lo, zero, v)
        for half in range(2):
            o_ref[rows(j), slab(2 * h + half)] = (
                _mm(p[(2 * half) * c:(2 * half + 1) * c], v_lo)
                + _mm(p[(2 * half + 1) * c:(2 * half + 2) * c], v_hi))

    for ref in (kwin_ref, vwin_ref):
        ref[0:WINDOW, :] = ref[tr:tr + WINDOW, :]


def _swa_call(qlo, qhi, kd, vd, k_hist, v_hist, sinks, hist_valid, ns, tr):
    t = qlo.shape[0] // ns
    dup_w = SWA_KV_HEADS * LANES
    seq = lambda width: pl.BlockSpec((None, tr, width), lambda s, i: (s, i, 0))
    hist = pl.BlockSpec((None, WINDOW, dup_w), lambda s, i: (s, 0, 0))
    return pl.pallas_call(
        functools.partial(_swa_body, hist_valid),
        grid=(ns, t // tr),
        in_specs=[seq(SWA_QW), seq(SWA_QW), seq(dup_w), seq(dup_w), hist, hist,
                  pl.BlockSpec(memory_space=pltpu.SMEM)],
        out_specs=seq(SWA_QW),
        out_shape=jax.ShapeDtypeStruct((ns, t, SWA_QW), F32),
        scratch_shapes=[pltpu.VMEM((WINDOW + tr, dup_w), BF16)] * 2,
        compiler_params=_params(("arbitrary", "arbitrary")),
        name="swa",
    )(qlo.reshape(ns, t, SWA_QW), qhi.reshape(ns, t, SWA_QW), kd.reshape(ns, t, dup_w),
      vd.reshape(ns, t, dup_w), k_hist, v_hist, sinks.reshape(1, SWA_HEADS))


def _tail_body(seqs, gated, x_ref, om_ref, *refs):
    if gated:
        z_ref, og_ref = refs[:2]
        refs = refs[2:]
    qm_ref, mk_ref, mv_ref, mqg_ref, wo_ref, gf_ref, wgu_ref, wd_ref, y_ref = refs
    tm = x_ref.shape[0]
    rows = tm // seqs
    d_ff = wd_ref.shape[0]
    mqg = mqg_ref[...] * (MEM_HD ** -0.5)

    mem_parts = []
    for sq in range(seqs):
        heads = []
        for hd in range(MEM_HEADS):
            sl = slice(hd * MEM_HD, (hd + 1) * MEM_HD)
            qh = _rms(qm_ref[sq * rows:(sq + 1) * rows, sl], mqg)
            s = _mm_nt(qh, mk_ref[sq, :, sl])
            p = jnp.exp(s - jnp.max(s, axis=-1, keepdims=True))
            p = p / jnp.sum(p, axis=-1, keepdims=True)
            heads.append(_mm(p, mv_ref[sq, :, sl]))
        mem_parts.append(jnp.concatenate(heads, axis=1))
    o_mem = mem_parts[0] if seqs == 1 else jnp.concatenate(mem_parts, axis=0)

    mix_w = om_ref.shape[1]
    if gated:
        og = og_ref[...]
        o_mix = jnp.concatenate(
            [(_rms(om_ref[:, hd * GDN_DV:(hd + 1) * GDN_DV], og)
              * _silu(z_ref[:, hd * GDN_DV:(hd + 1) * GDN_DV])).astype(BF16)
             for hd in range(GDN_HEADS)], axis=1)
    else:
        o_mix = om_ref[...]
    x = x_ref[...] + _mm(o_mix, wo_ref[0:mix_w, :]) + _mm(o_mem, wo_ref[mix_w:, :])
    h = _rms(x, gf_ref[...])
    gu = _mm(h, wgu_ref[...])
    act = _silu(gu[:, :d_ff]) * gu[:, d_ff:]
    y_ref[...] = x + _mm(act, wd_ref[...])


def _tail_call(x, o_mix, gate, qm, mk, mv, mem_q_gain, w_out, g_ffn, w_gate_up, w_down, tm, seqs):
    rows = x.shape[0]
    mix_w = o_mix.shape[1]
    tile = lambda width: pl.BlockSpec((tm, width), lambda i: (i, 0))
    tiles_per_seq = rows // (tm * mk.shape[0]) if seqs == 1 else 1
    mem = pl.BlockSpec((seqs, N_MEM, MEM_W), lambda i: (i // tiles_per_seq, 0, 0))
    gate_specs = [tile(mix_w), _const_spec((1, GDN_DV))] if gate else []
    gate_args = [gate[0], gate[1].reshape(1, GDN_DV)] if gate else []
    return pl.pallas_call(
        functools.partial(_tail_body, seqs, bool(gate)),
        grid=(rows // tm,),
        in_specs=[tile(D_MODEL), tile(mix_w)] + gate_specs + [
            tile(MEM_W), mem, mem,
            _const_spec((1, MEM_HD)), _const_spec(w_out.shape), _const_spec((1, D_MODEL)),
            _const_spec(w_gate_up.shape), _const_spec(w_down.shape),
        ],
        out_specs=tile(D_MODEL),
        out_shape=jax.ShapeDtypeStruct((rows, D_MODEL), F32),
        compiler_params=_params(("arbitrary",)),
        name="tail",
    )(x, o_mix, *gate_args, qm, mk, mv, mem_q_gain.reshape(1, MEM_HD), w_out,
      g_ffn.reshape(1, D_MODEL), w_gate_up, w_down)


def _lane_row(vals, offset):
    return jnp.zeros((1, LANES), F32).at[0, offset:offset + vals.shape[0]].set(vals)


def _dup_heads(a):
    lead = a.shape[:-1]
    a = a.reshape(lead + (SWA_KV_HEADS, 1, SWA_HD))
    return jnp.broadcast_to(a, lead + (SWA_KV_HEADS, 2, SWA_HD)).reshape(lead + (SWA_KV_HEADS * LANES,))


def _undup_heads(a):
    lead = a.shape[:-1]
    return a.reshape(lead + (SWA_KV_HEADS, 2, SWA_HD))[..., 0, :]


def _trunk(x, mem_k, mem_v, gdn_conv, gdn_state, swa_k_hist, swa_v_hist, hist_valid, w):
    ns, t, _ = x.shape
    rows = ns * t
    tm = ROW_TILE
    seqs = max(1, tm // t)
    x2 = x.reshape(rows, D_MODEL)

    conv0 = jnp.pad(gdn_conv, ((0, 0), (CONV_HIST, 0), (0, 0)))
    *gdn_ops, z, qm, conv_new = _proj_a_call(x2, w["norm_mix"][0], w, conv0, tm, seqs)
    o_raw, s_new = _gdn_call(gdn_ops, gdn_state, ns, min(t, GDN_TILE_ROWS))
    x2 = _tail_call(x2, o_raw.reshape(rows, GDN_VW), (z, w["o_norm_a"]), qm, mem_k[0], mem_v[0],
                    w["mem_q_norm"][0], w["out_a"], w["norm_ffn"][0], w["gate_up"][0], w["down"][0],
                    tm, seqs)

    qlo, qhi, kd, vd, k_cache, v_cache, qm = _proj_b_call(x2, w["norm_mix"][1], w, tm, ns > 1)
    o_mix = _swa_call(qlo, qhi, kd, vd, _dup_heads(swa_k_hist).astype(BF16),
                      _dup_heads(swa_v_hist).astype(BF16), w["sinks_b"], hist_valid, ns,
                      min(t, SWA_TILE_ROWS))
    x2 = _tail_call(x2, o_mix.reshape(rows, SWA_QW), None, qm, mem_k[1], mem_v[1],
                    w["mem_q_norm"][1], w["out_b"], w["norm_ffn"][1], w["gate_up"][1], w["down"][1],
                    tm, seqs)

    keep = min(WINDOW, t) if ns == 1 else t
    k_new = _undup_heads(k_cache).reshape(ns, -1, SWA_KVW)[:, -keep:]
    v_new = _undup_heads(v_cache).reshape(ns, -1, SWA_KVW)[:, -keep:]
    return x2.reshape(ns, t, D_MODEL), conv_new[:, CONV_HIST:, :], s_new, k_new, v_new


def kernel(x_prompt, x_sample, mem_prompt, cache_mem_k, cache_mem_v, state_gdn, state_gdn_conv, cache_swa_k, cache_swa_v, norm_mix, norm_ffn, mem_norm, w_mem_kv, mem_q_norm, mem_k_norm, w_in_a, conv_w_a, a_log, dt_bias, o_norm_a, w_out_a, w_in_b, q_norm_b, k_norm_b, sinks_b, w_out_b, w_gate_up, w_down):
    bsz = x_prompt.shape[0]
    dec = x_sample.shape[0]
    wa = w_in_a[0]
    z_lo = GDN_QKV
    ba_lo = GDN_QKV + GDN_VW
    qm_lo = ba_lo + 2 * GDN_HEADS
    wb = w_in_b[0]
    w = {
        "norm_mix": norm_mix, "norm_ffn": norm_ffn, "mem_q_norm": mem_q_norm,
        "a_qkv": wa[:, :z_lo].astype(BF16), "a_z": wa[:, z_lo:ba_lo].astype(BF16),
        "a_ba": jnp.pad(wa[:, ba_lo:qm_lo], ((0, 0), (0, LANES - 2 * GDN_HEADS))).astype(BF16),
        "a_qm": wa[:, qm_lo:].astype(BF16),
        "b_q": wb[:, :SWA_QW].astype(BF16),
        "b_kv": jnp.concatenate([_dup_heads(wb[:, SWA_QW:SWA_QW + SWA_KVW]),
                                 _dup_heads(wb[:, SWA_QW + SWA_KVW:SWA_QW + 2 * SWA_KVW])],
                                axis=1).astype(BF16),
        "b_qm": wb[:, SWA_QW + 2 * SWA_KVW:].astype(BF16),
        "q_gain2": jnp.tile(q_norm_b[0], 2).reshape(1, LANES),
        "k_gain2": jnp.tile(k_norm_b[0], 2).reshape(1, LANES),
        "conv_w_a": conv_w_a[0], "o_norm_a": o_norm_a[0],
        "alog_row": _lane_row(a_log[0], GDN_HEADS), "dtb_row": _lane_row(dt_bias[0], GDN_HEADS),
        "out_a": w_out_a[0].astype(BF16), "out_b": w_out_b[0].astype(BF16),
        "sinks_b": sinks_b[0],
        "gate_up": w_gate_up.astype(BF16), "down": w_down.astype(BF16),
    }

    mk, mv = _memkv_call(mem_prompt[0], mem_norm, w_mem_kv, mem_k_norm)
    depth = mk.shape[0]
    new_mem_k = mk.reshape(depth, bsz, N_MEM, MEM_HEADS, MEM_HD)
    new_mem_v = mv.reshape(depth, bsz, N_MEM, MEM_HEADS, MEM_HD)

    zero_conv = jnp.zeros((bsz, GDN_CONV - 1, GDN_QKV), F32)
    zero_state = jnp.zeros((bsz, GDN_HEADS, GDN_DK, GDN_DV), F32)
    zero_hist = jnp.zeros((bsz, WINDOW, SWA_KVW), F32)
    y_p, conv_p, state_p, k_p, v_p = _trunk(
        x_prompt, mk.reshape(depth, bsz, N_MEM, MEM_W), mv.reshape(depth, bsz, N_MEM, MEM_W),
        zero_conv, zero_state, zero_hist, zero_hist, False, w)

    y_s, conv_s, state_s, k_s, v_s = _trunk(
        x_sample, cache_mem_k.reshape(depth, dec, N_MEM, MEM_W),
        cache_mem_v.reshape(depth, dec, N_MEM, MEM_W), state_gdn_conv[0], state_gdn[0],
        cache_swa_k[0].reshape(dec, WINDOW, SWA_KVW), cache_swa_v[0].reshape(dec, WINDOW, SWA_KVW),
        True, w)

    kv_shape = lambda a: a.reshape(a.shape[0], a.shape[1], SWA_KV_HEADS, SWA_HD)[None]
    return (y_p, y_s, state_p[None], conv_p[None], state_s[None], conv_s[None],
            kv_shape(k_p), kv_shape(v_p), kv_shape(k_s), kv_shape(v_s), new_mem_k, new_mem_v)
```

```python
import functools

import jax
import jax.numpy as jnp
from jax import lax
from jax.experimental import pallas as pl
from jax.experimental.pallas import tpu as pltpu

F32 = jnp.float32
BF16 = jnp.bfloat16

D_MODEL = 1024
CHUNK = 64
EPS = 1e-6
GDN_HEADS = 8
GDN_DK = 128
GDN_DV = 128
GDN_CONV = 4
GDN_QKV = GDN_HEADS * (2 * GDN_DK + GDN_DV)
GDN_VW = GDN_HEADS * GDN_DV
SWA_HEADS = 16
SWA_KV_HEADS = 4
SWA_HD = 64
SWA_GROUP = SWA_HEADS // SWA_KV_HEADS
SWA_QW = SWA_HEADS * SWA_HD
SWA_KVW = SWA_KV_HEADS * SWA_HD
WINDOW = 128
N_MEM = 256
MEM_HEADS = 4
MEM_HD = 128
MEM_W = MEM_HEADS * MEM_HD
LANES = 128
SUBLANES = 8
VMEM_LIMIT = 56 * 1024 * 1024
ROW_TILE = 256
SWA_TILE_ROWS = 4 * CHUNK
GDN_TILE_ROWS = 4 * CHUNK

_HI = lax.Precision.HIGHEST


def _mm(a, b):
    return jnp.dot(a.astype(BF16), b.astype(BF16), preferred_element_type=F32)


def _mm_nt(a, b):
    return lax.dot_general(a.astype(BF16), b.astype(BF16), (((1,), (1,)), ((), ())),
                           preferred_element_type=F32)


def _mm_tn(a, b):
    return lax.dot_general(a.astype(BF16), b.astype(BF16), (((0,), (0,)), ((), ())),
                           preferred_element_type=F32)


def _rms(x, g):
    return x * lax.rsqrt(jnp.mean(x * x, axis=-1, keepdims=True) + EPS) * g


def _sigmoid(x):
    return 1.0 / (1.0 + jnp.exp(-x))


def _silu(x):
    return x * _sigmoid(x)


def _softplus(x):
    return jnp.maximum(x, 0.0) + jnp.log1p(jnp.exp(-jnp.abs(x)))


def _const_spec(shape):
    nd = len(shape)
    return pl.BlockSpec(shape, lambda *_: (0,) * nd, pipeline_mode=pl.Buffered(1))


def _params(sem):
    return pltpu.CompilerParams(dimension_semantics=sem, vmem_limit_bytes=VMEM_LIMIT)


def _memkv_body(mem_ref, g_ref, w_ref, kg_ref, mk_ref, mv_ref):
    h = _rms(mem_ref[...], g_ref[...])
    kv = _mm(h, w_ref[...])
    kg = kg_ref[...]
    for hd in range(MEM_HEADS):
        sl = slice(hd * MEM_HD, (hd + 1) * MEM_HD)
        mk_ref[:, sl] = _rms(kv[:, sl], kg)
    mv_ref[...] = kv[:, MEM_W:]


def _memkv_call(mem, mem_norm, w_mem_kv, mem_k_norm):
    depth = w_mem_kv.shape[0]
    out = jax.ShapeDtypeStruct((depth, N_MEM, MEM_W), F32)
    return pl.pallas_call(
        _memkv_body,
        grid=(depth,),
        in_specs=[
            pl.BlockSpec((N_MEM, D_MODEL), lambda i: (0, 0)),
            pl.BlockSpec((None, 1, D_MODEL), lambda i: (i, 0, 0)),
            pl.BlockSpec((None, D_MODEL, 2 * MEM_W), lambda i: (i, 0, 0)),
            pl.BlockSpec((None, 1, MEM_HD), lambda i: (i, 0, 0)),
        ],
        out_specs=[pl.BlockSpec((None, N_MEM, MEM_W), lambda i: (i, 0, 0))] * 2,
        out_shape=[out, out],
        compiler_params=_params(("arbitrary",)),
        name="memkv",
    )(mem, mem_norm.reshape(depth, 1, D_MODEL), w_mem_kv, mem_k_norm.reshape(depth, 1, MEM_HD))


def _chunk_time(pos):
    return lax.shift_right_logical(pos, 3) + SUBLANES * (pos & (SUBLANES - 1))


def _chunk_perm(tm):
    pos = jnp.arange(tm)
    src = (pos // CHUNK) * CHUNK + (pos % CHUNK) // SUBLANES + SUBLANES * (pos % SUBLANES)
    return (src[:, None] == pos[None, :]).astype(BF16)


def _proj_a_matmuls(x_ref, g_ref, perm_ref, wqkv_ref, wz_ref, wba_ref, wqm_ref,
                    z_ref, qm_ref, raw_ref, ba_ref):
    h = _rms(x_ref[...], g_ref[...]).astype(BF16)
    h = jnp.dot(perm_ref[...], h, preferred_element_type=F32).astype(BF16)
    z_ref[...] = jnp.dot(h, wz_ref[...], preferred_element_type=F32)
    qm_ref[...] = jnp.dot(h, wqm_ref[...], preferred_element_type=F32)
    ba_ref[...] = jnp.dot(h, wba_ref[...], preferred_element_type=F32)
    raw_ref[...] = jnp.dot(h, wqkv_ref[...], preferred_element_type=F32)


def _proj_a_rows(seqs, raw_ref, ba_ref, cw_ref, alog_ref, dtb_ref,
                 q_ref, k_ref, kb_ref, qd_ref, kd_ref, vb_ref, kbe_ref, cum_ref, cumt_ref,
                 convn_ref, hist_ref):
    c = CHUNK
    tm = raw_ref.shape[0]
    rows = tm // seqs
    n_hist = (GDN_CONV - 1) * SUBLANES

    t_row = _chunk_time(lax.broadcasted_iota(jnp.int32, (c, c), 0))
    t_col = _chunk_time(lax.broadcasted_iota(jnp.int32, (c, c), 1))
    tril = (t_row >= t_col).astype(F32)
    eye_l = (lax.broadcasted_iota(jnp.int32, (LANES, LANES), 0)
             == lax.broadcasted_iota(jnp.int32, (LANES, LANES), 1)).astype(F32)
    sub0 = lax.broadcasted_iota(jnp.int32, (SUBLANES, LANES), 0) == 0

    ba = ba_ref[...]
    beta_all = _sigmoid(ba)
    g_all = -jnp.exp(alog_ref[...]) * _softplus(ba + dtb_ref[...])

    for j in range(tm // c):
        rs = slice(j * c, (j + 1) * c)
        s, lr = (j * c) // rows, (j * c) % rows
        cum = jnp.dot(tril, g_all[rs], precision=_HI, preferred_element_type=F32)
        cum_ref[rs, :] = cum
        cumt_ref[j] = lax.dot_general(eye_l, cum, (((1,), (1,)), ((), ())), precision=_HI,
                                      preferred_element_type=F32)
        e_cum = jnp.exp(cum)
        e_rest = jnp.exp(cum[c - 1:c, :] - cum)
        beta = beta_all[rs]

        def conv_act(lo):
            sl = slice(lo, lo + LANES)
            x = raw_ref[rs, sl]
            prev = (hist_ref[s, :, sl] if lr == 0
                    else raw_ref[j * c - n_hist:j * c, sl])
            shifted = [jnp.where(sub0,
                                 pltpu.roll(prev[i * SUBLANES:(i + 1) * SUBLANES], 1, 0),
                                 pltpu.roll(x[c - n_hist + i * SUBLANES:c - n_hist + (i + 1) * SUBLANES],
                                            1, 0)) for i in range(GDN_CONV - 1)]
            acc = x * cw_ref[GDN_CONV - 1:GDN_CONV, sl]
            for d in range(1, GDN_CONV):
                xd = jnp.concatenate(shifted[GDN_CONV - 1 - d:] + [x[0:c - d * SUBLANES]], axis=0)
                acc = acc + xd * cw_ref[GDN_CONV - 1 - d:GDN_CONV - d, sl]
            return _silu(acc)

        for hd in range(GDN_HEADS):
            hs = slice(hd * GDN_DK, (hd + 1) * GDN_DK)
            b_col = beta[:, hd:hd + 1]
            ec_col = e_cum[:, GDN_HEADS + hd:GDN_HEADS + hd + 1]
            er_col = e_rest[:, GDN_HEADS + hd:GDN_HEADS + hd + 1]
            q = conv_act(hd * GDN_DK)
            q = q * lax.rsqrt(jnp.sum(q * q, axis=-1, keepdims=True) + EPS) * (GDN_DK ** -0.5)
            k = conv_act(GDN_HEADS * GDN_DK + hd * GDN_DK)
            k = k * lax.rsqrt(jnp.sum(k * k, axis=-1, keepdims=True) + EPS)
            v = conv_act(2 * GDN_HEADS * GDN_DK + hd * GDN_DV)
            kb = k * b_col
            q_ref[rs, hs] = q.astype(BF16)
            qd_ref[rs, hs] = (q * ec_col).astype(BF16)
            k_ref[rs, hs] = k.astype(BF16)
            kb_ref[rs, hs] = kb.astype(BF16)
            kd_ref[rs, hs] = (k * er_col).astype(BF16)
            kbe_ref[rs, hs] = kb * ec_col
            vb_ref[rs, hs] = v * b_col

    for s in range(seqs):
        tail = raw_ref[(s + 1) * rows - n_hist:(s + 1) * rows, :]
        hist_ref[s] = tail
        convn_ref[s] = tail


def _proj_a_body(seqs, tiles_per_seq, x_ref, g_ref, perm_ref, wqkv_ref, wz_ref, wba_ref, wqm_ref,
                 conv0_ref, cw_ref, alog_ref, dtb_ref,
                 z_ref, qm_ref, q_ref, k_ref, kb_ref, qd_ref, kd_ref, vb_ref, kbe_ref, cum_ref,
                 cumt_ref, convn_ref, raw0_ref, raw1_ref, ba0_ref, ba1_ref, hist_ref):
    i = pl.program_id(0)

    @pl.when(i == 0)
    def _():
        raw1_ref[...] = jnp.zeros_like(raw1_ref)
        ba1_ref[...] = jnp.zeros_like(ba1_ref)
        hist_ref[...] = jnp.zeros_like(hist_ref)

    @pl.when((i + tiles_per_seq - 1) % tiles_per_seq == 0)
    def _():
        hist_ref[...] = conv0_ref[...]

    def step(raw_w, ba_w, raw_r, ba_r):
        _proj_a_matmuls(x_ref, g_ref, perm_ref, wqkv_ref, wz_ref, wba_ref, wqm_ref,
                        z_ref, qm_ref, raw_w, ba_w)
        _proj_a_rows(seqs, raw_r, ba_r, cw_ref, alog_ref, dtb_ref,
                     q_ref, k_ref, kb_ref, qd_ref, kd_ref, vb_ref, kbe_ref, cum_ref, cumt_ref,
                     convn_ref, hist_ref)

    @pl.when(i % 2 == 0)
    def _():
        step(raw0_ref, ba0_ref, raw1_ref, ba1_ref)

    @pl.when(i % 2 == 1)
    def _():
        step(raw1_ref, ba1_ref, raw0_ref, ba0_ref)


def _proj_a_call(x, g, w, conv0, tm, seqs):
    rows = x.shape[0]
    ns = conv0.shape[0]
    n_tiles = rows // tm
    n_hist = (GDN_CONV - 1) * SUBLANES
    tiles_per_seq = rows // (ns * tm) if seqs == 1 else 1
    seq_of = (lambda i: i // tiles_per_seq) if seqs == 1 else (lambda i: i)
    cur = lambda i: jnp.minimum(i, n_tiles - 1)
    prev = lambda i: jnp.maximum(i - 1, 0)
    tile = lambda width, which: pl.BlockSpec((tm, width), lambda i: (which(i), 0))
    hist = pl.BlockSpec((seqs, n_hist, GDN_QKV), lambda i: (seq_of(prev(i)), 0, 0))
    wide_bf = jax.ShapeDtypeStruct((rows, GDN_VW), BF16)
    wide_f32 = jax.ShapeDtypeStruct((rows, GDN_VW), F32)
    perm = _chunk_perm(tm)
    return pl.pallas_call(
        functools.partial(_proj_a_body, seqs, tiles_per_seq),
        grid=(n_tiles + 1,),
        in_specs=[tile(D_MODEL, cur), _const_spec((1, D_MODEL)), _const_spec(perm.shape)]
        + [_const_spec(w[name].shape) for name in ("a_qkv", "a_z", "a_ba", "a_qm")]
        + [hist, _const_spec((GDN_CONV, GDN_QKV)), _const_spec((1, LANES)), _const_spec((1, LANES))],
        out_specs=[tile(GDN_VW, cur), tile(MEM_W, cur)] + [tile(GDN_VW, prev)] * 7
        + [tile(LANES, prev),
           pl.BlockSpec((tm // CHUNK, LANES, CHUNK), lambda i: (prev(i), 0, 0)), hist],
        out_shape=[wide_f32, jax.ShapeDtypeStruct((rows, MEM_W), F32)]
        + [wide_bf] * 5 + [wide_f32] * 2
        + [jax.ShapeDtypeStruct((rows, LANES), F32),
           jax.ShapeDtypeStruct((rows // CHUNK, LANES, CHUNK), F32),
           jax.ShapeDtypeStruct((ns, n_hist, GDN_QKV), F32)],
        scratch_shapes=[pltpu.VMEM((tm, GDN_QKV), F32)] * 2 + [pltpu.VMEM((tm, LANES), F32)] * 2
        + [pltpu.VMEM((seqs, n_hist, GDN_QKV), F32)],
        compiler_params=_params(("arbitrary",)),
        name="proj_a",
    )(x, g.reshape(1, D_MODEL), perm, w["a_qkv"], w["a_z"], w["a_ba"], w["a_qm"], conv0,
      w["conv_w_a"], w["alog_row"], w["dtb_row"])


def _gdn_body(q_ref, k_ref, kb_ref, qd_ref, kd_ref, vb_ref, kbe_ref, cum_ref, cumt_ref, s0_ref,
              o_ref, sn_ref, s_ref):
    c = CHUNK
    nc = q_ref.shape[0] // c

    @pl.when(pl.program_id(1) == 0)
    def _():
        s_ref[...] = s0_ref[...]

    row = _chunk_time(lax.broadcasted_iota(jnp.int32, (c, c), 0))
    col = _chunk_time(lax.broadcasted_iota(jnp.int32, (c, c), 1))
    causal = row >= col
    strict = row > col

    pairs = [(j, hd) for j in range(nc) for hd in range(GDN_HEADS)]
    rs = lambda j: slice(j * c, (j + 1) * c)
    hs = lambda hd: slice(hd * GDN_DK, (hd + 1) * GDN_DK)
    gl = lambda hd: slice(GDN_HEADS + hd, GDN_HEADS + hd + 1)

    decays = [jnp.exp(jnp.where(causal, cum_ref[rs(j), gl(hd)] - cumt_ref[j, gl(hd), :], -jnp.inf))
              for j, hd in pairs]
    grams = [_mm_nt(jnp.concatenate([kb_ref[rs(j), hs(hd)], q_ref[rs(j), hs(hd)]], axis=0),
                    k_ref[rs(j), hs(hd)]) for j, hd in pairs]
    ps = [jnp.where(strict, -(g[:c] * d), 0.0) for g, d in zip(grams, decays)]
    qks = [g[c:] * d for g, d in zip(grams, decays)]

    ns = ps
    for _ in range(5):
        ps = [_mm(p, p) for p in ps]
        ns = [n + p + _mm(n, p) for n, p in zip(ns, ps)]

    rhss = [jnp.concatenate([vb_ref[rs(j), hs(hd)], kbe_ref[rs(j), hs(hd)]], axis=1)
            for j, hd in pairs]
    sols = [rhs + _mm(n, rhs) for n, rhs in zip(ns, rhss)]

    ss = [s_ref[hd] for hd in range(GDN_HEADS)]
    for j in range(nc):
        sol_j = sols[j * GDN_HEADS:(j + 1) * GDN_HEADS]
        qk_j = qks[j * GDN_HEADS:(j + 1) * GDN_HEADS]
        r1s = [_mm(jnp.concatenate([sol_j[hd][:, GDN_DV:].astype(BF16), qd_ref[rs(j), hs(hd)]],
                                   axis=0), ss[hd]) for hd in range(GDN_HEADS)]
        us = [sol_j[hd][:, :GDN_DV] - r1s[hd][:c] for hd in range(GDN_HEADS)]
        for hd in range(GDN_HEADS):
            o_ref[rs(j), hs(hd)] = r1s[hd][c:] + _mm(qk_j[hd], us[hd])
        ss = [ss[hd] * jnp.exp(cum_ref[j * c + c - 1:j * c + c, gl(hd)])
              + _mm_tn(kd_ref[rs(j), hs(hd)], us[hd]) for hd in range(GDN_HEADS)]
    for hd in range(GDN_HEADS):
        s_ref[hd] = ss[hd]
        sn_ref[hd] = ss[hd]


def _gdn_call(ops, s0, ns, tr):
    t = ops[0].shape[0] // ns
    seq = lambda width: pl.BlockSpec((None, tr, width), lambda s, i: (s, i, 0))
    state = pl.BlockSpec((None, GDN_HEADS, GDN_DK, GDN_DV), lambda s, i: (s, 0, 0, 0))
    args = [a.reshape(ns, t, a.shape[-1]) for a in ops[:8]]
    args.append(ops[8].reshape(ns, t // CHUNK, LANES, CHUNK))
    return pl.pallas_call(
        _gdn_body,
        grid=(ns, t // tr),
        in_specs=[seq(GDN_VW)] * 7
        + [seq(LANES), pl.BlockSpec((None, tr // CHUNK, LANES, CHUNK), lambda s, i: (s, i, 0, 0)),
           state],
        out_specs=[seq(GDN_VW), state],
        out_shape=[jax.ShapeDtypeStruct((ns, t, GDN_VW), F32),
                   jax.ShapeDtypeStruct((ns, GDN_HEADS, GDN_DK, GDN_DV), F32)],
        scratch_shapes=[pltpu.VMEM((GDN_HEADS, GDN_DK, GDN_DV), F32)],
        compiler_params=_params(("arbitrary", "arbitrary")),
        name="gdn",
    )(*args, s0)


def _proj_b_body(x_ref, g_ref, wq_ref, wkv_ref, wqm_ref, qg_ref, kg_ref,
                 qlo_ref, qhi_ref, kd_ref, vd_ref, kc_ref, vc_ref, qm_ref):
    h = _rms(x_ref[...], g_ref[...]).astype(BF16)
    qm_ref[...] = jnp.dot(h, wqm_ref[...], preferred_element_type=F32)
    q = jnp.dot(h, wq_ref[...], preferred_element_type=F32)
    kv = jnp.dot(h, wkv_ref[...], preferred_element_type=F32)
    lo = lax.broadcasted_iota(jnp.int32, (1, LANES), 1) < SWA_HD
    qg = qg_ref[...] * (SWA_HD ** -0.5)
    for p in range(SWA_QW // LANES):
        sl = slice(p * LANES, (p + 1) * LANES)
        x = q[:, sl]
        x2 = x * x
        m_lo = jnp.sum(jnp.where(lo, x2, 0.0), axis=-1, keepdims=True) * (1.0 / SWA_HD)
        m_hi = jnp.sum(jnp.where(lo, 0.0, x2), axis=-1, keepdims=True) * (1.0 / SWA_HD)
        qn = x * jnp.where(lo, lax.rsqrt(m_lo + EPS), lax.rsqrt(m_hi + EPS)) * qg
        qlo_ref[:, sl] = jnp.where(lo, qn, 0.0).astype(BF16)
        qhi_ref[:, sl] = jnp.where(lo, 0.0, qn).astype(BF16)
    kg = kg_ref[...]
    for p in range(SWA_KV_HEADS):
        sl = slice(p * LANES, (p + 1) * LANES)
        kn = _rms(kv[:, sl], kg)
        kc_ref[:, sl] = kn
        kd_ref[:, sl] = kn.astype(BF16)
    v = kv[:, SWA_KV_HEADS * LANES:]
    vc_ref[...] = v
    vd_ref[...] = v.astype(BF16)


def _proj_b_call(x, g, w, tm, keep_all):
    rows = x.shape[0]
    dup_w = SWA_KV_HEADS * LANES
    tile = lambda width: pl.BlockSpec((tm, width), lambda i: (i, 0))
    cache = tile(dup_w) if keep_all else pl.BlockSpec((tm, dup_w), lambda i: (0, 0))
    cache_shape = jax.ShapeDtypeStruct((rows if keep_all else tm, dup_w), F32)
    return pl.pallas_call(
        _proj_b_body,
        grid=(rows // tm,),
        in_specs=[tile(D_MODEL), _const_spec((1, D_MODEL))]
        + [_const_spec(w[name].shape) for name in ("b_q", "b_kv", "b_qm")]
        + [_const_spec((1, LANES))] * 2,
        out_specs=[tile(SWA_QW)] * 2 + [tile(dup_w)] * 2 + [cache] * 2 + [tile(MEM_W)],
        out_shape=[jax.ShapeDtypeStruct((rows, SWA_QW), BF16)] * 2
        + [jax.ShapeDtypeStruct((rows, dup_w), BF16)] * 2 + [cache_shape] * 2
        + [jax.ShapeDtypeStruct((rows, MEM_W), F32)],
        compiler_params=_params(("arbitrary",)),
        name="proj_b",
    )(x, g.reshape(1, D_MODEL), w["b_q"], w["b_kv"], w["b_qm"], w["q_gain2"], w["k_gain2"])


def _swa_body(hist_valid, qlo_ref, qhi_ref, k_ref, v_ref, kh_ref, vh_ref, sink_ref,
              o_ref, kwin_ref, vwin_ref):
    t = pl.program_id(1)
    c = CHUNK
    tr = qlo_ref.shape[0]

    @pl.when(t == 0)
    def _():
        kwin_ref[0:WINDOW, :] = kh_ref[...]
        vwin_ref[0:WINDOW, :] = vh_ref[...]

    kwin_ref[WINDOW:WINDOW + tr, :] = k_ref[...]
    vwin_ref[WINDOW:WINDOW + tr, :] = v_ref[...]

    kv_heads = range(SWA_KV_HEADS)
    sinks = [jnp.concatenate([jnp.full((c, 1), sink_ref[0, h * SWA_GROUP + g], F32)
                              for g in range(SWA_GROUP)], axis=0) for h in kv_heads]
    lo = lax.broadcasted_iota(jnp.int32, (1, LANES), 1) < SWA_HD
    zero = jnp.zeros((), BF16)

    pairs = [(j, h) for j in range(tr // c) for h in kv_heads]
    slab = lambda p: slice(p * LANES, (p + 1) * LANES)
    rows = lambda j: slice(j * c, (j + 1) * c)
    win = lambda j: slice(j * c, j * c + WINDOW + c)
    qss = [jnp.concatenate([qlo_ref[rows(j), slab(2 * h)], qhi_ref[rows(j), slab(2 * h)],
                            qlo_ref[rows(j), slab(2 * h + 1)], qhi_ref[rows(j), slab(2 * h + 1)]],
                           axis=0) for j, h in pairs]
    ss = [_mm_nt(qs, kwin_ref[win(j), slab(h)]) for qs, (j, h) in zip(qss, pairs)]
    if not hist_valid:
        key_col = lax.broadcasted_iota(jnp.int32, (SWA_GROUP * c, WINDOW + c), 1)
        ss = [jnp.where(key_col + (t * tr + j * c - WINDOW) >= 0, s, -jnp.inf)
              if j * c < WINDOW else s for s, (j, h) in zip(ss, pairs)]
    ps = []
    for s, (j, h) in zip(ss, pairs):
        m = jnp.maximum(jnp.max(s, axis=-1, keepdims=True), sinks[h])
        p = jnp.exp(s - m)
        ps.append((p / (jnp.sum(p, axis=-1, keepdims=True) + jnp.exp(sinks[h] - m))).astype(BF16))
    for p, (j, h) in zip(ps, pairs):
        v = vwin_ref[win(j), slab(h)]
        v_lo = jnp.where(lo, v, zero)
        v_hi = jnp.where(lo, zero, v)
        for half in range(2):
            o_ref[rows(j), slab(2 * h + half)] = (
                _mm(p[(2 * half) * c:(2 * half + 1) * c], v_lo)
                + _mm(p[(2 * half + 1) * c:(2 * half + 2) * c], v_hi))

    for ref in (kwin_ref, vwin_ref):
        ref[0:WINDOW, :] = ref[tr:tr + WINDOW, :]


def _swa_call(qlo, qhi, kd, vd, k_hist, v_hist, sinks, hist_valid, ns, tr):
    t = qlo.shape[0] // ns
    dup_w = SWA_KV_HEADS * LANES
    seq = lambda width: pl.BlockSpec((None, tr, width), lambda s, i: (s, i, 0))
    hist = pl.BlockSpec((None, WINDOW, dup_w), lambda s, i: (s, 0, 0))
    return pl.pallas_call(
        functools.partial(_swa_body, hist_valid),
        grid=(ns, t // tr),
        in_specs=[seq(SWA_QW), seq(SWA_QW), seq(dup_w), seq(dup_w), hist, hist,
                  pl.BlockSpec(memory_space=pltpu.SMEM)],
        out_specs=seq(SWA_QW),
        out_shape=jax.ShapeDtypeStruct((ns, t, SWA_QW), F32),
        scratch_shapes=[pltpu.VMEM((WINDOW + tr, dup_w), BF16)] * 2,
        compiler_params=_params(("arbitrary", "arbitrary")),
        name="swa",
    )(qlo.reshape(ns, t, SWA_QW), qhi.reshape(ns, t, SWA_QW), kd.reshape(ns, t, dup_w),
      vd.reshape(ns, t, dup_w), k_hist, v_hist, sinks.reshape(1, SWA_HEADS))


def _tail_body(seqs, gated, x_ref, om_ref, *refs):
    if gated:
        z_ref, og_ref, unperm_ref = refs[:3]
        refs = refs[3:]
    qm_ref, mk_ref, mv_ref, mqg_ref, wo_ref, gf_ref, wgu_ref, wd_ref, y_ref = refs
    tm = x_ref.shape[0]
    rows = tm // seqs
    d_ff = wd_ref.shape[0]
    mqg = mqg_ref[...] * (MEM_HD ** -0.5)

    mem_parts = []
    for sq in range(seqs):
        heads = []
        for hd in range(MEM_HEADS):
            sl = slice(hd * MEM_HD, (hd + 1) * MEM_HD)
            qh = _rms(qm_ref[sq * rows:(sq + 1) * rows, sl], mqg)
            s = _mm_nt(qh, mk_ref[sq, :, sl])
            p = jnp.exp(s - jnp.max(s, axis=-1, keepdims=True))
            p = p / jnp.sum(p, axis=-1, keepdims=True)
            heads.append(_mm(p, mv_ref[sq, :, sl]))
        mem_parts.append(jnp.concatenate(heads, axis=1))
    o_mem = mem_parts[0] if seqs == 1 else jnp.concatenate(mem_parts, axis=0)

    if gated:
        og = og_ref[...]
        parts = [(_rms(om_ref[:, hd * GDN_DV:(hd + 1) * GDN_DV], og)
                  * _silu(z_ref[:, hd * GDN_DV:(hd + 1) * GDN_DV])).astype(BF16)
                 for hd in range(GDN_HEADS)]
        mixed = jnp.concatenate(parts + [o_mem.astype(BF16)], axis=1)
        mixed = jnp.dot(unperm_ref[...], mixed, preferred_element_type=F32).astype(BF16)
        x = x_ref[...] + jnp.dot(mixed, wo_ref[...], preferred_element_type=F32)
    else:
        mix_w = om_ref.shape[1]
        x = x_ref[...] + _mm(om_ref[...], wo_ref[0:mix_w, :]) + _mm(o_mem, wo_ref[mix_w:, :])
    h = _rms(x, gf_ref[...])
    gu = _mm(h, wgu_ref[...])
    act = _silu(gu[:, :d_ff]) * gu[:, d_ff:]
    y_ref[...] = x + _mm(act, wd_ref[...])


def _tail_call(x, o_mix, gate, qm, mk, mv, mem_q_gain, w_out, g_ffn, w_gate_up, w_down, tm, seqs):
    rows = x.shape[0]
    mix_w = o_mix.shape[1]
    tile = lambda width: pl.BlockSpec((tm, width), lambda i: (i, 0))
    tiles_per_seq = rows // (tm * mk.shape[0]) if seqs == 1 else 1
    mem = pl.BlockSpec((seqs, N_MEM, MEM_W), lambda i: (i // tiles_per_seq, 0, 0))
    gate_specs = [tile(mix_w), _const_spec((1, GDN_DV)), _const_spec((tm, tm))] if gate else []
    gate_args = [gate[0], gate[1].reshape(1, GDN_DV), _chunk_perm(tm).T] if gate else []
    return pl.pallas_call(
        functools.partial(_tail_body, seqs, bool(gate)),
        grid=(rows // tm,),
        in_specs=[tile(D_MODEL), tile(mix_w)] + gate_specs + [
            tile(MEM_W), mem, mem,
            _const_spec((1, MEM_HD)), _const_spec(w_out.shape), _const_spec((1, D_MODEL)),
            _const_spec(w_gate_up.shape), _const_spec(w_down.shape),
        ],
        out_specs=tile(D_MODEL),
        out_shape=jax.ShapeDtypeStruct((rows, D_MODEL), F32),
        compiler_params=_params(("arbitrary",)),
        name="tail",
    )(x, o_mix, *gate_args, qm, mk, mv, mem_q_gain.reshape(1, MEM_HD), w_out,
      g_ffn.reshape(1, D_MODEL), w_gate_up, w_down)


def _lane_row(vals, offset):
    return jnp.zeros((1, LANES), F32).at[0, offset:offset + vals.shape[0]].set(vals)


def _dup_heads(a):
    lead = a.shape[:-1]
    a = a.reshape(lead + (SWA_KV_HEADS, 1, SWA_HD))
    return jnp.broadcast_to(a, lead + (SWA_KV_HEADS, 2, SWA_HD)).reshape(lead + (SWA_KV_HEADS * LANES,))


def _undup_heads(a):
    lead = a.shape[:-1]
    return a.reshape(lead + (SWA_KV_HEADS, 2, SWA_HD))[..., 0, :]


def _trunk(x, mem_k, mem_v, gdn_conv, gdn_state, swa_k_hist, swa_v_hist, hist_valid, w):
    ns, t, _ = x.shape
    rows = ns * t
    tm = ROW_TILE
    seqs = max(1, tm // t)
    x2 = x.reshape(rows, D_MODEL)

    conv0 = jnp.pad(gdn_conv[:, :, None, :], ((0, 0), (0, 0), (SUBLANES - 1, 0), (0, 0)))
    conv0 = conv0.reshape(ns, (GDN_CONV - 1) * SUBLANES, GDN_QKV)
    z, qm, *gdn_ops, conv_new = _proj_a_call(x2, w["norm_mix"][0], w, conv0, tm, seqs)
    conv_new = conv_new[:, SUBLANES - 1::SUBLANES, :]
    o_raw, s_new = _gdn_call(gdn_ops, gdn_state, ns, min(t, GDN_TILE_ROWS))
    x2 = _tail_call(x2, o_raw.reshape(rows, GDN_VW), (z, w["o_norm_a"]), qm, mem_k[0], mem_v[0],
                    w["mem_q_norm"][0], w["out_a"], w["norm_ffn"][0], w["gate_up"][0], w["down"][0],
                    tm, seqs)

    qlo, qhi, kd, vd, k_cache, v_cache, qm = _proj_b_call(x2, w["norm_mix"][1], w, tm, ns > 1)
    o_mix = _swa_call(qlo, qhi, kd, vd, _dup_heads(swa_k_hist).astype(BF16),
                      _dup_heads(swa_v_hist).astype(BF16), w["sinks_b"], hist_valid, ns,
                      min(t, SWA_TILE_ROWS))
    x2 = _tail_call(x2, o_mix.reshape(rows, SWA_QW), None, qm, mem_k[1], mem_v[1],
                    w["mem_q_norm"][1], w["out_b"], w["norm_ffn"][1], w["gate_up"][1], w["down"][1],
                    tm, seqs)

    keep = min(WINDOW, t) if ns == 1 else t
    k_new = _undup_heads(k_cache).reshape(ns, -1, SWA_KVW)[:, -keep:]
    v_new = _undup_heads(v_cache).reshape(ns, -1, SWA_KVW)[:, -keep:]
    return x2.reshape(ns, t, D_MODEL), conv_new, s_new, k_new, v_new


def kernel(x_prompt, x_sample, mem_prompt, cache_mem_k, cache_mem_v, state_gdn, state_gdn_conv, cache_swa_k, cache_swa_v, norm_mix, norm_ffn, mem_norm, w_mem_kv, mem_q_norm, mem_k_norm, w_in_a, conv_w_a, a_log, dt_bias, o_norm_a, w_out_a, w_in_b, q_norm_b, k_norm_b, sinks_b, w_out_b, w_gate_up, w_down):
    bsz = x_prompt.shape[0]
    dec = x_sample.shape[0]
    wa = w_in_a[0]
    z_lo = GDN_QKV
    ba_lo = GDN_QKV + GDN_VW
    qm_lo = ba_lo + 2 * GDN_HEADS
    wb = w_in_b[0]
    w = {
        "norm_mix": norm_mix, "norm_ffn": norm_ffn, "mem_q_norm": mem_q_norm,
        "a_qkv": wa[:, :z_lo].astype(BF16), "a_z": wa[:, z_lo:ba_lo].astype(BF16),
        "a_ba": jnp.pad(wa[:, ba_lo:qm_lo], ((0, 0), (0, LANES - 2 * GDN_HEADS))).astype(BF16),
        "a_qm": wa[:, qm_lo:].astype(BF16),
        "b_q": wb[:, :SWA_QW].astype(BF16),
        "b_kv": jnp.concatenate([_dup_heads(wb[:, SWA_QW:SWA_QW + SWA_KVW]),
                                 _dup_heads(wb[:, SWA_QW + SWA_KVW:SWA_QW + 2 * SWA_KVW])],
                                axis=1).astype(BF16),
        "b_qm": wb[:, SWA_QW + 2 * SWA_KVW:].astype(BF16),
        "q_gain2": jnp.tile(q_norm_b[0], 2).reshape(1, LANES),
        "k_gain2": jnp.tile(k_norm_b[0], 2).reshape(1, LANES),
        "conv_w_a": conv_w_a[0], "o_norm_a": o_norm_a[0],
        "alog_row": _lane_row(a_log[0], GDN_HEADS), "dtb_row": _lane_row(dt_bias[0], GDN_HEADS),
        "out_a": w_out_a[0].astype(BF16), "out_b": w_out_b[0].astype(BF16),
        "sinks_b": sinks_b[0],
        "gate_up": w_gate_up.astype(BF16), "down": w_down.astype(BF16),
    }

    mk, mv = _memkv_call(mem_prompt[0], mem_norm, w_mem_kv, mem_k_norm)
    depth = mk.shape[0]
    new_mem_k = mk.reshape(depth, bsz, N_MEM, MEM_HEADS, MEM_HD)
    new_mem_v = mv.reshape(depth, bsz, N_MEM, MEM_HEADS, MEM_HD)

    zero_conv = jnp.zeros((bsz, GDN_CONV - 1, GDN_QKV), F32)
    zero_state = jnp.zeros((bsz, GDN_HEADS, GDN_DK, GDN_DV), F32)
    zero_hist = jnp.zeros((bsz, WINDOW, SWA_KVW), F32)
    y_p, conv_p, state_p, k_p, v_p = _trunk(
        x_prompt, mk.reshape(depth, bsz, N_MEM, MEM_W), mv.reshape(depth, bsz, N_MEM, MEM_W),
        zero_conv, zero_state, zero_hist, zero_hist, False, w)

    y_s, conv_s, state_s, k_s, v_s = _trunk(
        x_sample, cache_mem_k.reshape(depth, dec, N_MEM, MEM_W),
        cache_mem_v.reshape(depth, dec, N_MEM, MEM_W), state_gdn_conv[0], state_gdn[0],
        cache_swa_k[0].reshape(dec, WINDOW, SWA_KVW), cache_swa_v[0].reshape(dec, WINDOW, SWA_KVW),
        True, w)

    kv_shape = lambda a: a.reshape(a.shape[0], a.shape[1], SWA_KV_HEADS, SWA_HD)[None]
    return (y_p, y_s, state_p[None], conv_p[None], state_s[None], conv_s[None],
            kv_shape(k_p), kv_shape(v_p), kv_shape(k_s), kv_shape(v_s), new_mem_k, new_mem_v)
```

```python
import functools

import jax
import jax.numpy as jnp
import numpy as np
from jax import lax
from jax.experimental import pallas as pl
from jax.experimental.pallas import tpu as pltpu

F32 = jnp.float32
BF16 = jnp.bfloat16

D_MODEL = 1024
CHUNK = 64
EPS = 1e-6
GDN_HEADS = 8
GDN_DK = 128
GDN_DV = 128
GDN_CONV = 4
GDN_QKV = GDN_HEADS * (2 * GDN_DK + GDN_DV)
GDN_VW = GDN_HEADS * GDN_DV
SWA_HEADS = 16
SWA_KV_HEADS = 4
SWA_HD = 64
SWA_GROUP = SWA_HEADS // SWA_KV_HEADS
SWA_QW = SWA_HEADS * SWA_HD
SWA_KVW = SWA_KV_HEADS * SWA_HD
WINDOW = 128
N_MEM = 256
MEM_HEADS = 4
MEM_HD = 128
MEM_W = MEM_HEADS * MEM_HD
LANES = 128
SUBLANES = 8
VMEM_LIMIT = 56 * 1024 * 1024
ROW_TILE = 256
SWA_TILE_ROWS = 4 * CHUNK
GDN_TILE_ROWS = 4 * CHUNK

_HI = lax.Precision.HIGHEST


def _mm(a, b):
    return jnp.dot(a.astype(BF16), b.astype(BF16), preferred_element_type=F32)


def _mm_nt(a, b):
    return lax.dot_general(a.astype(BF16), b.astype(BF16), (((1,), (1,)), ((), ())),
                           preferred_element_type=F32)


def _mm_tn(a, b):
    return lax.dot_general(a.astype(BF16), b.astype(BF16), (((0,), (0,)), ((), ())),
                           preferred_element_type=F32)


def _rms(x, g):
    return x * lax.rsqrt(jnp.mean(x * x, axis=-1, keepdims=True) + EPS) * g


def _sigmoid(x):
    return 1.0 / (1.0 + jnp.exp(-x))


def _silu(x):
    return x * _sigmoid(x)


def _softplus(x):
    return jnp.maximum(x, 0.0) + jnp.log1p(jnp.exp(-jnp.abs(x)))


def _const_spec(shape):
    nd = len(shape)
    return pl.BlockSpec(shape, lambda *_: (0,) * nd, pipeline_mode=pl.Buffered(1))


def _params(sem):
    return pltpu.CompilerParams(dimension_semantics=sem, vmem_limit_bytes=VMEM_LIMIT)


def _memkv_body(mem_ref, g_ref, w_ref, kg_ref, mk_ref, mv_ref):
    h = _rms(mem_ref[...], g_ref[...])
    kv = _mm(h, w_ref[...])
    kg = kg_ref[...]
    for hd in range(MEM_HEADS):
        sl = slice(hd * MEM_HD, (hd + 1) * MEM_HD)
        mk_ref[:, sl] = _rms(kv[:, sl], kg)
    mv_ref[...] = kv[:, MEM_W:]


def _memkv_call(mem, mem_norm, w_mem_kv, mem_k_norm):
    depth = w_mem_kv.shape[0]
    out = jax.ShapeDtypeStruct((depth, N_MEM, MEM_W), F32)
    return pl.pallas_call(
        _memkv_body,
        grid=(depth,),
        in_specs=[
            pl.BlockSpec((N_MEM, D_MODEL), lambda i: (0, 0)),
            pl.BlockSpec((None, 1, D_MODEL), lambda i: (i, 0, 0)),
            pl.BlockSpec((None, D_MODEL, 2 * MEM_W), lambda i: (i, 0, 0)),
            pl.BlockSpec((None, 1, MEM_HD), lambda i: (i, 0, 0)),
        ],
        out_specs=[pl.BlockSpec((None, N_MEM, MEM_W), lambda i: (i, 0, 0))] * 2,
        out_shape=[out, out],
        compiler_params=_params(("arbitrary",)),
        name="memkv",
    )(mem, mem_norm.reshape(depth, 1, D_MODEL), w_mem_kv, mem_k_norm.reshape(depth, 1, MEM_HD))


def _chunk_time(pos):
    return lax.shift_right_logical(pos, 3) + SUBLANES * (pos & (SUBLANES - 1))


def _chunk_perm(tm):
    pos = np.arange(tm)
    src = (pos // CHUNK) * CHUNK + (pos % CHUNK) // SUBLANES + SUBLANES * (pos % SUBLANES)
    return jnp.asarray(src[:, None] == pos[None, :], BF16)


def _proj_a_matmuls(x_ref, g_ref, perm_ref, wqkv_ref, wz_ref, wba_ref, wqm_ref,
                    z_ref, qm_ref, raw_ref, ba_ref):
    h = _rms(x_ref[...], g_ref[...]).astype(BF16)
    h = jnp.dot(perm_ref[...], h, preferred_element_type=F32).astype(BF16)
    z_ref[...] = jnp.dot(h, wz_ref[...], preferred_element_type=F32)
    qm_ref[...] = jnp.dot(h, wqm_ref[...], preferred_element_type=F32)
    ba_ref[...] = jnp.dot(h, wba_ref[...], preferred_element_type=F32)
    raw_ref[...] = jnp.dot(h, wqkv_ref[...], preferred_element_type=F32)


def _proj_a_rows(seqs, raw_ref, ba_ref, cw_ref, alog_ref, dtb_ref,
                 q_ref, k_ref, kb_ref, qd_ref, kd_ref, vb_ref, kbe_ref, cum_ref, cumt_ref,
                 convn_ref, hist_ref):
    c = CHUNK
    tm = raw_ref.shape[0]
    rows = tm // seqs
    n_hist = (GDN_CONV - 1) * SUBLANES

    t_row = _chunk_time(lax.broadcasted_iota(jnp.int32, (c, c), 0))
    t_col = _chunk_time(lax.broadcasted_iota(jnp.int32, (c, c), 1))
    tril = (t_row >= t_col).astype(F32)
    eye_l = (lax.broadcasted_iota(jnp.int32, (LANES, LANES), 0)
             == lax.broadcasted_iota(jnp.int32, (LANES, LANES), 1)).astype(F32)
    sub0 = lax.broadcasted_iota(jnp.int32, (SUBLANES, LANES), 0) == 0

    ba = ba_ref[...]
    beta_all = _sigmoid(ba)
    g_all = -jnp.exp(alog_ref[...]) * _softplus(ba + dtb_ref[...])

    for j in range(tm // c):
        rs = slice(j * c, (j + 1) * c)
        s, lr = (j * c) // rows, (j * c) % rows
        cum = jnp.dot(tril, g_all[rs], precision=_HI, preferred_element_type=F32)
        cum_ref[rs, :] = cum
        cumt_ref[j] = lax.dot_general(eye_l, cum, (((1,), (1,)), ((), ())), precision=_HI,
                                      preferred_element_type=F32)
        e_cum = jnp.exp(cum)
        e_rest = jnp.exp(cum[c - 1:c, :] - cum)
        beta = beta_all[rs]

        def conv_act(lo):
            sl = slice(lo, lo + LANES)
            x = raw_ref[rs, sl]
            prev = (hist_ref[s, :, sl] if lr == 0
                    else raw_ref[j * c - n_hist:j * c, sl])
            shifted = [jnp.where(sub0,
                                 pltpu.roll(prev[i * SUBLANES:(i + 1) * SUBLANES], 1, 0),
                                 pltpu.roll(x[c - n_hist + i * SUBLANES:c - n_hist + (i + 1) * SUBLANES],
                                            1, 0)) for i in range(GDN_CONV - 1)]
            acc = x * cw_ref[GDN_CONV - 1:GDN_CONV, sl]
            for d in range(1, GDN_CONV):
                xd = jnp.concatenate(shifted[GDN_CONV - 1 - d:] + [x[0:c - d * SUBLANES]], axis=0)
                acc = acc + xd * cw_ref[GDN_CONV - 1 - d:GDN_CONV - d, sl]
            return _silu(acc)

        for hd in range(GDN_HEADS):
            hs = slice(hd * GDN_DK, (hd + 1) * GDN_DK)
            b_col = beta[:, hd:hd + 1]
            ec_col = e_cum[:, GDN_HEADS + hd:GDN_HEADS + hd + 1]
            er_col = e_rest[:, GDN_HEADS + hd:GDN_HEADS + hd + 1]
            q = conv_act(hd * GDN_DK)
            q = q * lax.rsqrt(jnp.sum(q * q, axis=-1, keepdims=True) + EPS) * (GDN_DK ** -0.5)
            k = conv_act(GDN_HEADS * GDN_DK + hd * GDN_DK)
            k = k * lax.rsqrt(jnp.sum(k * k, axis=-1, keepdims=True) + EPS)
            v = conv_act(2 * GDN_HEADS * GDN_DK + hd * GDN_DV)
            kb = k * b_col
            q_ref[rs, hs] = q.astype(BF16)
            qd_ref[rs, hs] = (q * ec_col).astype(BF16)
            k_ref[rs, hs] = k.astype(BF16)
            kb_ref[rs, hs] = kb.astype(BF16)
            kd_ref[rs, hs] = (k * er_col).astype(BF16)
            kbe_ref[rs, hs] = kb * ec_col
            vb_ref[rs, hs] = v * b_col

    for s in range(seqs):
        tail = raw_ref[(s + 1) * rows - n_hist:(s + 1) * rows, :]
        hist_ref[s] = tail
        convn_ref[s] = tail


def _proj_a_body(seqs, tiles_per_seq, x_ref, g_ref, perm_ref, wqkv_ref, wz_ref, wba_ref, wqm_ref,
                 conv0_ref, cw_ref, alog_ref, dtb_ref,
                 z_ref, qm_ref, q_ref, k_ref, kb_ref, qd_ref, kd_ref, vb_ref, kbe_ref, cum_ref,
                 cumt_ref, convn_ref, raw0_ref, raw1_ref, ba0_ref, ba1_ref, hist_ref):
    i = pl.program_id(0)

    @pl.when(i == 0)
    def _():
        raw1_ref[...] = jnp.zeros_like(raw1_ref)
        ba1_ref[...] = jnp.zeros_like(ba1_ref)
        hist_ref[...] = jnp.zeros_like(hist_ref)

    @pl.when((i + tiles_per_seq - 1) % tiles_per_seq == 0)
    def _():
        hist_ref[...] = conv0_ref[...]

    def step(raw_w, ba_w, raw_r, ba_r):
        _proj_a_matmuls(x_ref, g_ref, perm_ref, wqkv_ref, wz_ref, wba_ref, wqm_ref,
                        z_ref, qm_ref, raw_w, ba_w)
        _proj_a_rows(seqs, raw_r, ba_r, cw_ref, alog_ref, dtb_ref,
                     q_ref, k_ref, kb_ref, qd_ref, kd_ref, vb_ref, kbe_ref, cum_ref, cumt_ref,
                     convn_ref, hist_ref)

    @pl.when(i % 2 == 0)
    def _():
        step(raw0_ref, ba0_ref, raw1_ref, ba1_ref)

    @pl.when(i % 2 == 1)
    def _():
        step(raw1_ref, ba1_ref, raw0_ref, ba0_ref)


def _proj_a_call(x, g, w, conv0, tm, seqs):
    rows = x.shape[0]
    ns = conv0.shape[0]
    n_tiles = rows // tm
    n_hist = (GDN_CONV - 1) * SUBLANES
    tiles_per_seq = rows // (ns * tm) if seqs == 1 else 1
    seq_of = (lambda i: i // tiles_per_seq) if seqs == 1 else (lambda i: i)
    cur = lambda i: jnp.minimum(i, n_tiles - 1)
    prev = lambda i: jnp.maximum(i - 1, 0)
    tile = lambda width, which: pl.BlockSpec((tm, width), lambda i: (which(i), 0))
    hist = pl.BlockSpec((seqs, n_hist, GDN_QKV), lambda i: (seq_of(prev(i)), 0, 0))
    wide_bf = jax.ShapeDtypeStruct((rows, GDN_VW), BF16)
    wide_f32 = jax.ShapeDtypeStruct((rows, GDN_VW), F32)
    perm = _chunk_perm(tm)
    return pl.pallas_call(
        functools.partial(_proj_a_body, seqs, tiles_per_seq),
        grid=(n_tiles + 1,),
        in_specs=[tile(D_MODEL, cur), _const_spec((1, D_MODEL)), _const_spec(perm.shape)]
        + [_const_spec(w[name].shape) for name in ("a_qkv", "a_z", "a_ba", "a_qm")]
        + [hist, _const_spec((GDN_CONV, GDN_QKV)), _const_spec((1, LANES)), _const_spec((1, LANES))],
        out_specs=[tile(GDN_VW, cur), tile(MEM_W, cur)] + [tile(GDN_VW, prev)] * 7
        + [tile(LANES, prev),
           pl.BlockSpec((tm // CHUNK, LANES, CHUNK), lambda i: (prev(i), 0, 0)), hist],
        out_shape=[wide_f32, jax.ShapeDtypeStruct((rows, MEM_W), F32)]
        + [wide_bf] * 5 + [wide_f32] * 2
        + [jax.ShapeDtypeStruct((rows, LANES), F32),
           jax.ShapeDtypeStruct((rows // CHUNK, LANES, CHUNK), F32),
           jax.ShapeDtypeStruct((ns, n_hist, GDN_QKV), F32)],
        scratch_shapes=[pltpu.VMEM((tm, GDN_QKV), F32)] * 2 + [pltpu.VMEM((tm, LANES), F32)] * 2
        + [pltpu.VMEM((seqs, n_hist, GDN_QKV), F32)],
        compiler_params=_params(("arbitrary",)),
        name="proj_a",
    )(x, g.reshape(1, D_MODEL), perm, w["a_qkv"], w["a_z"], w["a_ba"], w["a_qm"], conv0,
      w["conv_w_a"], w["alog_row"], w["dtb_row"])


def _gdn_body(q_ref, k_ref, kb_ref, qd_ref, kd_ref, vb_ref, kbe_ref, cum_ref, cumt_ref, s0_ref,
              o_ref, sn_ref, s_ref):
    c = CHUNK
    nc = q_ref.shape[0] // c

    @pl.when(pl.program_id(1) == 0)
    def _():
        s_ref[...] = s0_ref[...]

    row = _chunk_time(lax.broadcasted_iota(jnp.int32, (c, c), 0))
    col = _chunk_time(lax.broadcasted_iota(jnp.int32, (c, c), 1))
    causal = row >= col
    strict = row > col

    pairs = [(j, hd) for j in range(nc) for hd in range(GDN_HEADS)]
    rs = lambda j: slice(j * c, (j + 1) * c)
    hs = lambda hd: slice(hd * GDN_DK, (hd + 1) * GDN_DK)
    gl = lambda hd: slice(GDN_HEADS + hd, GDN_HEADS + hd + 1)

    decays = [jnp.exp(jnp.where(causal, cum_ref[rs(j), gl(hd)] - cumt_ref[j, gl(hd), :], -jnp.inf))
              for j, hd in pairs]
    grams = [_mm_nt(jnp.concatenate([kb_ref[rs(j), hs(hd)], q_ref[rs(j), hs(hd)]], axis=0),
                    k_ref[rs(j), hs(hd)]) for j, hd in pairs]
    ps = [jnp.where(strict, -(g[:c] * d), 0.0) for g, d in zip(grams, decays)]
    qks = [g[c:] * d for g, d in zip(grams, decays)]

    ns = ps
    for _ in range(5):
        ps = [_mm(p, p) for p in ps]
        ns = [n + p + _mm(n, p) for n, p in zip(ns, ps)]

    rhss = [jnp.concatenate([vb_ref[rs(j), hs(hd)], kbe_ref[rs(j), hs(hd)]], axis=1)
            for j, hd in pairs]
    sols = [rhs + _mm(n, rhs) for n, rhs in zip(ns, rhss)]

    ss = [s_ref[hd] for hd in range(GDN_HEADS)]
    for j in range(nc):
        sol_j = sols[j * GDN_HEADS:(j + 1) * GDN_HEADS]
        qk_j = qks[j * GDN_HEADS:(j + 1) * GDN_HEADS]
        r1s = [_mm(jnp.concatenate([sol_j[hd][:, GDN_DV:].astype(BF16), qd_ref[rs(j), hs(hd)]],
                                   axis=0), ss[hd]) for hd in range(GDN_HEADS)]
        us = [sol_j[hd][:, :GDN_DV] - r1s[hd][:c] for hd in range(GDN_HEADS)]
        for hd in range(GDN_HEADS):
            o_ref[rs(j), hs(hd)] = r1s[hd][c:] + _mm(qk_j[hd], us[hd])
        ss = [ss[hd] * jnp.exp(cum_ref[j * c + c - 1:j * c + c, gl(hd)])
              + _mm_tn(kd_ref[rs(j), hs(hd)], us[hd]) for hd in range(GDN_HEADS)]
    for hd in range(GDN_HEADS):
        s_ref[hd] = ss[hd]
        sn_ref[hd] = ss[hd]


def _gdn_call(ops, s0, ns, tr):
    t = ops[0].shape[0] // ns
    seq = lambda width: pl.BlockSpec((None, tr, width), lambda s, i: (s, i, 0))
    state = pl.BlockSpec((None, GDN_HEADS, GDN_DK, GDN_DV), lambda s, i: (s, 0, 0, 0))
    args = [a.reshape(ns, t, a.shape[-1]) for a in ops[:8]]
    args.append(ops[8].reshape(ns, t // CHUNK, LANES, CHUNK))
    return pl.pallas_call(
        _gdn_body,
        grid=(ns, t // tr),
        in_specs=[seq(GDN_VW)] * 7
        + [seq(LANES), pl.BlockSpec((None, tr // CHUNK, LANES, CHUNK), lambda s, i: (s, i, 0, 0)),
           state],
        out_specs=[seq(GDN_VW), state],
        out_shape=[jax.ShapeDtypeStruct((ns, t, GDN_VW), F32),
                   jax.ShapeDtypeStruct((ns, GDN_HEADS, GDN_DK, GDN_DV), F32)],
        scratch_shapes=[pltpu.VMEM((GDN_HEADS, GDN_DK, GDN_DV), F32)],
        compiler_params=_params(("arbitrary", "arbitrary")),
        name="gdn",
    )(*args, s0)


def _proj_b_body(x_ref, g_ref, wq_ref, wkv_ref, wqm_ref, qg_ref, kg_ref,
                 qlo_ref, qhi_ref, kd_ref, vd_ref, kc_ref, vc_ref, qm_ref):
    h = _rms(x_ref[...], g_ref[...]).astype(BF16)
    qm_ref[...] = jnp.dot(h, wqm_ref[...], preferred_element_type=F32)
    q = jnp.dot(h, wq_ref[...], preferred_element_type=F32)
    kv = jnp.dot(h, wkv_ref[...], preferred_element_type=F32)
    lo = lax.broadcasted_iota(jnp.int32, (1, LANES), 1) < SWA_HD
    qg = qg_ref[...] * (SWA_HD ** -0.5)
    for p in range(SWA_QW // LANES):
        sl = slice(p * LANES, (p + 1) * LANES)
        x = q[:, sl]
        x2 = x * x
        m_lo = jnp.sum(jnp.where(lo, x2, 0.0), axis=-1, keepdims=True) * (1.0 / SWA_HD)
        m_hi = jnp.sum(jnp.where(lo, 0.0, x2), axis=-1, keepdims=True) * (1.0 / SWA_HD)
        qn = x * jnp.where(lo, lax.rsqrt(m_lo + EPS), lax.rsqrt(m_hi + EPS)) * qg
        qlo_ref[:, sl] = jnp.where(lo, qn, 0.0).astype(BF16)
        qhi_ref[:, sl] = jnp.where(lo, 0.0, qn).astype(BF16)
    kg = kg_ref[...]
    for p in range(SWA_KV_HEADS):
        sl = slice(p * LANES, (p + 1) * LANES)
        kn = _rms(kv[:, sl], kg)
        kc_ref[:, sl] = kn
        kd_ref[:, sl] = kn.astype(BF16)
    v = kv[:, SWA_KV_HEADS * LANES:]
    vc_ref[...] = v
    vd_ref[...] = v.astype(BF16)


def _proj_b_call(x, g, w, tm, keep_all):
    rows = x.shape[0]
    dup_w = SWA_KV_HEADS * LANES
    tile = lambda width: pl.BlockSpec((tm, width), lambda i: (i, 0))
    cache = tile(dup_w) if keep_all else pl.BlockSpec((tm, dup_w), lambda i: (0, 0))
    cache_shape = jax.ShapeDtypeStruct((rows if keep_all else tm, dup_w), F32)
    return pl.pallas_call(
        _proj_b_body,
        grid=(rows // tm,),
        in_specs=[tile(D_MODEL), _const_spec((1, D_MODEL))]
        + [_const_spec(w[name].shape) for name in ("b_q", "b_kv", "b_qm")]
        + [_const_spec((1, LANES))] * 2,
        out_specs=[tile(SWA_QW)] * 2 + [tile(dup_w)] * 2 + [cache] * 2 + [tile(MEM_W)],
        out_shape=[jax.ShapeDtypeStruct((rows, SWA_QW), BF16)] * 2
        + [jax.ShapeDtypeStruct((rows, dup_w), BF16)] * 2 + [cache_shape] * 2
        + [jax.ShapeDtypeStruct((rows, MEM_W), F32)],
        compiler_params=_params(("arbitrary",)),
        name="proj_b",
    )(x, g.reshape(1, D_MODEL), w["b_q"], w["b_kv"], w["b_qm"], w["q_gain2"], w["k_gain2"])


def _swa_body(hist_valid, qlo_ref, qhi_ref, k_ref, v_ref, kh_ref, vh_ref, sink_ref,
              o_ref, kwin_ref, vwin_ref):
    t = pl.program_id(1)
    c = CHUNK
    tr = qlo_ref.shape[0]

    @pl.when(t == 0)
    def _():
        kwin_ref[0:WINDOW, :] = kh_ref[...]
        vwin_ref[0:WINDOW, :] = vh_ref[...]

    kwin_ref[WINDOW:WINDOW + tr, :] = k_ref[...]
    vwin_ref[WINDOW:WINDOW + tr, :] = v_ref[...]

    kv_heads = range(SWA_KV_HEADS)
    sinks = [jnp.concatenate([jnp.full((c, 1), sink_ref[0, h * SWA_GROUP + g], F32)
                              for g in range(SWA_GROUP)], axis=0) for h in kv_heads]
    lo = lax.broadcasted_iota(jnp.int32, (1, LANES), 1) < SWA_HD
    zero = jnp.zeros((), BF16)

    pairs = [(j, h) for j in range(tr // c) for h in kv_heads]
    slab = lambda p: slice(p * LANES, (p + 1) * LANES)
    rows = lambda j: slice(j * c, (j + 1) * c)
    win = lambda j: slice(j * c, j * c + WINDOW + c)
    qss = [jnp.concatenate([qlo_ref[rows(j), slab(2 * h)], qhi_ref[rows(j), slab(2 * h)],
                            qlo_ref[rows(j), slab(2 * h + 1)], qhi_ref[rows(j), slab(2 * h + 1)]],
                           axis=0) for j, h in pairs]
    ss = [_mm_nt(qs, kwin_ref[win(j), slab(h)]) for qs, (j, h) in zip(qss, pairs)]
    if not hist_valid:
        key_col = lax.broadcasted_iota(jnp.int32, (SWA_GROUP * c, WINDOW + c), 1)
        ss = [jnp.where(key_col + (t * tr + j * c - WINDOW) >= 0, s, -jnp.inf)
              if j * c < WINDOW else s for s, (j, h) in zip(ss, pairs)]
    ps = []
    for s, (j, h) in zip(ss, pairs):
        m = jnp.maximum(jnp.max(s, axis=-1, keepdims=True), sinks[h])
        p = jnp.exp(s - m)
        ps.append((p / (jnp.sum(p, axis=-1, keepdims=True) + jnp.exp(sinks[h] - m))).astype(BF16))
    for p, (j, h) in zip(ps, pairs):
        v = vwin_ref[win(j), slab(h)]
        v_lo = jnp.where(lo, v, zero)
        v_hi = jnp.where(lo, zero, v)
        for half in range(2):
            o_ref[rows(j), slab(2 * h + half)] = (
                _mm(p[(2 * half) * c:(2 * half + 1) * c], v_lo)
                + _mm(p[(2 * half + 1) * c:(2 * half + 2) * c], v_hi))

    for ref in (kwin_ref, vwin_ref):
        ref[0:WINDOW, :] = ref[tr:tr + WINDOW, :]


def _swa_call(qlo, qhi, kd, vd, k_hist, v_hist, sinks, hist_valid, ns, tr):
    t = qlo.shape[0] // ns
    dup_w = SWA_KV_HEADS * LANES
    seq = lambda width: pl.BlockSpec((None, tr, width), lambda s, i: (s, i, 0))
    hist = pl.BlockSpec((None, WINDOW, dup_w), lambda s, i: (s, 0, 0))
    return pl.pallas_call(
        functools.partial(_swa_body, hist_valid),
        grid=(ns, t // tr),
        in_specs=[seq(SWA_QW), seq(SWA_QW), seq(dup_w), seq(dup_w), hist, hist,
                  pl.BlockSpec(memory_space=pltpu.SMEM)],
        out_specs=seq(SWA_QW),
        out_shape=jax.ShapeDtypeStruct((ns, t, SWA_QW), F32),
        scratch_shapes=[pltpu.VMEM((WINDOW + tr, dup_w), BF16)] * 2,
        compiler_params=_params(("arbitrary", "arbitrary")),
        name="swa",
    )(qlo.reshape(ns, t, SWA_QW), qhi.reshape(ns, t, SWA_QW), kd.reshape(ns, t, dup_w),
      vd.reshape(ns, t, dup_w), k_hist, v_hist, sinks.reshape(1, SWA_HEADS))


def _tail_step(seqs, gated, x_ref, om_ref, gate_refs, qm_ref, mk_ref, mv_ref, mqg_ref, wo_ref,
               gf_ref, wgu_ref, wd_ref, y_ref, mix_w, mix_r):
    tm = om_ref.shape[0]
    rows = tm // seqs
    d_ff = wd_ref.shape[0]
    half = d_ff // 2
    heads = [(sq, hd) for sq in range(seqs) for hd in range(MEM_HEADS)]
    rs = lambda sq: slice(sq * rows, (sq + 1) * rows)
    hs = lambda hd: slice(hd * MEM_HD, (hd + 1) * MEM_HD)

    x = x_ref[...] + jnp.dot(mix_r[...], wo_ref[...], preferred_element_type=F32)

    mqg = mqg_ref[...] * (MEM_HD ** -0.5)
    scores = [_mm_nt(_rms(qm_ref[rs(sq), hs(hd)], mqg), mk_ref[sq, :, hs(hd)])
              for sq, hd in heads]

    h = _rms(x, gf_ref[...]).astype(BF16)
    g0 = jnp.dot(h, wgu_ref[:, 0:half], preferred_element_type=F32)
    u0 = jnp.dot(h, wgu_ref[:, d_ff:d_ff + half], preferred_element_type=F32)

    probs = []
    for s in scores:
        p = jnp.exp(s - jnp.max(s, axis=-1, keepdims=True))
        probs.append((p / jnp.sum(p, axis=-1, keepdims=True)).astype(BF16))
    o_heads = [_mm(p, mv_ref[sq, :, hs(hd)]).astype(BF16) for p, (sq, hd) in zip(probs, heads)]
    o_mem = jnp.concatenate(
        [jnp.concatenate(o_heads[sq * MEM_HEADS:(sq + 1) * MEM_HEADS], axis=1)
         for sq in range(seqs)], axis=0)

    act0 = (_silu(g0) * u0).astype(BF16)
    g1 = jnp.dot(h, wgu_ref[:, half:d_ff], preferred_element_type=F32)
    u1 = jnp.dot(h, wgu_ref[:, d_ff + half:], preferred_element_type=F32)

    if gated:
        z_ref, og_ref, unperm_ref = gate_refs
        og = og_ref[...]
        parts = [(_rms(om_ref[:, hd * GDN_DV:(hd + 1) * GDN_DV], og)
                  * _silu(z_ref[:, hd * GDN_DV:(hd + 1) * GDN_DV])).astype(BF16)
                 for hd in range(GDN_HEADS)]
        mixed = jnp.concatenate(parts + [o_mem], axis=1)
        mix_w[...] = jnp.dot(unperm_ref[...], mixed, preferred_element_type=F32).astype(BF16)
    else:
        mix_w[...] = jnp.concatenate([om_ref[...].astype(BF16), o_mem], axis=1)

    y = x + jnp.dot(act0, wd_ref[0:half, :], preferred_element_type=F32)
    act1 = (_silu(g1) * u1).astype(BF16)
    y_ref[...] = y + jnp.dot(act1, wd_ref[half:, :], preferred_element_type=F32)


def _tail_body(seqs, gated, x_ref, om_ref, *refs):
    n_gate = 3 if gated else 0
    gate_refs, refs = refs[:n_gate], refs[n_gate:]
    (qm_ref, mk_ref, mv_ref, mqg_ref, wo_ref, gf_ref, wgu_ref, wd_ref, y_ref,
     mix0_ref, mix1_ref) = refs
    i = pl.program_id(0)

    @pl.when(i == 0)
    def _():
        mix1_ref[...] = jnp.zeros_like(mix1_ref)

    def step(mix_w, mix_r):
        _tail_step(seqs, gated, x_ref, om_ref, gate_refs, qm_ref, mk_ref, mv_ref, mqg_ref, wo_ref,
                   gf_ref, wgu_ref, wd_ref, y_ref, mix_w, mix_r)

    @pl.when(i % 2 == 0)
    def _():
        step(mix0_ref, mix1_ref)

    @pl.when(i % 2 == 1)
    def _():
        step(mix1_ref, mix0_ref)


def _tail_call(x, o_mix, gate, qm, mk, mv, mem_q_gain, w_out, g_ffn, w_gate_up, w_down, tm, seqs):
    rows = x.shape[0]
    mix_w = o_mix.shape[1]
    n_tiles = rows // tm
    cur = lambda i: jnp.minimum(i, n_tiles - 1)
    prev = lambda i: jnp.maximum(i - 1, 0)
    tile = lambda width, which: pl.BlockSpec((tm, width), lambda i: (which(i), 0))
    tiles_per_seq = rows // (tm * mk.shape[0]) if seqs == 1 else 1
    mem = pl.BlockSpec((seqs, N_MEM, MEM_W), lambda i: (cur(i) // tiles_per_seq, 0, 0))
    gate_specs = ([tile(mix_w, cur), _const_spec((1, GDN_DV)), _const_spec((tm, tm))]
                  if gate else [])
    gate_args = [gate[0], gate[1].reshape(1, GDN_DV), _chunk_perm(tm).T] if gate else []
    return pl.pallas_call(
        functools.partial(_tail_body, seqs, bool(gate)),
        grid=(n_tiles + 1,),
        in_specs=[tile(D_MODEL, prev), tile(mix_w, cur)] + gate_specs + [
            tile(MEM_W, cur), mem, mem,
            _const_spec((1, MEM_HD)), _const_spec(w_out.shape), _const_spec((1, D_MODEL)),
            _const_spec(w_gate_up.shape), _const_spec(w_down.shape),
        ],
        out_specs=tile(D_MODEL, prev),
        out_shape=jax.ShapeDtypeStruct((rows, D_MODEL), F32),
        scratch_shapes=[pltpu.VMEM((tm, mix_w + MEM_W), BF16)] * 2,
        compiler_params=_params(("arbitrary",)),
        name="tail",
    )(x, o_mix, *gate_args, qm, mk, mv, mem_q_gain.reshape(1, MEM_HD), w_out,
      g_ffn.reshape(1, D_MODEL), w_gate_up, w_down)


def _lane_row(vals, offset):
    return jnp.zeros((1, LANES), F32).at[0, offset:offset + vals.shape[0]].set(vals)


def _dup_heads(a):
    lead = a.shape[:-1]
    a = a.reshape(lead + (SWA_KV_HEADS, 1, SWA_HD))
    return jnp.broadcast_to(a, lead + (SWA_KV_HEADS, 2, SWA_HD)).reshape(lead + (SWA_KV_HEADS * LANES,))


def _undup_heads(a):
    lead = a.shape[:-1]
    return a.reshape(lead + (SWA_KV_HEADS, 2, SWA_HD))[..., 0, :]


def _trunk(x, mem_k, mem_v, gdn_conv, gdn_state, swa_k_hist, swa_v_hist, hist_valid, w):
    ns, t, _ = x.shape
    rows = ns * t
    tm = ROW_TILE
    seqs = max(1, tm // t)
    x2 = x.reshape(rows, D_MODEL)

    conv0 = jnp.pad(gdn_conv[:, :, None, :], ((0, 0), (0, 0), (SUBLANES - 1, 0), (0, 0)))
    conv0 = conv0.reshape(ns, (GDN_CONV - 1) * SUBLANES, GDN_QKV)
    z, qm, *gdn_ops, conv_new = _proj_a_call(x2, w["norm_mix"][0], w, conv0, tm, seqs)
    conv_new = conv_new[:, SUBLANES - 1::SUBLANES, :]
    o_raw, s_new = _gdn_call(gdn_ops, gdn_state, ns, min(t, GDN_TILE_ROWS))
    x2 = _tail_call(x2, o_raw.reshape(rows, GDN_VW), (z, w["o_norm_a"]), qm, mem_k[0], mem_v[0],
                    w["mem_q_norm"][0], w["out_a"], w["norm_ffn"][0], w["gate_up"][0], w["down"][0],
                    tm, seqs)

    qlo, qhi, kd, vd, k_cache, v_cache, qm = _proj_b_call(x2, w["norm_mix"][1], w, tm, ns > 1)
    o_mix = _swa_call(qlo, qhi, kd, vd, _dup_heads(swa_k_hist).astype(BF16),
                      _dup_heads(swa_v_hist).astype(BF16), w["sinks_b"], hist_valid, ns,
                      min(t, SWA_TILE_ROWS))
    x2 = _tail_call(x2, o_mix.reshape(rows, SWA_QW), None, qm, mem_k[1], mem_v[1],
                    w["mem_q_norm"][1], w["out_b"], w["norm_ffn"][1], w["gate_up"][1], w["down"][1],
                    tm, seqs)

    keep = min(WINDOW, t) if ns == 1 else t
    k_new = _undup_heads(k_cache).reshape(ns, -1, SWA_KVW)[:, -keep:]
    v_new = _undup_heads(v_cache).reshape(ns, -1, SWA_KVW)[:, -keep:]
    return x2.reshape(ns, t, D_MODEL), conv_new, s_new, k_new, v_new


def kernel(x_prompt, x_sample, mem_prompt, cache_mem_k, cache_mem_v, state_gdn, state_gdn_conv, cache_swa_k, cache_swa_v, norm_mix, norm_ffn, mem_norm, w_mem_kv, mem_q_norm, mem_k_norm, w_in_a, conv_w_a, a_log, dt_bias, o_norm_a, w_out_a, w_in_b, q_norm_b, k_norm_b, sinks_b, w_out_b, w_gate_up, w_down):
    bsz = x_prompt.shape[0]
    dec = x_sample.shape[0]
    wa = w_in_a[0]
    z_lo = GDN_QKV
    ba_lo = GDN_QKV + GDN_VW
    qm_lo = ba_lo + 2 * GDN_HEADS
    wb = w_in_b[0]
    w = {
        "norm_mix": norm_mix, "norm_ffn": norm_ffn, "mem_q_norm": mem_q_norm,
        "a_qkv": wa[:, :z_lo].astype(BF16), "a_z": wa[:, z_lo:ba_lo].astype(BF16),
        "a_ba": jnp.pad(wa[:, ba_lo:qm_lo], ((0, 0), (0, LANES - 2 * GDN_HEADS))).astype(BF16),
        "a_qm": wa[:, qm_lo:].astype(BF16),
        "b_q": wb[:, :SWA_QW].astype(BF16),
        "b_kv": jnp.concatenate([_dup_heads(wb[:, SWA_QW:SWA_QW + SWA_KVW]),
                                 _dup_heads(wb[:, SWA_QW + SWA_KVW:SWA_QW + 2 * SWA_KVW])],
                                axis=1).astype(BF16),
        "b_qm": wb[:, SWA_QW + 2 * SWA_KVW:].astype(BF16),
        "q_gain2": jnp.tile(q_norm_b[0], 2).reshape(1, LANES),
        "k_gain2": jnp.tile(k_norm_b[0], 2).reshape(1, LANES),
        "conv_w_a": conv_w_a[0], "o_norm_a": o_norm_a[0],
        "alog_row": _lane_row(a_log[0], GDN_HEADS), "dtb_row": _lane_row(dt_bias[0], GDN_HEADS),
        "out_a": w_out_a[0].astype(BF16), "out_b": w_out_b[0].astype(BF16),
        "sinks_b": sinks_b[0],
        "gate_up": w_gate_up.astype(BF16), "down": w_down.astype(BF16),
    }

    mk, mv = _memkv_call(mem_prompt[0], mem_norm, w_mem_kv, mem_k_norm)
    depth = mk.shape[0]
    new_mem_k = mk.reshape(depth, bsz, N_MEM, MEM_HEADS, MEM_HD)
    new_mem_v = mv.reshape(depth, bsz, N_MEM, MEM_HEADS, MEM_HD)

    zero_conv = jnp.zeros((bsz, GDN_CONV - 1, GDN_QKV), F32)
    zero_state = jnp.zeros((bsz, GDN_HEADS, GDN_DK, GDN_DV), F32)
    zero_hist = jnp.zeros((bsz, WINDOW, SWA_KVW), F32)
    y_p, conv_p, state_p, k_p, v_p = _trunk(
        x_prompt, mk.reshape(depth, bsz, N_MEM, MEM_W), mv.reshape(depth, bsz, N_MEM, MEM_W),
        zero_conv, zero_state, zero_hist, zero_hist, False, w)

    y_s, conv_s, state_s, k_s, v_s = _trunk(
        x_sample, cache_mem_k.reshape(depth, dec, N_MEM, MEM_W),
        cache_mem_v.reshape(depth, dec, N_MEM, MEM_W), state_gdn_conv[0], state_gdn[0],
        cache_swa_k[0].reshape(dec, WINDOW, SWA_KVW), cache_swa_v[0].reshape(dec, WINDOW, SWA_KVW),
        True, w)

    kv_shape = lambda a: a.reshape(a.shape[0], a.shape[1], SWA_KV_HEADS, SWA_HD)[None]
    return (y_p, y_s, state_p[None], conv_p[None], state_s[None], conv_s[None],
            kv_shape(k_p), kv_shape(v_p), kv_shape(k_s), kv_shape(v_s), new_mem_k, new_mem_v)
```

```python
import functools

import jax
import jax.numpy as jnp
import numpy as np
from jax import lax
from jax.experimental import pallas as pl
from jax.experimental.pallas import tpu as pltpu

F32 = jnp.float32
BF16 = jnp.bfloat16

D_MODEL = 1024
CHUNK = 64
EPS = 1e-6
GDN_HEADS = 8
GDN_DK = 128
GDN_DV = 128
GDN_CONV = 4
GDN_QKV = GDN_HEADS * (2 * GDN_DK + GDN_DV)
GDN_VW = GDN_HEADS * GDN_DV
SWA_HEADS = 16
SWA_KV_HEADS = 4
SWA_HD = 64
SWA_GROUP = SWA_HEADS // SWA_KV_HEADS
SWA_QW = SWA_HEADS * SWA_HD
SWA_KVW = SWA_KV_HEADS * SWA_HD
WINDOW = 128
N_MEM = 256
MEM_HEADS = 4
MEM_HD = 128
MEM_W = MEM_HEADS * MEM_HD
LANES = 128
SUBLANES = 8
VMEM_LIMIT = 56 * 1024 * 1024
ROW_TILE = 256
SWA_TILE_ROWS = 4 * CHUNK
GDN_TILE_ROWS = 4 * CHUNK

_HI = lax.Precision.HIGHEST


def _mm(a, b):
    return jnp.dot(a.astype(BF16), b.astype(BF16), preferred_element_type=F32)


def _mm_nt(a, b):
    return lax.dot_general(a.astype(BF16), b.astype(BF16), (((1,), (1,)), ((), ())),
                           preferred_element_type=F32)


def _mm_tn(a, b):
    return lax.dot_general(a.astype(BF16), b.astype(BF16), (((0,), (0,)), ((), ())),
                           preferred_element_type=F32)


def _rms(x, g):
    return x * lax.rsqrt(jnp.mean(x * x, axis=-1, keepdims=True) + EPS) * g


def _sigmoid(x):
    return 1.0 / (1.0 + jnp.exp(-x))


def _silu(x):
    return x * _sigmoid(x)


def _softplus(x):
    return jnp.maximum(x, 0.0) + jnp.log1p(jnp.exp(-jnp.abs(x)))


def _const_spec(shape):
    nd = len(shape)
    return pl.BlockSpec(shape, lambda *_: (0,) * nd, pipeline_mode=pl.Buffered(1))


def _params(sem):
    return pltpu.CompilerParams(dimension_semantics=sem, vmem_limit_bytes=VMEM_LIMIT)


def _memkv_body(mem_ref, g_ref, w_ref, kg_ref, mk_ref, mv_ref):
    h = _rms(mem_ref[...], g_ref[...])
    kv = _mm(h, w_ref[...])
    kg = kg_ref[...]
    for hd in range(MEM_HEADS):
        sl = slice(hd * MEM_HD, (hd + 1) * MEM_HD)
        mk_ref[:, sl] = _rms(kv[:, sl], kg)
    mv_ref[...] = kv[:, MEM_W:]


def _memkv_call(mem, mem_norm, w_mem_kv, mem_k_norm):
    depth = w_mem_kv.shape[0]
    out = jax.ShapeDtypeStruct((depth, N_MEM, MEM_W), F32)
    return pl.pallas_call(
        _memkv_body,
        grid=(depth,),
        in_specs=[
            pl.BlockSpec((N_MEM, D_MODEL), lambda i: (0, 0)),
            pl.BlockSpec((None, 1, D_MODEL), lambda i: (i, 0, 0)),
            pl.BlockSpec((None, D_MODEL, 2 * MEM_W), lambda i: (i, 0, 0)),
            pl.BlockSpec((None, 1, MEM_HD), lambda i: (i, 0, 0)),
        ],
        out_specs=[pl.BlockSpec((None, N_MEM, MEM_W), lambda i: (i, 0, 0))] * 2,
        out_shape=[out, out],
        compiler_params=_params(("arbitrary",)),
        name="memkv",
    )(mem, mem_norm.reshape(depth, 1, D_MODEL), w_mem_kv, mem_k_norm.reshape(depth, 1, MEM_HD))


def _chunk_time(pos):
    return lax.shift_right_logical(pos, 3) + SUBLANES * (pos & (SUBLANES - 1))


def _chunk_perm(tm):
    pos = np.arange(tm)
    src = (pos // CHUNK) * CHUNK + (pos % CHUNK) // SUBLANES + SUBLANES * (pos % SUBLANES)
    return jnp.asarray(src[:, None] == pos[None, :], BF16)


def _proj_a_matmuls(x_ref, g_ref, perm_ref, wqkv_ref, wz_ref, wba_ref, wqm_ref,
                    z_ref, qm_ref, raw_ref, ba_ref):
    h = _rms(x_ref[...], g_ref[...]).astype(BF16)
    h = jnp.dot(perm_ref[...], h, preferred_element_type=F32).astype(BF16)
    z_ref[...] = jnp.dot(h, wz_ref[...], preferred_element_type=F32)
    qm_ref[...] = jnp.dot(h, wqm_ref[...], preferred_element_type=F32)
    ba_ref[...] = jnp.dot(h, wba_ref[...], preferred_element_type=F32)
    raw_ref[...] = jnp.dot(h, wqkv_ref[...], preferred_element_type=F32)


def _proj_a_rows(seqs, raw_ref, ba_ref, cw_ref, alog_ref, dtb_ref,
                 q_ref, k_ref, kb_ref, qd_ref, kd_ref, vb_ref, kbe_ref, cum_ref, cumt_ref,
                 convn_ref, hist_ref):
    c = CHUNK
    tm = raw_ref.shape[0]
    rows = tm // seqs
    n_hist = (GDN_CONV - 1) * SUBLANES

    t_row = _chunk_time(lax.broadcasted_iota(jnp.int32, (c, c), 0))
    t_col = _chunk_time(lax.broadcasted_iota(jnp.int32, (c, c), 1))
    tril = (t_row >= t_col).astype(F32)
    eye_l = (lax.broadcasted_iota(jnp.int32, (LANES, LANES), 0)
             == lax.broadcasted_iota(jnp.int32, (LANES, LANES), 1)).astype(F32)
    sub0 = lax.broadcasted_iota(jnp.int32, (SUBLANES, LANES), 0) == 0

    ba = ba_ref[...]
    beta_all = _sigmoid(ba)
    g_all = -jnp.exp(alog_ref[...]) * _softplus(ba + dtb_ref[...])

    for j in range(tm // c):
        rs = slice(j * c, (j + 1) * c)
        s, lr = (j * c) // rows, (j * c) % rows
        cum = jnp.dot(tril, g_all[rs], precision=_HI, preferred_element_type=F32)
        cum_ref[rs, :] = cum
        cumt_ref[j] = lax.dot_general(eye_l, cum, (((1,), (1,)), ((), ())), precision=_HI,
                                      preferred_element_type=F32)
        e_cum = jnp.exp(cum)
        e_rest = jnp.exp(cum[c - 1:c, :] - cum)
        beta = beta_all[rs]

        def conv_act(lo):
            sl = slice(lo, lo + LANES)
            x = raw_ref[rs, sl]
            prev = (hist_ref[s, :, sl] if lr == 0
                    else raw_ref[j * c - n_hist:j * c, sl])
            shifted = [jnp.where(sub0,
                                 pltpu.roll(prev[i * SUBLANES:(i + 1) * SUBLANES], 1, 0),
                                 pltpu.roll(x[c - n_hist + i * SUBLANES:c - n_hist + (i + 1) * SUBLANES],
                                            1, 0)) for i in range(GDN_CONV - 1)]
            acc = x * cw_ref[GDN_CONV - 1:GDN_CONV, sl]
            for d in range(1, GDN_CONV):
                xd = jnp.concatenate(shifted[GDN_CONV - 1 - d:] + [x[0:c - d * SUBLANES]], axis=0)
                acc = acc + xd * cw_ref[GDN_CONV - 1 - d:GDN_CONV - d, sl]
            return _silu(acc)

        for hd in range(GDN_HEADS):
            hs = slice(hd * GDN_DK, (hd + 1) * GDN_DK)
            b_col = beta[:, hd:hd + 1]
            ec_col = e_cum[:, GDN_HEADS + hd:GDN_HEADS + hd + 1]
            er_col = e_rest[:, GDN_HEADS + hd:GDN_HEADS + hd + 1]
            q = conv_act(hd * GDN_DK)
            q = q * lax.rsqrt(jnp.sum(q * q, axis=-1, keepdims=True) + EPS) * (GDN_DK ** -0.5)
            k = conv_act(GDN_HEADS * GDN_DK + hd * GDN_DK)
            k = k * lax.rsqrt(jnp.sum(k * k, axis=-1, keepdims=True) + EPS)
            v = conv_act(2 * GDN_HEADS * GDN_DK + hd * GDN_DV)
            kb = k * b_col
            q_ref[rs, hs] = q.astype(BF16)
            qd_ref[rs, hs] = (q * ec_col).astype(BF16)
            k_ref[rs, hs] = k.astype(BF16)
            kb_ref[rs, hs] = kb.astype(BF16)
            kd_ref[rs, hs] = (k * er_col).astype(BF16)
            kbe_ref[rs, hs] = kb * ec_col
            vb_ref[rs, hs] = v * b_col

    for s in range(seqs):
        tail = raw_ref[(s + 1) * rows - n_hist:(s + 1) * rows, :]
        hist_ref[s] = tail
        convn_ref[s] = tail


def _proj_a_body(seqs, tiles_per_seq, x_ref, g_ref, perm_ref, wqkv_ref, wz_ref, wba_ref, wqm_ref,
                 conv0_ref, cw_ref, alog_ref, dtb_ref,
                 z_ref, qm_ref, q_ref, k_ref, kb_ref, qd_ref, kd_ref, vb_ref, kbe_ref, cum_ref,
                 cumt_ref, convn_ref, raw0_ref, raw1_ref, ba0_ref, ba1_ref, hist_ref):
    i = pl.program_id(0)

    @pl.when(i == 0)
    def _():
        raw1_ref[...] = jnp.zeros_like(raw1_ref)
        ba1_ref[...] = jnp.zeros_like(ba1_ref)
        hist_ref[...] = jnp.zeros_like(hist_ref)

    @pl.when((i + tiles_per_seq - 1) % tiles_per_seq == 0)
    def _():
        hist_ref[...] = conv0_ref[...]

    def step(raw_w, ba_w, raw_r, ba_r):
        _proj_a_matmuls(x_ref, g_ref, perm_ref, wqkv_ref, wz_ref, wba_ref, wqm_ref,
                        z_ref, qm_ref, raw_w, ba_w)
        _proj_a_rows(seqs, raw_r, ba_r, cw_ref, alog_ref, dtb_ref,
                     q_ref, k_ref, kb_ref, qd_ref, kd_ref, vb_ref, kbe_ref, cum_ref, cumt_ref,
                     convn_ref, hist_ref)

    @pl.when(i % 2 == 0)
    def _():
        step(raw0_ref, ba0_ref, raw1_ref, ba1_ref)

    @pl.when(i % 2 == 1)
    def _():
        step(raw1_ref, ba1_ref, raw0_ref, ba0_ref)


def _proj_a_call(x, g, w, conv0, tm, seqs):
    rows = x.shape[0]
    ns = conv0.shape[0]
    n_tiles = rows // tm
    n_hist = (GDN_CONV - 1) * SUBLANES
    tiles_per_seq = rows // (ns * tm) if seqs == 1 else 1
    seq_of = (lambda i: i // tiles_per_seq) if seqs == 1 else (lambda i: i)
    cur = lambda i: jnp.minimum(i, n_tiles - 1)
    prev = lambda i: jnp.maximum(i - 1, 0)
    tile = lambda width, which: pl.BlockSpec((tm, width), lambda i: (which(i), 0))
    hist = pl.BlockSpec((seqs, n_hist, GDN_QKV), lambda i: (seq_of(prev(i)), 0, 0))
    wide_bf = jax.ShapeDtypeStruct((rows, GDN_VW), BF16)
    wide_f32 = jax.ShapeDtypeStruct((rows, GDN_VW), F32)
    perm = _chunk_perm(tm)
    return pl.pallas_call(
        functools.partial(_proj_a_body, seqs, tiles_per_seq),
        grid=(n_tiles + 1,),
        in_specs=[tile(D_MODEL, cur), _const_spec((1, D_MODEL)), _const_spec(perm.shape)]
        + [_const_spec(w[name].shape) for name in ("a_qkv", "a_z", "a_ba", "a_qm")]
        + [hist, _const_spec((GDN_CONV, GDN_QKV)), _const_spec((1, LANES)), _const_spec((1, LANES))],
        out_specs=[tile(GDN_VW, cur), tile(MEM_W, cur)] + [tile(GDN_VW, prev)] * 7
        + [tile(LANES, prev),
           pl.BlockSpec((tm // CHUNK, LANES, CHUNK), lambda i: (prev(i), 0, 0)), hist],
        out_shape=[wide_f32, jax.ShapeDtypeStruct((rows, MEM_W), F32)]
        + [wide_bf] * 5 + [wide_f32] * 2
        + [jax.ShapeDtypeStruct((rows, LANES), F32),
           jax.ShapeDtypeStruct((rows // CHUNK, LANES, CHUNK), F32),
           jax.ShapeDtypeStruct((ns, n_hist, GDN_QKV), F32)],
        scratch_shapes=[pltpu.VMEM((tm, GDN_QKV), F32)] * 2 + [pltpu.VMEM((tm, LANES), F32)] * 2
        + [pltpu.VMEM((seqs, n_hist, GDN_QKV), F32)],
        compiler_params=_params(("arbitrary",)),
        name="proj_a",
    )(x, g.reshape(1, D_MODEL), perm, w["a_qkv"], w["a_z"], w["a_ba"], w["a_qm"], conv0,
      w["conv_w_a"], w["alog_row"], w["dtb_row"])


def _gdn_body(q_ref, k_ref, kb_ref, qd_ref, kd_ref, vb_ref, kbe_ref, cum_ref, cumt_ref, s0_ref,
              o_ref, sn_ref, s_ref):
    c = CHUNK
    nc = q_ref.shape[0] // c

    @pl.when(pl.program_id(1) == 0)
    def _():
        s_ref[...] = s0_ref[...]

    row = _chunk_time(lax.broadcasted_iota(jnp.int32, (c, c), 0))
    col = _chunk_time(lax.broadcasted_iota(jnp.int32, (c, c), 1))
    causal = row >= col
    strict = row > col

    pairs = [(j, hd) for j in range(nc) for hd in range(GDN_HEADS)]
    rs = lambda j: slice(j * c, (j + 1) * c)
    hs = lambda hd: slice(hd * GDN_DK, (hd + 1) * GDN_DK)
    gl = lambda hd: slice(GDN_HEADS + hd, GDN_HEADS + hd + 1)

    decays = [jnp.exp(jnp.where(causal, cum_ref[rs(j), gl(hd)] - cumt_ref[j, gl(hd), :], -jnp.inf))
              for j, hd in pairs]
    grams = [_mm_nt(jnp.concatenate([kb_ref[rs(j), hs(hd)], q_ref[rs(j), hs(hd)]], axis=0),
                    k_ref[rs(j), hs(hd)]) for j, hd in pairs]
    ps = [jnp.where(strict, -(g[:c] * d), 0.0) for g, d in zip(grams, decays)]
    qks = [g[c:] * d for g, d in zip(grams, decays)]

    ns = ps
    for _ in range(5):
        ps = [_mm(p, p) for p in ps]
        ns = [n + p + _mm(n, p) for n, p in zip(ns, ps)]

    rhss = [jnp.concatenate([vb_ref[rs(j), hs(hd)], kbe_ref[rs(j), hs(hd)]], axis=1)
            for j, hd in pairs]
    sols = [rhs + _mm(n, rhs) for n, rhs in zip(ns, rhss)]

    ss = [s_ref[hd] for hd in range(GDN_HEADS)]
    for j in range(nc):
        sol_j = sols[j * GDN_HEADS:(j + 1) * GDN_HEADS]
        qk_j = qks[j * GDN_HEADS:(j + 1) * GDN_HEADS]
        r1s = [_mm(jnp.concatenate([sol_j[hd][:, GDN_DV:].astype(BF16), qd_ref[rs(j), hs(hd)]],
                                   axis=0), ss[hd]) for hd in range(GDN_HEADS)]
        us = [sol_j[hd][:, :GDN_DV] - r1s[hd][:c] for hd in range(GDN_HEADS)]
        for hd in range(GDN_HEADS):
            o_ref[rs(j), hs(hd)] = r1s[hd][c:] + _mm(qk_j[hd], us[hd])
        ss = [ss[hd] * jnp.exp(cum_ref[j * c + c - 1:j * c + c, gl(hd)])
              + _mm_tn(kd_ref[rs(j), hs(hd)], us[hd]) for hd in range(GDN_HEADS)]
    for hd in range(GDN_HEADS):
        s_ref[hd] = ss[hd]
        sn_ref[hd] = ss[hd]


def _gdn_call(ops, s0, ns, tr):
    t = ops[0].shape[0] // ns
    seq = lambda width: pl.BlockSpec((None, tr, width), lambda s, i: (s, i, 0))
    state = pl.BlockSpec((None, GDN_HEADS, GDN_DK, GDN_DV), lambda s, i: (s, 0, 0, 0))
    args = [a.reshape(ns, t, a.shape[-1]) for a in ops[:8]]
    args.append(ops[8].reshape(ns, t // CHUNK, LANES, CHUNK))
    return pl.pallas_call(
        _gdn_body,
        grid=(ns, t // tr),
        in_specs=[seq(GDN_VW)] * 7
        + [seq(LANES), pl.BlockSpec((None, tr // CHUNK, LANES, CHUNK), lambda s, i: (s, i, 0, 0)),
           state],
        out_specs=[seq(GDN_VW), state],
        out_shape=[jax.ShapeDtypeStruct((ns, t, GDN_VW), F32),
                   jax.ShapeDtypeStruct((ns, GDN_HEADS, GDN_DK, GDN_DV), F32)],
        scratch_shapes=[pltpu.VMEM((GDN_HEADS, GDN_DK, GDN_DV), F32)],
        compiler_params=_params(("arbitrary", "arbitrary")),
        name="gdn",
    )(*args, s0)


def _proj_b_body(x_ref, g_ref, wq_ref, wkv_ref, wqm_ref, qg_ref, kg_ref,
                 qlo_ref, qhi_ref, kd_ref, vd_ref, kc_ref, vc_ref, qm_ref):
    h = _rms(x_ref[...], g_ref[...]).astype(BF16)
    qm_ref[...] = jnp.dot(h, wqm_ref[...], preferred_element_type=F32)
    q = jnp.dot(h, wq_ref[...], preferred_element_type=F32)
    kv = jnp.dot(h, wkv_ref[...], preferred_element_type=F32)
    lo = lax.broadcasted_iota(jnp.int32, (1, LANES), 1) < SWA_HD
    qg = qg_ref[...] * (SWA_HD ** -0.5)
    for p in range(SWA_QW // LANES):
        sl = slice(p * LANES, (p + 1) * LANES)
        x = q[:, sl]
        x2 = x * x
        m_lo = jnp.sum(jnp.where(lo, x2, 0.0), axis=-1, keepdims=True) * (1.0 / SWA_HD)
        m_hi = jnp.sum(jnp.where(lo, 0.0, x2), axis=-1, keepdims=True) * (1.0 / SWA_HD)
        qn = x * jnp.where(lo, lax.rsqrt(m_lo + EPS), lax.rsqrt(m_hi + EPS)) * qg
        qlo_ref[:, sl] = jnp.where(lo, qn, 0.0).astype(BF16)
        qhi_ref[:, sl] = jnp.where(lo, 0.0, qn).astype(BF16)
    kg = kg_ref[...]
    for p in range(SWA_KV_HEADS):
        sl = slice(p * LANES, (p + 1) * LANES)
        kn = _rms(kv[:, sl], kg)
        kc_ref[:, sl] = kn
        kd_ref[:, sl] = kn.astype(BF16)
    v = kv[:, SWA_KV_HEADS * LANES:]
    vc_ref[...] = v
    vd_ref[...] = v.astype(BF16)


def _proj_b_call(x, g, w, tm, keep_all):
    rows = x.shape[0]
    dup_w = SWA_KV_HEADS * LANES
    tile = lambda width: pl.BlockSpec((tm, width), lambda i: (i, 0))
    cache = tile(dup_w) if keep_all else pl.BlockSpec((tm, dup_w), lambda i: (0, 0))
    cache_shape = jax.ShapeDtypeStruct((rows if keep_all else tm, dup_w), F32)
    return pl.pallas_call(
        _proj_b_body,
        grid=(rows // tm,),
        in_specs=[tile(D_MODEL), _const_spec((1, D_MODEL))]
        + [_const_spec(w[name].shape) for name in ("b_q", "b_kv", "b_qm")]
        + [_const_spec((1, LANES))] * 2,
        out_specs=[tile(SWA_QW)] * 2 + [tile(dup_w)] * 2 + [cache] * 2 + [tile(MEM_W)],
        out_shape=[jax.ShapeDtypeStruct((rows, SWA_QW), BF16)] * 2
        + [jax.ShapeDtypeStruct((rows, dup_w), BF16)] * 2 + [cache_shape] * 2
        + [jax.ShapeDtypeStruct((rows, MEM_W), F32)],
        compiler_params=_params(("arbitrary",)),
        name="proj_b",
    )(x, g.reshape(1, D_MODEL), w["b_q"], w["b_kv"], w["b_qm"], w["q_gain2"], w["k_gain2"])


class _SwaJob:
    def __init__(self, hist_valid, seqs, tile_in_seq, refs, kwin_ref, vwin_ref):
        (self.qlo_ref, self.qhi_ref, self.k_ref, self.v_ref, self.kh_ref, self.vh_ref,
         self.sink_ref) = refs
        self.hist_valid, self.seqs, self.tile_in_seq = hist_valid, seqs, tile_in_seq
        self.kwin_ref, self.vwin_ref = kwin_ref, vwin_ref
        self.tm = self.qlo_ref.shape[0]
        self.rows = self.tm // seqs
        self.units = [(sq, j, h) for sq in range(seqs) for j in range(self.rows // CHUNK)
                      for h in range(SWA_KV_HEADS)]

    def load_history(self):
        self.kwin_ref[:, 0:WINDOW, :] = self.kh_ref[...]
        self.vwin_ref[:, 0:WINDOW, :] = self.vh_ref[...]

    def scores(self):
        c, rows = CHUNK, self.rows
        slab = lambda p: slice(p * LANES, (p + 1) * LANES)
        for sq in range(self.seqs):
            self.kwin_ref[sq, WINDOW:WINDOW + rows, :] = self.k_ref[sq * rows:(sq + 1) * rows, :]
            self.vwin_ref[sq, WINDOW:WINDOW + rows, :] = self.v_ref[sq * rows:(sq + 1) * rows, :]
        out = []
        for sq, j, h in self.units:
            r = slice(sq * rows + j * c, sq * rows + (j + 1) * c)
            qs = jnp.concatenate([self.qlo_ref[r, slab(2 * h)], self.qhi_ref[r, slab(2 * h)],
                                  self.qlo_ref[r, slab(2 * h + 1)], self.qhi_ref[r, slab(2 * h + 1)]],
                                 axis=0)
            s = _mm_nt(qs, self.kwin_ref[sq, j * c:j * c + WINDOW + c, slab(h)])
            if not self.hist_valid and j * c < WINDOW:
                key_col = lax.broadcasted_iota(jnp.int32, (SWA_GROUP * c, WINDOW + c), 1)
                first_key = self.tile_in_seq * rows + j * c - WINDOW
                s = jnp.where(key_col + first_key >= 0, s, -jnp.inf)
            out.append(s)
        return out

    def softmax(self, scores):
        c = CHUNK
        sinks = [jnp.concatenate([jnp.full((c, 1), self.sink_ref[0, h * SWA_GROUP + g], F32)
                                  for g in range(SWA_GROUP)], axis=0)
                 for h in range(SWA_KV_HEADS)]
        out = []
        for s, (sq, j, h) in zip(scores, self.units):
            m = jnp.maximum(jnp.max(s, axis=-1, keepdims=True), sinks[h])
            p = jnp.exp(s - m)
            out.append((p / (jnp.sum(p, axis=-1, keepdims=True) + jnp.exp(sinks[h] - m)))
                       .astype(BF16))
        return out

    def values(self, probs):
        c, rows = CHUNK, self.rows
        slab = lambda p: slice(p * LANES, (p + 1) * LANES)
        lo = lax.broadcasted_iota(jnp.int32, (1, LANES), 1) < SWA_HD
        zero = jnp.zeros((), BF16)
        slabs = {}
        for p, (sq, j, h) in zip(probs, self.units):
            v = self.vwin_ref[sq, j * c:j * c + WINDOW + c, slab(h)]
            v_lo = jnp.where(lo, v, zero)
            v_hi = jnp.where(lo, zero, v)
            for half in range(2):
                slabs[sq, j, 2 * h + half] = (
                    _mm(p[(2 * half) * c:(2 * half + 1) * c], v_lo)
                    + _mm(p[(2 * half + 1) * c:(2 * half + 2) * c], v_hi)).astype(BF16)
        for ref in (self.kwin_ref, self.vwin_ref):
            ref[:, 0:WINDOW, :] = ref[:, rows:rows + WINDOW, :]
        return jnp.concatenate(
            [jnp.concatenate([slabs[sq, j, p] for p in range(SWA_QW // LANES)], axis=1)
             for sq in range(self.seqs) for j in range(rows // c)], axis=0)


def _tail_step(seqs, gated, swa, x_ref, om_ref, gate_refs, qm_ref, mk_ref, mv_ref, mqg_ref, wo_ref,
               gf_ref, wgu_ref, wd_ref, y_ref, mix_w, mix_r):
    tm = x_ref.shape[0]
    rows = tm // seqs
    d_ff = wd_ref.shape[0]
    half = d_ff // 2
    heads = [(sq, hd) for sq in range(seqs) for hd in range(MEM_HEADS)]
    rs = lambda sq: slice(sq * rows, (sq + 1) * rows)
    hs = lambda hd: slice(hd * MEM_HD, (hd + 1) * MEM_HD)

    x = x_ref[...] + jnp.dot(mix_r[...], wo_ref[...], preferred_element_type=F32)

    if swa:
        swa_scores = swa.scores()
    mqg = mqg_ref[...] * (MEM_HD ** -0.5)
    scores = [_mm_nt(_rms(qm_ref[rs(sq), hs(hd)], mqg), mk_ref[sq, :, hs(hd)])
              for sq, hd in heads]

    h = _rms(x, gf_ref[...]).astype(BF16)
    g0 = jnp.dot(h, wgu_ref[:, 0:half], preferred_element_type=F32)
    u0 = jnp.dot(h, wgu_ref[:, d_ff:d_ff + half], preferred_element_type=F32)

    probs = []
    for s in scores:
        p = jnp.exp(s - jnp.max(s, axis=-1, keepdims=True))
        probs.append((p / jnp.sum(p, axis=-1, keepdims=True)).astype(BF16))
    o_heads = [_mm(p, mv_ref[sq, :, hs(hd)]).astype(BF16) for p, (sq, hd) in zip(probs, heads)]
    o_mem = jnp.concatenate(
        [jnp.concatenate(o_heads[sq * MEM_HEADS:(sq + 1) * MEM_HEADS], axis=1)
         for sq in range(seqs)], axis=0)

    act0 = (_silu(g0) * u0).astype(BF16)
    g1 = jnp.dot(h, wgu_ref[:, half:d_ff], preferred_element_type=F32)
    u1 = jnp.dot(h, wgu_ref[:, d_ff + half:], preferred_element_type=F32)

    if gated:
        z_ref, og_ref, unperm_ref = gate_refs
        og = og_ref[...]
        parts = [(_rms(om_ref[:, hd * GDN_DV:(hd + 1) * GDN_DV], og)
                  * _silu(z_ref[:, hd * GDN_DV:(hd + 1) * GDN_DV])).astype(BF16)
                 for hd in range(GDN_HEADS)]
        mixed = jnp.concatenate(parts + [o_mem], axis=1)
        mix_w[...] = jnp.dot(unperm_ref[...], mixed, preferred_element_type=F32).astype(BF16)
    else:
        mix_w[...] = jnp.concatenate([swa.values(swa.softmax(swa_scores)), o_mem], axis=1)

    y = x + jnp.dot(act0, wd_ref[0:half, :], preferred_element_type=F32)
    act1 = (_silu(g1) * u1).astype(BF16)
    y_ref[...] = y + jnp.dot(act1, wd_ref[half:, :], preferred_element_type=F32)


def _tail_body(seqs, tiles_per_seq, hist_valid, x_ref, *refs):
    gated = hist_valid is None
    n_mixer = 4 if gated else 7
    mixer_refs, refs = refs[:n_mixer], refs[n_mixer:]
    qm_ref, mk_ref, mv_ref, mqg_ref, wo_ref, gf_ref, wgu_ref, wd_ref, y_ref = refs[:9]
    mix0_ref, mix1_ref = refs[9:11]
    i = pl.program_id(0)

    @pl.when(i == 0)
    def _():
        mix1_ref[...] = jnp.zeros_like(mix1_ref)

    if gated:
        om_ref, gate_refs, swa = mixer_refs[0], mixer_refs[1:], None
    else:
        om_ref, gate_refs = None, ()
        swa = _SwaJob(hist_valid, seqs, i % tiles_per_seq, mixer_refs, *refs[11:13])

        @pl.when(i % tiles_per_seq == 0)
        def _():
            swa.load_history()

    def step(mix_w, mix_r):
        _tail_step(seqs, gated, swa, x_ref, om_ref, gate_refs, qm_ref, mk_ref, mv_ref, mqg_ref,
                   wo_ref, gf_ref, wgu_ref, wd_ref, y_ref, mix_w, mix_r)

    @pl.when(i % 2 == 0)
    def _():
        step(mix0_ref, mix1_ref)

    @pl.when(i % 2 == 1)
    def _():
        step(mix1_ref, mix0_ref)


def _tail_call(x, gdn, swa, qm, mk, mv, mem_q_gain, w_out, g_ffn, w_gate_up, w_down, tm, seqs):
    rows = x.shape[0]
    ns = mk.shape[0]
    n_tiles = rows // tm
    cur = lambda i: jnp.minimum(i, n_tiles - 1)
    prev = lambda i: jnp.maximum(i - 1, 0)
    tile = lambda width, which: pl.BlockSpec((tm, width), lambda i: (which(i), 0))
    tiles_per_seq = rows // (tm * ns) if seqs == 1 else 1
    per_seq = lambda *dims: pl.BlockSpec((seqs,) + dims,
                                         lambda i: (cur(i) // tiles_per_seq,) + (0,) * len(dims))
    if gdn:
        o_raw, z, o_gain = gdn
        hist_valid = None
        mixer_specs = [tile(GDN_VW, cur), tile(GDN_VW, cur), _const_spec((1, GDN_DV)),
                       _const_spec((tm, tm))]
        mixer_args = [o_raw, z, o_gain.reshape(1, GDN_DV), _chunk_perm(tm).T]
        scratch = []
    else:
        *mixer_args, sinks, hist_valid = swa
        dup_w = SWA_KV_HEADS * LANES
        mixer_specs = [tile(SWA_QW, cur), tile(SWA_QW, cur), tile(dup_w, cur), tile(dup_w, cur),
                       per_seq(WINDOW, dup_w), per_seq(WINDOW, dup_w),
                       pl.BlockSpec(memory_space=pltpu.SMEM)]
        mixer_args.append(sinks.reshape(1, SWA_HEADS))
        scratch = [pltpu.VMEM((seqs, WINDOW + tm // seqs, dup_w), BF16)] * 2
    return pl.pallas_call(
        functools.partial(_tail_body, seqs, tiles_per_seq, hist_valid),
        grid=(n_tiles + 1,),
        in_specs=[tile(D_MODEL, prev)] + mixer_specs + [
            tile(MEM_W, cur), per_seq(N_MEM, MEM_W), per_seq(N_MEM, MEM_W),
            _const_spec((1, MEM_HD)), _const_spec(w_out.shape), _const_spec((1, D_MODEL)),
            _const_spec(w_gate_up.shape), _const_spec(w_down.shape),
        ],
        out_specs=tile(D_MODEL, prev),
        out_shape=jax.ShapeDtypeStruct((rows, D_MODEL), F32),
        scratch_shapes=[pltpu.VMEM((tm, w_out.shape[0]), BF16)] * 2 + scratch,
        compiler_params=_params(("arbitrary",)),
        name="tail",
    )(x, *mixer_args, qm, mk, mv, mem_q_gain.reshape(1, MEM_HD), w_out,
      g_ffn.reshape(1, D_MODEL), w_gate_up, w_down)


def _lane_row(vals, offset):
    return jnp.zeros((1, LANES), F32).at[0, offset:offset + vals.shape[0]].set(vals)


def _dup_heads(a):
    lead = a.shape[:-1]
    a = a.reshape(lead + (SWA_KV_HEADS, 1, SWA_HD))
    return jnp.broadcast_to(a, lead + (SWA_KV_HEADS, 2, SWA_HD)).reshape(lead + (SWA_KV_HEADS * LANES,))


def _undup_heads(a):
    lead = a.shape[:-1]
    return a.reshape(lead + (SWA_KV_HEADS, 2, SWA_HD))[..., 0, :]


def _trunk(x, mem_k, mem_v, gdn_conv, gdn_state, swa_k_hist, swa_v_hist, hist_valid, w):
    ns, t, _ = x.shape
    rows = ns * t
    tm = ROW_TILE
    seqs = max(1, tm // t)
    x2 = x.reshape(rows, D_MODEL)

    conv0 = jnp.pad(gdn_conv[:, :, None, :], ((0, 0), (0, 0), (SUBLANES - 1, 0), (0, 0)))
    conv0 = conv0.reshape(ns, (GDN_CONV - 1) * SUBLANES, GDN_QKV)
    z, qm, *gdn_ops, conv_new = _proj_a_call(x2, w["norm_mix"][0], w, conv0, tm, seqs)
    conv_new = conv_new[:, SUBLANES - 1::SUBLANES, :]
    o_raw, s_new = _gdn_call(gdn_ops, gdn_state, ns, min(t, GDN_TILE_ROWS))
    x2 = _tail_call(x2, (o_raw.reshape(rows, GDN_VW), z, w["o_norm_a"]), None, qm, mem_k[0],
                    mem_v[0], w["mem_q_norm"][0], w["out_a"], w["norm_ffn"][0], w["gate_up"][0],
                    w["down"][0], tm, seqs)

    qlo, qhi, kd, vd, k_cache, v_cache, qm = _proj_b_call(x2, w["norm_mix"][1], w, tm, ns > 1)
    swa = (qlo, qhi, kd, vd, _dup_heads(swa_k_hist).astype(BF16),
           _dup_heads(swa_v_hist).astype(BF16), w["sinks_b"], hist_valid)
    x2 = _tail_call(x2, None, swa, qm, mem_k[1], mem_v[1], w["mem_q_norm"][1], w["out_b"],
                    w["norm_ffn"][1], w["gate_up"][1], w["down"][1], tm, seqs)

    keep = min(WINDOW, t) if ns == 1 else t
    k_new = _undup_heads(k_cache).reshape(ns, -1, SWA_KVW)[:, -keep:]
    v_new = _undup_heads(v_cache).reshape(ns, -1, SWA_KVW)[:, -keep:]
    return x2.reshape(ns, t, D_MODEL), conv_new, s_new, k_new, v_new


def kernel(x_prompt, x_sample, mem_prompt, cache_mem_k, cache_mem_v, state_gdn, state_gdn_conv, cache_swa_k, cache_swa_v, norm_mix, norm_ffn, mem_norm, w_mem_kv, mem_q_norm, mem_k_norm, w_in_a, conv_w_a, a_log, dt_bias, o_norm_a, w_out_a, w_in_b, q_norm_b, k_norm_b, sinks_b, w_out_b, w_gate_up, w_down):
    bsz = x_prompt.shape[0]
    dec = x_sample.shape[0]
    wa = w_in_a[0]
    z_lo = GDN_QKV
    ba_lo = GDN_QKV + GDN_VW
    qm_lo = ba_lo + 2 * GDN_HEADS
    wb = w_in_b[0]
    w = {
        "norm_mix": norm_mix, "norm_ffn": norm_ffn, "mem_q_norm": mem_q_norm,
        "a_qkv": wa[:, :z_lo].astype(BF16), "a_z": wa[:, z_lo:ba_lo].astype(BF16),
        "a_ba": jnp.pad(wa[:, ba_lo:qm_lo], ((0, 0), (0, LANES - 2 * GDN_HEADS))).astype(BF16),
        "a_qm": wa[:, qm_lo:].astype(BF16),
        "b_q": wb[:, :SWA_QW].astype(BF16),
        "b_kv": jnp.concatenate([_dup_heads(wb[:, SWA_QW:SWA_QW + SWA_KVW]),
                                 _dup_heads(wb[:, SWA_QW + SWA_KVW:SWA_QW + 2 * SWA_KVW])],
                                axis=1).astype(BF16),
        "b_qm": wb[:, SWA_QW + 2 * SWA_KVW:].astype(BF16),
        "q_gain2": jnp.tile(q_norm_b[0], 2).reshape(1, LANES),
        "k_gain2": jnp.tile(k_norm_b[0], 2).reshape(1, LANES),
        "conv_w_a": conv_w_a[0], "o_norm_a": o_norm_a[0],
        "alog_row": _lane_row(a_log[0], GDN_HEADS), "dtb_row": _lane_row(dt_bias[0], GDN_HEADS),
        "out_a": w_out_a[0].astype(BF16), "out_b": w_out_b[0].astype(BF16),
        "sinks_b": sinks_b[0],
        "gate_up": w_gate_up.astype(BF16), "down": w_down.astype(BF16),
    }

    mk, mv = _memkv_call(mem_prompt[0], mem_norm, w_mem_kv, mem_k_norm)
    depth = mk.shape[0]
    new_mem_k = mk.reshape(depth, bsz, N_MEM, MEM_HEADS, MEM_HD)
    new_mem_v = mv.reshape(depth, bsz, N_MEM, MEM_HEADS, MEM_HD)

    zero_conv = jnp.zeros((bsz, GDN_CONV - 1, GDN_QKV), F32)
    zero_state = jnp.zeros((bsz, GDN_HEADS, GDN_DK, GDN_DV), F32)
    zero_hist = jnp.zeros((bsz, WINDOW, SWA_KVW), F32)
    y_p, conv_p, state_p, k_p, v_p = _trunk(
        x_prompt, mk.reshape(depth, bsz, N_MEM, MEM_W), mv.reshape(depth, bsz, N_MEM, MEM_W),
        zero_conv, zero_state, zero_hist, zero_hist, False, w)

    y_s, conv_s, state_s, k_s, v_s = _trunk(
        x_sample, cache_mem_k.reshape(depth, dec, N_MEM, MEM_W),
        cache_mem_v.reshape(depth, dec, N_MEM, MEM_W), state_gdn_conv[0], state_gdn[0],
        cache_swa_k[0].reshape(dec, WINDOW, SWA_KVW), cache_swa_v[0].reshape(dec, WINDOW, SWA_KVW),
        True, w)

    kv_shape = lambda a: a.reshape(a.shape[0], a.shape[1], SWA_KV_HEADS, SWA_HD)[None]
    return (y_p, y_s, state_p[None], conv_p[None], state_s[None], conv_s[None],
            kv_shape(k_p), kv_shape(v_p), kv_shape(k_s), kv_shape(v_s), new_mem_k, new_mem_v)
```

```python
import functools

import jax
import jax.numpy as jnp
import numpy as np
from jax import lax
from jax.experimental import pallas as pl
from jax.experimental.pallas import tpu as pltpu

F32 = jnp.float32
BF16 = jnp.bfloat16

D_MODEL = 1024
CHUNK = 64
EPS = 1e-6
GDN_HEADS = 8
GDN_DK = 128
GDN_DV = 128
GDN_CONV = 4
GDN_QKV = GDN_HEADS * (2 * GDN_DK + GDN_DV)
GDN_VW = GDN_HEADS * GDN_DV
SWA_HEADS = 16
SWA_KV_HEADS = 4
SWA_HD = 64
SWA_GROUP = SWA_HEADS // SWA_KV_HEADS
SWA_QW = SWA_HEADS * SWA_HD
SWA_KVW = SWA_KV_HEADS * SWA_HD
WINDOW = 128
N_MEM = 256
MEM_HEADS = 4
MEM_HD = 128
MEM_W = MEM_HEADS * MEM_HD
LANES = 128
SUBLANES = 8
VMEM_LIMIT = 56 * 1024 * 1024
ROW_TILE = 256
SWA_TILE_ROWS = 4 * CHUNK
GDN_TILE_ROWS = 8 * CHUNK

_HI = lax.Precision.HIGHEST


def _mm(a, b):
    return jnp.dot(a.astype(BF16), b.astype(BF16), preferred_element_type=F32)


def _mm_nt(a, b):
    return lax.dot_general(a.astype(BF16), b.astype(BF16), (((1,), (1,)), ((), ())),
                           preferred_element_type=F32)


def _mm_tn(a, b):
    return lax.dot_general(a.astype(BF16), b.astype(BF16), (((0,), (0,)), ((), ())),
                           preferred_element_type=F32)


def _rms(x, g):
    return x * lax.rsqrt(jnp.mean(x * x, axis=-1, keepdims=True) + EPS) * g


def _sigmoid(x):
    return 1.0 / (1.0 + jnp.exp(-x))


def _silu(x):
    return x * _sigmoid(x)


def _softplus(x):
    return jnp.maximum(x, 0.0) + jnp.log1p(jnp.exp(-jnp.abs(x)))


def _const_spec(shape):
    nd = len(shape)
    return pl.BlockSpec(shape, lambda *_: (0,) * nd, pipeline_mode=pl.Buffered(1))


def _slab_spec(arr, layer, col_block, width):
    return pl.BlockSpec((None, arr.shape[1], width), lambda *_: (layer, 0, col_block),
                        pipeline_mode=pl.Buffered(1))


def _params(sem):
    return pltpu.CompilerParams(dimension_semantics=sem, vmem_limit_bytes=VMEM_LIMIT)


def _memkv_body(mem_ref, g_ref, w_ref, kg_ref, mk_ref, mv_ref, mkb_ref, mvb_ref):
    h = _rms(mem_ref[...], g_ref[...])
    kv = _mm(h, w_ref[...])
    kg = kg_ref[...]
    for hd in range(MEM_HEADS):
        sl = slice(hd * MEM_HD, (hd + 1) * MEM_HD)
        mk = _rms(kv[:, sl], kg)
        mk_ref[:, sl] = mk
        mkb_ref[:, sl] = mk.astype(BF16)
    mv_ref[...] = kv[:, MEM_W:]
    mvb_ref[...] = kv[:, MEM_W:].astype(BF16)


def _memkv_call(mem, mem_norm, w_mem_kv, mem_k_norm):
    depth = w_mem_kv.shape[0]
    out = jax.ShapeDtypeStruct((depth, N_MEM, MEM_W), F32)
    out_bf = jax.ShapeDtypeStruct((depth, N_MEM, MEM_W), BF16)
    return pl.pallas_call(
        _memkv_body,
        grid=(depth,),
        in_specs=[
            pl.BlockSpec((N_MEM, D_MODEL), lambda i: (0, 0)),
            pl.BlockSpec((None, 1, D_MODEL), lambda i: (i, 0, 0)),
            pl.BlockSpec((None, D_MODEL, 2 * MEM_W), lambda i: (i, 0, 0)),
            pl.BlockSpec((None, 1, MEM_HD), lambda i: (i, 0, 0)),
        ],
        out_specs=[pl.BlockSpec((None, N_MEM, MEM_W), lambda i: (i, 0, 0))] * 4,
        out_shape=[out, out, out_bf, out_bf],
        compiler_params=_params(("arbitrary",)),
        name="memkv",
    )(mem, mem_norm.reshape(depth, 1, D_MODEL), w_mem_kv, mem_k_norm.reshape(depth, 1, MEM_HD))


def _chunk_time(pos):
    return lax.shift_right_logical(pos, 3) + SUBLANES * (pos & (SUBLANES - 1))


def _chunk_perm(tm):
    pos = np.arange(tm)
    src = (pos // CHUNK) * CHUNK + (pos % CHUNK) // SUBLANES + SUBLANES * (pos % SUBLANES)
    return jnp.asarray(src[:, None] == pos[None, :], BF16)


def _proj_a_matmuls(x_ref, g_ref, perm_ref, wqkv_ref, wz_ref, wba_ref, wqm_ref,
                    z_ref, qm_ref, raw_ref, ba_ref):
    h = _rms(x_ref[...], g_ref[...]).astype(BF16)
    h = jnp.dot(perm_ref[...], h, preferred_element_type=F32).astype(BF16)
    z_ref[...] = jnp.dot(h, wz_ref[...], preferred_element_type=F32)
    qm_ref[...] = jnp.dot(h, wqm_ref[...], preferred_element_type=F32)
    ba_ref[...] = jnp.dot(h, wba_ref[...], preferred_element_type=F32)
    raw_ref[...] = jnp.dot(h, wqkv_ref[...], preferred_element_type=F32)


def _proj_a_rows(seqs, raw_ref, ba_ref, cw_ref, alog_ref, dtb_ref,
                 q_ref, k_ref, kb_ref, qd_ref, kd_ref, vb_ref, kbe_ref, cum_ref, cumt_ref,
                 convn_ref, hist_ref):
    c = CHUNK
    tm = raw_ref.shape[0]
    rows = tm // seqs
    n_hist = (GDN_CONV - 1) * SUBLANES

    t_row = _chunk_time(lax.broadcasted_iota(jnp.int32, (c, c), 0))
    t_col = _chunk_time(lax.broadcasted_iota(jnp.int32, (c, c), 1))
    tril = (t_row >= t_col).astype(F32)
    eye_l = (lax.broadcasted_iota(jnp.int32, (LANES, LANES), 0)
             == lax.broadcasted_iota(jnp.int32, (LANES, LANES), 1)).astype(F32)
    sub0 = lax.broadcasted_iota(jnp.int32, (SUBLANES, LANES), 0) == 0

    ba = ba_ref[...]
    beta_all = _sigmoid(ba)
    g_all = -jnp.exp(alog_ref[...]) * _softplus(ba + dtb_ref[...])

    for j in range(tm // c):
        rs = slice(j * c, (j + 1) * c)
        s, lr = (j * c) // rows, (j * c) % rows
        cum = jnp.dot(tril, g_all[rs], precision=_HI, preferred_element_type=F32)
        cum_ref[rs, :] = cum
        cumt_ref[j] = lax.dot_general(eye_l, cum, (((1,), (1,)), ((), ())), precision=_HI,
                                      preferred_element_type=F32)
        e_cum = jnp.exp(cum)
        e_rest = jnp.exp(cum[c - 1:c, :] - cum)
        beta = beta_all[rs]

        def conv_act(lo):
            sl = slice(lo, lo + LANES)
            x = raw_ref[rs, sl]
            prev = (hist_ref[s, :, sl] if lr == 0
                    else raw_ref[j * c - n_hist:j * c, sl])
            shifted = [jnp.where(sub0,
                                 pltpu.roll(prev[i * SUBLANES:(i + 1) * SUBLANES], 1, 0),
                                 pltpu.roll(x[c - n_hist + i * SUBLANES:c - n_hist + (i + 1) * SUBLANES],
                                            1, 0)) for i in range(GDN_CONV - 1)]
            acc = x * cw_ref[GDN_CONV - 1:GDN_CONV, sl]
            for d in range(1, GDN_CONV):
                xd = jnp.concatenate(shifted[GDN_CONV - 1 - d:] + [x[0:c - d * SUBLANES]], axis=0)
                acc = acc + xd * cw_ref[GDN_CONV - 1 - d:GDN_CONV - d, sl]
            return _silu(acc)

        for hd in range(GDN_HEADS):
            hs = slice(hd * GDN_DK, (hd + 1) * GDN_DK)
            b_col = beta[:, hd:hd + 1]
            ec_col = e_cum[:, GDN_HEADS + hd:GDN_HEADS + hd + 1]
            er_col = e_rest[:, GDN_HEADS + hd:GDN_HEADS + hd + 1]
            q = conv_act(hd * GDN_DK)
            q = q * lax.rsqrt(jnp.sum(q * q, axis=-1, keepdims=True) + EPS) * (GDN_DK ** -0.5)
            k = conv_act(GDN_HEADS * GDN_DK + hd * GDN_DK)
            k = k * lax.rsqrt(jnp.sum(k * k, axis=-1, keepdims=True) + EPS)
            v = conv_act(2 * GDN_HEADS * GDN_DK + hd * GDN_DV)
            kb = k * b_col
            q_ref[rs, hs] = q.astype(BF16)
            qd_ref[rs, hs] = (q * ec_col).astype(BF16)
            k_ref[rs, hs] = k.astype(BF16)
            kb_ref[rs, hs] = kb.astype(BF16)
            kd_ref[rs, hs] = (k * er_col).astype(BF16)
            kbe_ref[rs, hs] = kb * ec_col
            vb_ref[rs, hs] = v * b_col

    for s in range(seqs):
        tail = raw_ref[(s + 1) * rows - n_hist:(s + 1) * rows, :]
        hist_ref[s] = tail
        convn_ref[s] = tail


def _proj_a_body(seqs, tiles_per_seq, x_ref, g_ref, perm_ref, wqkv_ref, wz_ref, wba_ref, wqm_ref,
                 conv0_ref, cw_ref, alog_ref, dtb_ref,
                 z_ref, qm_ref, q_ref, k_ref, kb_ref, qd_ref, kd_ref, vb_ref, kbe_ref, cum_ref,
                 cumt_ref, convn_ref, raw0_ref, raw1_ref, ba0_ref, ba1_ref, hist_ref):
    i = pl.program_id(0)

    @pl.when(i == 0)
    def _():
        raw1_ref[...] = jnp.zeros_like(raw1_ref)
        ba1_ref[...] = jnp.zeros_like(ba1_ref)
        hist_ref[...] = jnp.zeros_like(hist_ref)

    @pl.when((i + tiles_per_seq - 1) % tiles_per_seq == 0)
    def _():
        hist_ref[...] = conv0_ref[...]

    def step(raw_w, ba_w, raw_r, ba_r):
        _proj_a_matmuls(x_ref, g_ref, perm_ref, wqkv_ref, wz_ref, wba_ref, wqm_ref,
                        z_ref, qm_ref, raw_w, ba_w)
        _proj_a_rows(seqs, raw_r, ba_r, cw_ref, alog_ref, dtb_ref,
                     q_ref, k_ref, kb_ref, qd_ref, kd_ref, vb_ref, kbe_ref, cum_ref, cumt_ref,
                     convn_ref, hist_ref)

    @pl.when(i % 2 == 0)
    def _():
        step(raw0_ref, ba0_ref, raw1_ref, ba1_ref)

    @pl.when(i % 2 == 1)
    def _():
        step(raw1_ref, ba1_ref, raw0_ref, ba0_ref)


def _proj_a_call(x, g, w, conv0, tm, seqs):
    rows = x.shape[0]
    ns = conv0.shape[0]
    n_tiles = rows // tm
    n_hist = (GDN_CONV - 1) * SUBLANES
    tiles_per_seq = rows // (ns * tm) if seqs == 1 else 1
    seq_of = (lambda i: i // tiles_per_seq) if seqs == 1 else (lambda i: i)
    cur = lambda i: jnp.minimum(i, n_tiles - 1)
    prev = lambda i: jnp.maximum(i - 1, 0)
    tile = lambda width, which: pl.BlockSpec((tm, width), lambda i: (which(i), 0))
    hist = pl.BlockSpec((seqs, n_hist, GDN_QKV), lambda i: (seq_of(prev(i)), 0, 0))
    wide_bf = jax.ShapeDtypeStruct((rows, GDN_VW), BF16)
    wide_f32 = jax.ShapeDtypeStruct((rows, GDN_VW), F32)
    perm = _chunk_perm(tm)
    return pl.pallas_call(
        functools.partial(_proj_a_body, seqs, tiles_per_seq),
        grid=(n_tiles + 1,),
        in_specs=[tile(D_MODEL, cur), _const_spec((1, D_MODEL)), _const_spec(perm.shape)]
        + [_slab_spec(w["in_a"], 0, 0, GDN_QKV), _slab_spec(w["in_a"], 0, GDN_QKV // GDN_VW, GDN_VW),
           _const_spec(w["a_ba"].shape), _const_spec(w["a_qm"].shape)]
        + [hist, _const_spec((GDN_CONV, GDN_QKV)), _const_spec((1, LANES)), _const_spec((1, LANES))],
        out_specs=[tile(GDN_VW, cur), tile(MEM_W, cur)] + [tile(GDN_VW, prev)] * 7
        + [tile(LANES, prev),
           pl.BlockSpec((tm // CHUNK, LANES, CHUNK), lambda i: (prev(i), 0, 0)), hist],
        out_shape=[wide_f32, jax.ShapeDtypeStruct((rows, MEM_W), F32)]
        + [wide_bf] * 5 + [wide_f32] * 2
        + [jax.ShapeDtypeStruct((rows, LANES), F32),
           jax.ShapeDtypeStruct((rows // CHUNK, LANES, CHUNK), F32),
           jax.ShapeDtypeStruct((ns, n_hist, GDN_QKV), F32)],
        scratch_shapes=[pltpu.VMEM((tm, GDN_QKV), F32)] * 2 + [pltpu.VMEM((tm, LANES), F32)] * 2
        + [pltpu.VMEM((seqs, n_hist, GDN_QKV), F32)],
        compiler_params=_params(("arbitrary",)),
        name="proj_a",
    )(x, g.reshape(1, D_MODEL), perm, w["in_a"], w["in_a"], w["a_ba"], w["a_qm"], conv0,
      w["conv_w_a"], w["alog_row"], w["dtb_row"])


def _gdn_body(q_ref, k_ref, kb_ref, qd_ref, kd_ref, vb_ref, kbe_ref, cum_ref, cumt_ref, s0_ref,
              o_ref, sn_ref, s_ref):
    c = CHUNK
    nc = q_ref.shape[0] // c

    @pl.when(pl.program_id(1) == 0)
    def _():
        s_ref[...] = s0_ref[...]

    row = _chunk_time(lax.broadcasted_iota(jnp.int32, (c, c), 0))
    col = _chunk_time(lax.broadcasted_iota(jnp.int32, (c, c), 1))
    causal = row >= col
    strict = row > col

    pairs = [(j, hd) for j in range(nc) for hd in range(GDN_HEADS)]
    rs = lambda j: slice(j * c, (j + 1) * c)
    hs = lambda hd: slice(hd * GDN_DK, (hd + 1) * GDN_DK)
    gl = lambda hd: slice(GDN_HEADS + hd, GDN_HEADS + hd + 1)

    decays = [jnp.exp(jnp.where(causal, cum_ref[rs(j), gl(hd)] - cumt_ref[j, gl(hd), :], -jnp.inf))
              for j, hd in pairs]
    grams = [_mm_nt(jnp.concatenate([kb_ref[rs(j), hs(hd)], q_ref[rs(j), hs(hd)]], axis=0),
                    k_ref[rs(j), hs(hd)]) for j, hd in pairs]
    ps = [jnp.where(strict, -(g[:c] * d), 0.0) for g, d in zip(grams, decays)]
    qks = [g[c:] * d for g, d in zip(grams, decays)]

    ns = ps
    for _ in range(5):
        ps = [_mm(p, p) for p in ps]
        ns = [n + p + _mm(n, p) for n, p in zip(ns, ps)]

    rhss = [jnp.concatenate([vb_ref[rs(j), hs(hd)], kbe_ref[rs(j), hs(hd)]], axis=1)
            for j, hd in pairs]
    sols = [rhs + _mm(n, rhs) for n, rhs in zip(ns, rhss)]

    ss = [s_ref[hd] for hd in range(GDN_HEADS)]
    for j in range(nc):
        sol_j = sols[j * GDN_HEADS:(j + 1) * GDN_HEADS]
        qk_j = qks[j * GDN_HEADS:(j + 1) * GDN_HEADS]
        r1s = [_mm(jnp.concatenate([sol_j[hd][:, GDN_DV:].astype(BF16), qd_ref[rs(j), hs(hd)]],
                                   axis=0), ss[hd]) for hd in range(GDN_HEADS)]
        us = [sol_j[hd][:, :GDN_DV] - r1s[hd][:c] for hd in range(GDN_HEADS)]
        for hd in range(GDN_HEADS):
            o_ref[rs(j), hs(hd)] = r1s[hd][c:] + _mm(qk_j[hd], us[hd])
        ss = [ss[hd] * jnp.exp(cum_ref[j * c + c - 1:j * c + c, gl(hd)])
              + _mm_tn(kd_ref[rs(j), hs(hd)], us[hd]) for hd in range(GDN_HEADS)]
    for hd in range(GDN_HEADS):
        s_ref[hd] = ss[hd]
        sn_ref[hd] = ss[hd]


def _gdn_call(ops, s0, ns, tr):
    t = ops[0].shape[0] // ns
    seq = lambda width: pl.BlockSpec((None, tr, width), lambda s, i: (s, i, 0))
    state = pl.BlockSpec((None, GDN_HEADS, GDN_DK, GDN_DV), lambda s, i: (s, 0, 0, 0))
    args = [a.reshape(ns, t, a.shape[-1]) for a in ops[:8]]
    args.append(ops[8].reshape(ns, t // CHUNK, LANES, CHUNK))
    return pl.pallas_call(
        _gdn_body,
        grid=(ns, t // tr),
        in_specs=[seq(GDN_VW)] * 7
        + [seq(LANES), pl.BlockSpec((None, tr // CHUNK, LANES, CHUNK), lambda s, i: (s, i, 0, 0)),
           state],
        out_specs=[seq(GDN_VW), state],
        out_shape=[jax.ShapeDtypeStruct((ns, t, GDN_VW), F32),
                   jax.ShapeDtypeStruct((ns, GDN_HEADS, GDN_DK, GDN_DV), F32)],
        scratch_shapes=[pltpu.VMEM((GDN_HEADS, GDN_DK, GDN_DV), F32)],
        compiler_params=_params(("arbitrary", "arbitrary")),
        name="gdn",
    )(*args, s0)


def _proj_b_body(x_ref, g_ref, wq_ref, wkv_ref, wqm_ref, qg_ref, kg_ref,
                 qlo_ref, qhi_ref, kd_ref, vd_ref, kc_ref, vc_ref, qm_ref):
    h = _rms(x_ref[...], g_ref[...]).astype(BF16)
    qm_ref[...] = jnp.dot(h, wqm_ref[...], preferred_element_type=F32)
    q = jnp.dot(h, wq_ref[...], preferred_element_type=F32)
    kv = jnp.dot(h, wkv_ref[...], preferred_element_type=F32)
    lo = lax.broadcasted_iota(jnp.int32, (1, LANES), 1) < SWA_HD
    qg = qg_ref[...] * (SWA_HD ** -0.5)
    for p in range(SWA_QW // LANES):
        sl = slice(p * LANES, (p + 1) * LANES)
        x = q[:, sl]
        x2 = x * x
        m_lo = jnp.sum(jnp.where(lo, x2, 0.0), axis=-1, keepdims=True) * (1.0 / SWA_HD)
        m_hi = jnp.sum(jnp.where(lo, 0.0, x2), axis=-1, keepdims=True) * (1.0 / SWA_HD)
        qn = x * jnp.where(lo, lax.rsqrt(m_lo + EPS), lax.rsqrt(m_hi + EPS)) * qg
        qlo_ref[:, sl] = jnp.where(lo, qn, 0.0).astype(BF16)
        qhi_ref[:, sl] = jnp.where(lo, 0.0, qn).astype(BF16)
    kg = kg_ref[...]
    for p in range(SWA_KV_HEADS):
        sl = slice(p * LANES, (p + 1) * LANES)
        kn = _rms(kv[:, sl], kg)
        kc_ref[:, sl] = kn
        kd_ref[:, sl] = kn.astype(BF16)
    v = kv[:, SWA_KV_HEADS * LANES:]
    vc_ref[...] = v
    vd_ref[...] = v.astype(BF16)


def _proj_b_call(x, g, w, tm, keep_all):
    rows = x.shape[0]
    dup_w = SWA_KV_HEADS * LANES
    tile = lambda width: pl.BlockSpec((tm, width), lambda i: (i, 0))
    cache = tile(dup_w) if keep_all else pl.BlockSpec((tm, dup_w), lambda i: (0, 0))
    cache_shape = jax.ShapeDtypeStruct((rows if keep_all else tm, dup_w), F32)
    return pl.pallas_call(
        _proj_b_body,
        grid=(rows // tm,),
        in_specs=[tile(D_MODEL), _const_spec((1, D_MODEL))]
        + [_slab_spec(w["in_b"], 0, 0, SWA_QW), _const_spec(w["b_kv"].shape),
           _slab_spec(w["in_b"], 0, (SWA_QW + 2 * SWA_KVW) // MEM_W, MEM_W)]
        + [_const_spec((1, LANES))] * 2,
        out_specs=[tile(SWA_QW)] * 2 + [tile(dup_w)] * 2 + [cache] * 2 + [tile(MEM_W)],
        out_shape=[jax.ShapeDtypeStruct((rows, SWA_QW), BF16)] * 2
        + [jax.ShapeDtypeStruct((rows, dup_w), BF16)] * 2 + [cache_shape] * 2
        + [jax.ShapeDtypeStruct((rows, MEM_W), F32)],
        compiler_params=_params(("arbitrary",)),
        name="proj_b",
    )(x, g.reshape(1, D_MODEL), w["in_b"], w["b_kv"], w["in_b"], w["q_gain2"], w["k_gain2"])


class _SwaJob:
    def __init__(self, hist_valid, seqs, tile_in_seq, refs, kwin_ref, vwin_ref):
        (self.qlo_ref, self.qhi_ref, self.k_ref, self.v_ref, self.kh_ref, self.vh_ref,
         self.sink_ref) = refs
        self.hist_valid, self.seqs, self.tile_in_seq = hist_valid, seqs, tile_in_seq
        self.kwin_ref, self.vwin_ref = kwin_ref, vwin_ref
        self.tm = self.qlo_ref.shape[0]
        self.rows = self.tm // seqs
        self.units = [(sq, j, h) for sq in range(seqs) for j in range(self.rows // CHUNK)
                      for h in range(SWA_KV_HEADS)]

    def load_history(self):
        self.kwin_ref[:, 0:WINDOW, :] = self.kh_ref[...]
        self.vwin_ref[:, 0:WINDOW, :] = self.vh_ref[...]

    def scores(self):
        c, rows = CHUNK, self.rows
        slab = lambda p: slice(p * LANES, (p + 1) * LANES)
        for sq in range(self.seqs):
            self.kwin_ref[sq, WINDOW:WINDOW + rows, :] = self.k_ref[sq * rows:(sq + 1) * rows, :]
            self.vwin_ref[sq, WINDOW:WINDOW + rows, :] = self.v_ref[sq * rows:(sq + 1) * rows, :]
        out = []
        for sq, j, h in self.units:
            r = slice(sq * rows + j * c, sq * rows + (j + 1) * c)
            qs = jnp.concatenate([self.qlo_ref[r, slab(2 * h)], self.qhi_ref[r, slab(2 * h)],
                                  self.qlo_ref[r, slab(2 * h + 1)], self.qhi_ref[r, slab(2 * h + 1)]],
                                 axis=0)
            s = _mm_nt(qs, self.kwin_ref[sq, j * c:j * c + WINDOW + c, slab(h)])
            if not self.hist_valid and j * c < WINDOW:
                key_col = lax.broadcasted_iota(jnp.int32, (SWA_GROUP * c, WINDOW + c), 1)
                first_key = self.tile_in_seq * rows + j * c - WINDOW
                s = jnp.where(key_col + first_key >= 0, s, -jnp.inf)
            out.append(s)
        return out

    def softmax(self, scores):
        c = CHUNK
        sinks = [jnp.concatenate([jnp.full((c, 1), self.sink_ref[0, h * SWA_GROUP + g], F32)
                                  for g in range(SWA_GROUP)], axis=0)
                 for h in range(SWA_KV_HEADS)]
        out = []
        for s, (sq, j, h) in zip(scores, self.units):
            m = jnp.maximum(jnp.max(s, axis=-1, keepdims=True), sinks[h])
            p = jnp.exp(s - m)
            out.append((p / (jnp.sum(p, axis=-1, keepdims=True) + jnp.exp(sinks[h] - m)))
                       .astype(BF16))
        return out

    def values(self, probs):
        c, rows = CHUNK, self.rows
        slab = lambda p: slice(p * LANES, (p + 1) * LANES)
        lo = lax.broadcasted_iota(jnp.int32, (1, LANES), 1) < SWA_HD
        zero = jnp.zeros((), BF16)
        slabs = {}
        for p, (sq, j, h) in zip(probs, self.units):
            v = self.vwin_ref[sq, j * c:j * c + WINDOW + c, slab(h)]
            v_lo = jnp.where(lo, v, zero)
            v_hi = jnp.where(lo, zero, v)
            for half in range(2):
                slabs[sq, j, 2 * h + half] = (
                    _mm(p[(2 * half) * c:(2 * half + 1) * c], v_lo)
                    + _mm(p[(2 * half + 1) * c:(2 * half + 2) * c], v_hi)).astype(BF16)
        for ref in (self.kwin_ref, self.vwin_ref):
            ref[:, 0:WINDOW, :] = ref[:, rows:rows + WINDOW, :]
        return jnp.concatenate(
            [jnp.concatenate([slabs[sq, j, p] for p in range(SWA_QW // LANES)], axis=1)
             for sq in range(self.seqs) for j in range(rows // c)], axis=0)


def _tail_step(seqs, gated, swa, x_ref, om_ref, gate_refs, qm_ref, mk_ref, mv_ref, mqg_ref, wo_ref,
               gf_ref, wgu_ref, wd_ref, y_ref, mix_w, mix_r):
    tm = x_ref.shape[0]
    rows = tm // seqs
    d_ff = wd_ref.shape[0]
    half = d_ff // 2
    heads = [(sq, hd) for sq in range(seqs) for hd in range(MEM_HEADS)]
    rs = lambda sq: slice(sq * rows, (sq + 1) * rows)
    hs = lambda hd: slice(hd * MEM_HD, (hd + 1) * MEM_HD)

    x = x_ref[...] + jnp.dot(mix_r[...], wo_ref[...], preferred_element_type=F32)

    if swa:
        swa_scores = swa.scores()
    mqg = mqg_ref[...] * (MEM_HD ** -0.5)
    scores = [_mm_nt(_rms(qm_ref[rs(sq), hs(hd)], mqg), mk_ref[sq, :, hs(hd)])
              for sq, hd in heads]

    h = _rms(x, gf_ref[...]).astype(BF16)
    g0 = jnp.dot(h, wgu_ref[:, 0:half], preferred_element_type=F32)
    u0 = jnp.dot(h, wgu_ref[:, d_ff:d_ff + half], preferred_element_type=F32)

    probs = []
    for s in scores:
        p = jnp.exp(s - jnp.max(s, axis=-1, keepdims=True))
        probs.append((p / jnp.sum(p, axis=-1, keepdims=True)).astype(BF16))
    o_heads = [_mm(p, mv_ref[sq, :, hs(hd)]).astype(BF16) for p, (sq, hd) in zip(probs, heads)]
    o_mem = jnp.concatenate(
        [jnp.concatenate(o_heads[sq * MEM_HEADS:(sq + 1) * MEM_HEADS], axis=1)
         for sq in range(seqs)], axis=0)

    act0 = (_silu(g0) * u0).astype(BF16)
    g1 = jnp.dot(h, wgu_ref[:, half:d_ff], preferred_element_type=F32)
    u1 = jnp.dot(h, wgu_ref[:, d_ff + half:], preferred_element_type=F32)

    if gated:
        z_ref, og_ref, unperm_ref = gate_refs
        og = og_ref[...]
        parts = [(_rms(om_ref[:, hd * GDN_DV:(hd + 1) * GDN_DV], og)
                  * _silu(z_ref[:, hd * GDN_DV:(hd + 1) * GDN_DV])).astype(BF16)
                 for hd in range(GDN_HEADS)]
        mixed = jnp.concatenate(parts + [o_mem], axis=1)
        mix_w[...] = jnp.dot(unperm_ref[...], mixed, preferred_element_type=F32).astype(BF16)
    else:
        mix_w[...] = jnp.concatenate([swa.values(swa.softmax(swa_scores)), o_mem], axis=1)

    y = x + jnp.dot(act0, wd_ref[0:half, :], preferred_element_type=F32)
    act1 = (_silu(g1) * u1).astype(BF16)
    y_ref[...] = y + jnp.dot(act1, wd_ref[half:, :], preferred_element_type=F32)


def _tail_body(seqs, tiles_per_seq, hist_valid, x_ref, *refs):
    gated = hist_valid is None
    n_mixer = 4 if gated else 7
    mixer_refs, refs = refs[:n_mixer], refs[n_mixer:]
    qm_ref, mk_ref, mv_ref, mqg_ref, wo_ref, gf_ref, wgu_ref, wd_ref, y_ref = refs[:9]
    mix0_ref, mix1_ref = refs[9:11]
    i = pl.program_id(0)

    @pl.when(i == 0)
    def _():
        mix1_ref[...] = jnp.zeros_like(mix1_ref)

    if gated:
        om_ref, gate_refs, swa = mixer_refs[0], mixer_refs[1:], None
    else:
        om_ref, gate_refs = None, ()
        swa = _SwaJob(hist_valid, seqs, i % tiles_per_seq, mixer_refs, *refs[11:13])

        @pl.when(i % tiles_per_seq == 0)
        def _():
            swa.load_history()

    def step(mix_w, mix_r):
        _tail_step(seqs, gated, swa, x_ref, om_ref, gate_refs, qm_ref, mk_ref, mv_ref, mqg_ref,
                   wo_ref, gf_ref, wgu_ref, wd_ref, y_ref, mix_w, mix_r)

    @pl.when(i % 2 == 0)
    def _():
        step(mix0_ref, mix1_ref)

    @pl.when(i % 2 == 1)
    def _():
        step(mix1_ref, mix0_ref)


def _tail_call(x, gdn, swa, qm, mk, mv, mem_q_gain, w_out, g_ffn, w_gate_up, w_down, layer, tm,
               seqs):
    rows = x.shape[0]
    ns = mk.shape[1]
    n_tiles = rows // tm
    cur = lambda i: jnp.minimum(i, n_tiles - 1)
    prev = lambda i: jnp.maximum(i - 1, 0)
    tile = lambda width, which: pl.BlockSpec((tm, width), lambda i: (which(i), 0))
    tiles_per_seq = rows // (tm * ns) if seqs == 1 else 1
    per_seq = lambda *dims: pl.BlockSpec((seqs,) + dims,
                                         lambda i: (cur(i) // tiles_per_seq,) + (0,) * len(dims))
    mem = pl.BlockSpec((None, seqs, N_MEM, MEM_W),
                       lambda i: (layer, cur(i) // tiles_per_seq, 0, 0))
    if gdn:
        o_raw, z, o_gain = gdn
        hist_valid = None
        mixer_specs = [tile(GDN_VW, cur), tile(GDN_VW, cur), _const_spec((1, GDN_DV)),
                       _const_spec((tm, tm))]
        mixer_args = [o_raw, z, o_gain.reshape(1, GDN_DV), _chunk_perm(tm).T]
        scratch = []
    else:
        *mixer_args, sinks, hist_valid = swa
        dup_w = SWA_KV_HEADS * LANES
        mixer_specs = [tile(SWA_QW, cur), tile(SWA_QW, cur), tile(dup_w, cur), tile(dup_w, cur),
                       per_seq(WINDOW, dup_w), per_seq(WINDOW, dup_w),
                       pl.BlockSpec(memory_space=pltpu.SMEM)]
        mixer_args.append(sinks.reshape(1, SWA_HEADS))
        scratch = [pltpu.VMEM((seqs, WINDOW + tm // seqs, dup_w), BF16)] * 2
    return pl.pallas_call(
        functools.partial(_tail_body, seqs, tiles_per_seq, hist_valid),
        grid=(n_tiles + 1,),
        in_specs=[tile(D_MODEL, prev)] + mixer_specs + [
            tile(MEM_W, cur), mem, mem,
            _const_spec((1, MEM_HD)), _slab_spec(w_out, 0, 0, D_MODEL), _const_spec((1, D_MODEL)),
            _slab_spec(w_gate_up, layer, 0, w_gate_up.shape[2]),
            _slab_spec(w_down, layer, 0, D_MODEL),
        ],
        out_specs=tile(D_MODEL, prev),
        out_shape=jax.ShapeDtypeStruct((rows, D_MODEL), F32),
        scratch_shapes=[pltpu.VMEM((tm, w_out.shape[1]), BF16)] * 2 + scratch,
        compiler_params=_params(("arbitrary",)),
        name="tail",
    )(x, *mixer_args, qm, mk, mv, mem_q_gain.reshape(1, MEM_HD), w_out,
      g_ffn.reshape(1, D_MODEL), w_gate_up, w_down)


def _lane_row(vals, offset):
    return jnp.zeros((1, LANES), F32).at[0, offset:offset + vals.shape[0]].set(vals)


def _dup_heads(a):
    lead = a.shape[:-1]
    a = a.reshape(lead + (SWA_KV_HEADS, 1, SWA_HD))
    return jnp.broadcast_to(a, lead + (SWA_KV_HEADS, 2, SWA_HD)).reshape(lead + (SWA_KV_HEADS * LANES,))


def _undup_heads(a):
    lead = a.shape[:-1]
    return a.reshape(lead + (SWA_KV_HEADS, 2, SWA_HD))[..., 0, :]


def _trunk(x, mem_k, mem_v, gdn_conv, gdn_state, swa_k_hist, swa_v_hist, hist_valid, w):
    ns, t, _ = x.shape
    rows = ns * t
    tm = ROW_TILE
    seqs = max(1, tm // t)
    x2 = x.reshape(rows, D_MODEL)

    conv0 = jnp.pad(gdn_conv[:, :, None, :], ((0, 0), (0, 0), (SUBLANES - 1, 0), (0, 0)))
    conv0 = conv0.reshape(ns, (GDN_CONV - 1) * SUBLANES, GDN_QKV)
    z, qm, *gdn_ops, conv_new = _proj_a_call(x2, w["norm_mix"][0], w, conv0, tm, seqs)
    conv_new = conv_new[:, SUBLANES - 1::SUBLANES, :]
    o_raw, s_new = _gdn_call(gdn_ops, gdn_state, ns, min(t, GDN_TILE_ROWS))
    x2 = _tail_call(x2, (o_raw.reshape(rows, GDN_VW), z, w["o_norm_a"]), None, qm, mem_k, mem_v,
                    w["mem_q_norm"][0], w["out_a"], w["norm_ffn"][0], w["gate_up"], w["down"], 0,
                    tm, seqs)

    qlo, qhi, kd, vd, k_cache, v_cache, qm = _proj_b_call(x2, w["norm_mix"][1], w, tm, ns > 1)
    swa = (qlo, qhi, kd, vd, _dup_heads(swa_k_hist).astype(BF16),
           _dup_heads(swa_v_hist).astype(BF16), w["sinks_b"], hist_valid)
    x2 = _tail_call(x2, None, swa, qm, mem_k, mem_v, w["mem_q_norm"][1], w["out_b"],
                    w["norm_ffn"][1], w["gate_up"], w["down"], 1, tm, seqs)

    keep = min(WINDOW, t) if ns == 1 else t
    k_new = _undup_heads(k_cache).reshape(ns, -1, SWA_KVW)[:, -keep:]
    v_new = _undup_heads(v_cache).reshape(ns, -1, SWA_KVW)[:, -keep:]
    return x2.reshape(ns, t, D_MODEL), conv_new, s_new, k_new, v_new


def kernel(x_prompt, x_sample, mem_prompt, cache_mem_k, cache_mem_v, state_gdn, state_gdn_conv, cache_swa_k, cache_swa_v, norm_mix, norm_ffn, mem_norm, w_mem_kv, mem_q_norm, mem_k_norm, w_in_a, conv_w_a, a_log, dt_bias, o_norm_a, w_out_a, w_in_b, q_norm_b, k_norm_b, sinks_b, w_out_b, w_gate_up, w_down):
    bsz = x_prompt.shape[0]
    dec = x_sample.shape[0]
    wa = w_in_a[0]
    ba_lo = GDN_QKV + GDN_VW
    qm_lo = ba_lo + 2 * GDN_HEADS
    wb = w_in_b[0]
    w = {
        "norm_mix": norm_mix, "norm_ffn": norm_ffn, "mem_q_norm": mem_q_norm,
        "in_a": w_in_a.astype(BF16), "in_b": w_in_b.astype(BF16),
        "a_ba": jnp.pad(wa[:, ba_lo:qm_lo], ((0, 0), (0, LANES - 2 * GDN_HEADS))).astype(BF16),
        "a_qm": wa[:, qm_lo:].astype(BF16),
        "b_kv": jnp.concatenate([_dup_heads(wb[:, SWA_QW:SWA_QW + SWA_KVW]),
                                 _dup_heads(wb[:, SWA_QW + SWA_KVW:SWA_QW + 2 * SWA_KVW])],
                                axis=1).astype(BF16),
        "q_gain2": jnp.tile(q_norm_b[0], 2).reshape(1, LANES),
        "k_gain2": jnp.tile(k_norm_b[0], 2).reshape(1, LANES),
        "conv_w_a": conv_w_a[0], "o_norm_a": o_norm_a[0],
        "alog_row": _lane_row(a_log[0], GDN_HEADS), "dtb_row": _lane_row(dt_bias[0], GDN_HEADS),
        "out_a": w_out_a.astype(BF16), "out_b": w_out_b.astype(BF16),
        "sinks_b": sinks_b[0],
        "gate_up": w_gate_up.astype(BF16), "down": w_down.astype(BF16),
    }

    mk, mv, mk_bf, mv_bf = _memkv_call(mem_prompt[0], mem_norm, w_mem_kv, mem_k_norm)
    depth = mk.shape[0]
    new_mem_k = mk.reshape(depth, bsz, N_MEM, MEM_HEADS, MEM_HD)
    new_mem_v = mv.reshape(depth, bsz, N_MEM, MEM_HEADS, MEM_HD)

    zero_conv = jnp.zeros((bsz, GDN_CONV - 1, GDN_QKV), F32)
    zero_state = jnp.zeros((bsz, GDN_HEADS, GDN_DK, GDN_DV), F32)
    zero_hist = jnp.zeros((bsz, WINDOW, SWA_KVW), F32)
    y_p, conv_p, state_p, k_p, v_p = _trunk(
        x_prompt, mk_bf.reshape(depth, bsz, N_MEM, MEM_W), mv_bf.reshape(depth, bsz, N_MEM, MEM_W),
        zero_conv, zero_state, zero_hist, zero_hist, False, w)

    y_s, conv_s, state_s, k_s, v_s = _trunk(
        x_sample, cache_mem_k.reshape(depth, dec, N_MEM, MEM_W).astype(BF16),
        cache_mem_v.reshape(depth, dec, N_MEM, MEM_W).astype(BF16), state_gdn_conv[0], state_gdn[0],
        cache_swa_k[0].reshape(dec, WINDOW, SWA_KVW), cache_swa_v[0].reshape(dec, WINDOW, SWA_KVW),
        True, w)

    kv_shape = lambda a: a.reshape(a.shape[0], a.shape[1], SWA_KV_HEADS, SWA_HD)[None]
    return (y_p, y_s, state_p[None], conv_p[None], state_s[None], conv_s[None],
            kv_shape(k_p), kv_shape(v_p), kv_shape(k_s), kv_shape(v_s), new_mem_k, new_mem_v)
```

```python
import functools

import jax
import jax.numpy as jnp
import numpy as np
from jax import lax
from jax.experimental import pallas as pl
from jax.experimental.pallas import tpu as pltpu

F32 = jnp.float32
BF16 = jnp.bfloat16

D_MODEL = 1024
CHUNK = 64
EPS = 1e-6
GDN_HEADS = 8
GDN_DK = 128
GDN_DV = 128
GDN_CONV = 4
GDN_QKV = GDN_HEADS * (2 * GDN_DK + GDN_DV)
GDN_VW = GDN_HEADS * GDN_DV
SWA_HEADS = 16
SWA_KV_HEADS = 4
SWA_HD = 64
SWA_GROUP = SWA_HEADS // SWA_KV_HEADS
SWA_QW = SWA_HEADS * SWA_HD
SWA_KVW = SWA_KV_HEADS * SWA_HD
WINDOW = 128
N_MEM = 256
MEM_HEADS = 4
MEM_HD = 128
MEM_W = MEM_HEADS * MEM_HD
LANES = 128
SUBLANES = 8
VMEM_LIMIT = 56 * 1024 * 1024
ROW_TILE = 256
PROJ_B_TILE = 512
SWA_TILE_ROWS = 4 * CHUNK
GDN_TILE_ROWS = 8 * CHUNK

_HI = lax.Precision.HIGHEST


def _mm(a, b):
    return jnp.dot(a.astype(BF16), b.astype(BF16), preferred_element_type=F32)


def _mm_nt(a, b):
    return lax.dot_general(a.astype(BF16), b.astype(BF16), (((1,), (1,)), ((), ())),
                           preferred_element_type=F32)


def _mm_tn(a, b):
    return lax.dot_general(a.astype(BF16), b.astype(BF16), (((0,), (0,)), ((), ())),
                           preferred_element_type=F32)


def _rms(x, g):
    return x * lax.rsqrt(jnp.mean(x * x, axis=-1, keepdims=True) + EPS) * g


def _sigmoid(x):
    return 1.0 / (1.0 + jnp.exp(-x))


def _silu(x):
    hx = 0.5 * x
    return hx * jnp.tanh(hx) + hx


def _softplus(x):
    return jnp.maximum(x, 0.0) + jnp.log1p(jnp.exp(-jnp.abs(x)))


def _const_spec(shape):
    nd = len(shape)
    return pl.BlockSpec(shape, lambda *_: (0,) * nd, pipeline_mode=pl.Buffered(1))


def _slab_spec(arr, layer, col_block, width):
    return pl.BlockSpec((None, arr.shape[1], width), lambda *_: (layer, 0, col_block),
                        pipeline_mode=pl.Buffered(1))


def _params(sem):
    return pltpu.CompilerParams(dimension_semantics=sem, vmem_limit_bytes=VMEM_LIMIT)


def _memkv_body(mem_ref, g_ref, w_ref, kg_ref, mk_ref, mv_ref, mkb_ref, mvb_ref):
    h = _rms(mem_ref[...], g_ref[...])
    kv = _mm(h, w_ref[...])
    kg = kg_ref[...]
    for hd in range(MEM_HEADS):
        sl = slice(hd * MEM_HD, (hd + 1) * MEM_HD)
        mk = _rms(kv[:, sl], kg)
        mk_ref[:, sl] = mk
        mkb_ref[:, sl] = mk.astype(BF16)
    mv_ref[...] = kv[:, MEM_W:]
    mvb_ref[...] = kv[:, MEM_W:].astype(BF16)


def _memkv_call(mem, mem_norm, w_mem_kv, mem_k_norm):
    depth = w_mem_kv.shape[0]
    out = jax.ShapeDtypeStruct((depth, N_MEM, MEM_W), F32)
    out_bf = jax.ShapeDtypeStruct((depth, N_MEM, MEM_W), BF16)
    return pl.pallas_call(
        _memkv_body,
        grid=(depth,),
        in_specs=[
            pl.BlockSpec((N_MEM, D_MODEL), lambda i: (0, 0)),
            pl.BlockSpec((None, 1, D_MODEL), lambda i: (i, 0, 0)),
            pl.BlockSpec((None, D_MODEL, 2 * MEM_W), lambda i: (i, 0, 0)),
            pl.BlockSpec((None, 1, MEM_HD), lambda i: (i, 0, 0)),
        ],
        out_specs=[pl.BlockSpec((None, N_MEM, MEM_W), lambda i: (i, 0, 0))] * 4,
        out_shape=[out, out, out_bf, out_bf],
        compiler_params=_params(("arbitrary",)),
        name="memkv",
    )(mem, mem_norm.reshape(depth, 1, D_MODEL), w_mem_kv, mem_k_norm.reshape(depth, 1, MEM_HD))


def _chunk_time(pos):
    return lax.shift_right_logical(pos, 3) + SUBLANES * (pos & (SUBLANES - 1))


def _chunk_perm(tm):
    pos = np.arange(tm)
    src = (pos // CHUNK) * CHUNK + (pos % CHUNK) // SUBLANES + SUBLANES * (pos % SUBLANES)
    return jnp.asarray(src[:, None] == pos[None, :], BF16)


def _proj_a_matmuls(x_ref, g_ref, perm_ref, wqkv_ref, wz_ref, wba_ref, wqm_ref,
                    z_ref, qm_ref, raw_ref, ba_ref):
    h = _rms(x_ref[...], g_ref[...]).astype(BF16)
    h = jnp.dot(perm_ref[...], h, preferred_element_type=F32).astype(BF16)
    z_ref[...] = jnp.dot(h, wz_ref[...], preferred_element_type=F32)
    qm_ref[...] = jnp.dot(h, wqm_ref[...], preferred_element_type=F32)
    ba_ref[...] = jnp.dot(h, wba_ref[...], preferred_element_type=F32)
    raw_ref[...] = jnp.dot(h, wqkv_ref[...], preferred_element_type=F32)


def _proj_a_rows(seqs, raw_ref, ba_ref, cw_ref, alog_ref, dtb_ref,
                 q_ref, k_ref, kb_ref, qd_ref, kd_ref, vb_ref, kbe_ref, cum_ref, cumt_ref,
                 convn_ref, hist_ref):
    c = CHUNK
    tm = raw_ref.shape[0]
    rows = tm // seqs
    n_hist = (GDN_CONV - 1) * SUBLANES

    t_row = _chunk_time(lax.broadcasted_iota(jnp.int32, (c, c), 0))
    t_col = _chunk_time(lax.broadcasted_iota(jnp.int32, (c, c), 1))
    tril = (t_row >= t_col).astype(F32)
    eye_l = (lax.broadcasted_iota(jnp.int32, (LANES, LANES), 0)
             == lax.broadcasted_iota(jnp.int32, (LANES, LANES), 1)).astype(F32)
    sub0 = lax.broadcasted_iota(jnp.int32, (SUBLANES, LANES), 0) == 0

    ba = ba_ref[...]
    beta_all = _sigmoid(ba)
    g_all = -jnp.exp(alog_ref[...]) * _softplus(ba + dtb_ref[...])

    for j in range(tm // c):
        rs = slice(j * c, (j + 1) * c)
        s, lr = (j * c) // rows, (j * c) % rows
        cum = jnp.dot(tril, g_all[rs], precision=_HI, preferred_element_type=F32)
        cum_ref[rs, :] = cum
        cumt_ref[j] = lax.dot_general(eye_l, cum, (((1,), (1,)), ((), ())), precision=_HI,
                                      preferred_element_type=F32)
        e_cum = jnp.exp(cum)
        e_rest = jnp.exp(cum[c - 1:c, :] - cum)
        beta = beta_all[rs]

        def conv_act(lo):
            sl = slice(lo, lo + LANES)
            x = raw_ref[rs, sl]
            prev = (hist_ref[s, :, sl] if lr == 0
                    else raw_ref[j * c - n_hist:j * c, sl])
            shifted = [jnp.where(sub0,
                                 pltpu.roll(prev[i * SUBLANES:(i + 1) * SUBLANES], 1, 0),
                                 pltpu.roll(x[c - n_hist + i * SUBLANES:c - n_hist + (i + 1) * SUBLANES],
                                            1, 0)) for i in range(GDN_CONV - 1)]
            acc = x * cw_ref[GDN_CONV - 1:GDN_CONV, sl]
            for d in range(1, GDN_CONV):
                xd = jnp.concatenate(shifted[GDN_CONV - 1 - d:] + [x[0:c - d * SUBLANES]], axis=0)
                acc = acc + xd * cw_ref[GDN_CONV - 1 - d:GDN_CONV - d, sl]
            return _silu(acc)

        for hd in range(GDN_HEADS):
            hs = slice(hd * GDN_DK, (hd + 1) * GDN_DK)
            b_col = beta[:, hd:hd + 1]
            ec_col = e_cum[:, GDN_HEADS + hd:GDN_HEADS + hd + 1]
            er_col = e_rest[:, GDN_HEADS + hd:GDN_HEADS + hd + 1]
            q = conv_act(hd * GDN_DK)
            q = q * (lax.rsqrt(jnp.sum(q * q, axis=-1, keepdims=True) + EPS) * (GDN_DK ** -0.5))
            k = conv_act(GDN_HEADS * GDN_DK + hd * GDN_DK)
            k = k * lax.rsqrt(jnp.sum(k * k, axis=-1, keepdims=True) + EPS)
            v = conv_act(2 * GDN_HEADS * GDN_DK + hd * GDN_DV)
            kb = k * b_col
            q_ref[rs, hs] = q.astype(BF16)
            qd_ref[rs, hs] = (q * ec_col).astype(BF16)
            k_ref[rs, hs] = k.astype(BF16)
            kb_ref[rs, hs] = kb.astype(BF16)
            kd_ref[rs, hs] = (k * er_col).astype(BF16)
            kbe_ref[rs, hs] = kb * ec_col
            vb_ref[rs, hs] = v * b_col

    for s in range(seqs):
        tail = raw_ref[(s + 1) * rows - n_hist:(s + 1) * rows, :]
        hist_ref[s] = tail
        convn_ref[s] = tail


def _proj_a_body(seqs, tiles_per_seq, x_ref, g_ref, perm_ref, wqkv_ref, wz_ref, wba_ref, wqm_ref,
                 conv0_ref, cw_ref, alog_ref, dtb_ref,
                 z_ref, qm_ref, q_ref, k_ref, kb_ref, qd_ref, kd_ref, vb_ref, kbe_ref, cum_ref,
                 cumt_ref, convn_ref, raw0_ref, raw1_ref, ba0_ref, ba1_ref, hist_ref):
    i = pl.program_id(0)

    @pl.when(i == 0)
    def _():
        raw1_ref[...] = jnp.zeros_like(raw1_ref)
        ba1_ref[...] = jnp.zeros_like(ba1_ref)
        hist_ref[...] = jnp.zeros_like(hist_ref)

    @pl.when((i + tiles_per_seq - 1) % tiles_per_seq == 0)
    def _():
        hist_ref[...] = conv0_ref[...]

    def step(raw_w, ba_w, raw_r, ba_r):
        _proj_a_matmuls(x_ref, g_ref, perm_ref, wqkv_ref, wz_ref, wba_ref, wqm_ref,
                        z_ref, qm_ref, raw_w, ba_w)
        _proj_a_rows(seqs, raw_r, ba_r, cw_ref, alog_ref, dtb_ref,
                     q_ref, k_ref, kb_ref, qd_ref, kd_ref, vb_ref, kbe_ref, cum_ref, cumt_ref,
                     convn_ref, hist_ref)

    @pl.when(i % 2 == 0)
    def _():
        step(raw0_ref, ba0_ref, raw1_ref, ba1_ref)

    @pl.when(i % 2 == 1)
    def _():
        step(raw1_ref, ba1_ref, raw0_ref, ba0_ref)


def _proj_a_call(x, g, w, conv0, tm, seqs):
    rows = x.shape[0]
    ns = conv0.shape[0]
    n_tiles = rows // tm
    n_hist = (GDN_CONV - 1) * SUBLANES
    tiles_per_seq = rows // (ns * tm) if seqs == 1 else 1
    seq_of = (lambda i: i // tiles_per_seq) if seqs == 1 else (lambda i: i)
    cur = lambda i: jnp.minimum(i, n_tiles - 1)
    prev = lambda i: jnp.maximum(i - 1, 0)
    tile = lambda width, which: pl.BlockSpec((tm, width), lambda i: (which(i), 0))
    hist = pl.BlockSpec((seqs, n_hist, GDN_QKV), lambda i: (seq_of(prev(i)), 0, 0))
    wide_bf = jax.ShapeDtypeStruct((rows, GDN_VW), BF16)
    wide_f32 = jax.ShapeDtypeStruct((rows, GDN_VW), F32)
    perm = _chunk_perm(tm)
    return pl.pallas_call(
        functools.partial(_proj_a_body, seqs, tiles_per_seq),
        grid=(n_tiles + 1,),
        in_specs=[tile(D_MODEL, cur), _const_spec((1, D_MODEL)), _const_spec(perm.shape)]
        + [_slab_spec(w["in_a"], 0, 0, GDN_QKV), _slab_spec(w["in_a"], 0, GDN_QKV // GDN_VW, GDN_VW),
           _const_spec(w["a_ba"].shape), _const_spec(w["a_qm"].shape)]
        + [hist, _const_spec((GDN_CONV, GDN_QKV)), _const_spec((1, LANES)), _const_spec((1, LANES))],
        out_specs=[tile(GDN_VW, cur), tile(MEM_W, cur)] + [tile(GDN_VW, prev)] * 7
        + [tile(LANES, prev),
           pl.BlockSpec((tm // CHUNK, LANES, CHUNK), lambda i: (prev(i), 0, 0)), hist],
        out_shape=[wide_f32, jax.ShapeDtypeStruct((rows, MEM_W), F32)]
        + [wide_bf] * 5 + [wide_f32] * 2
        + [jax.ShapeDtypeStruct((rows, LANES), F32),
           jax.ShapeDtypeStruct((rows // CHUNK, LANES, CHUNK), F32),
           jax.ShapeDtypeStruct((ns, n_hist, GDN_QKV), F32)],
        scratch_shapes=[pltpu.VMEM((tm, GDN_QKV), F32)] * 2 + [pltpu.VMEM((tm, LANES), F32)] * 2
        + [pltpu.VMEM((seqs, n_hist, GDN_QKV), F32)],
        compiler_params=_params(("arbitrary",)),
        name="proj_a",
    )(x, g.reshape(1, D_MODEL), perm, w["in_a"], w["in_a"], w["a_ba"], w["a_qm"], conv0,
      w["conv_w_a"], w["alog_row"], w["dtb_row"])


def _gdn_body(q_ref, k_ref, kb_ref, qd_ref, kd_ref, vb_ref, kbe_ref, cum_ref, cumt_ref, s0_ref,
              o_ref, sn_ref, s_ref):
    c = CHUNK
    nc = q_ref.shape[0] // c

    @pl.when(pl.program_id(1) == 0)
    def _():
        s_ref[...] = s0_ref[...]

    row = _chunk_time(lax.broadcasted_iota(jnp.int32, (c, c), 0))
    col = _chunk_time(lax.broadcasted_iota(jnp.int32, (c, c), 1))
    causal = row >= col
    strict = row > col

    pairs = [(j, hd) for j in range(nc) for hd in range(GDN_HEADS)]
    rs = lambda j: slice(j * c, (j + 1) * c)
    hs = lambda hd: slice(hd * GDN_DK, (hd + 1) * GDN_DK)
    gl = lambda hd: slice(GDN_HEADS + hd, GDN_HEADS + hd + 1)

    decays = [jnp.exp(jnp.where(causal, cum_ref[rs(j), gl(hd)] - cumt_ref[j, gl(hd), :], -jnp.inf))
              for j, hd in pairs]
    grams = [_mm_nt(jnp.concatenate([kb_ref[rs(j), hs(hd)], q_ref[rs(j), hs(hd)]], axis=0),
                    k_ref[rs(j), hs(hd)]) for j, hd in pairs]
    ps = [jnp.where(strict, -(g[:c] * d), 0.0) for g, d in zip(grams, decays)]
    qks = [g[c:] * d for g, d in zip(grams, decays)]

    ns = ps
    for _ in range(5):
        ps = [_mm(p, p) for p in ps]
        ns = [n + p + _mm(n, p) for n, p in zip(ns, ps)]

    rhss = [jnp.concatenate([vb_ref[rs(j), hs(hd)], kbe_ref[rs(j), hs(hd)]], axis=1)
            for j, hd in pairs]
    sols = [rhs + _mm(n, rhs) for n, rhs in zip(ns, rhss)]

    ss = [s_ref[hd] for hd in range(GDN_HEADS)]
    for j in range(nc):
        sol_j = sols[j * GDN_HEADS:(j + 1) * GDN_HEADS]
        qk_j = qks[j * GDN_HEADS:(j + 1) * GDN_HEADS]
        r1s = [_mm(jnp.concatenate([sol_j[hd][:, GDN_DV:].astype(BF16), qd_ref[rs(j), hs(hd)]],
                                   axis=0), ss[hd]) for hd in range(GDN_HEADS)]
        us = [sol_j[hd][:, :GDN_DV] - r1s[hd][:c] for hd in range(GDN_HEADS)]
        for hd in range(GDN_HEADS):
            o_ref[rs(j), hs(hd)] = r1s[hd][c:] + _mm(qk_j[hd], us[hd])
        ss = [ss[hd] * jnp.exp(cum_ref[j * c + c - 1:j * c + c, gl(hd)])
              + _mm_tn(kd_ref[rs(j), hs(hd)], us[hd]) for hd in range(GDN_HEADS)]
    for hd in range(GDN_HEADS):
        s_ref[hd] = ss[hd]
        sn_ref[hd] = ss[hd]


def _gdn_call(ops, s0, ns, tr):
    t = ops[0].shape[0] // ns
    seq = lambda width: pl.BlockSpec((None, tr, width), lambda s, i: (s, i, 0))
    state = pl.BlockSpec((None, GDN_HEADS, GDN_DK, GDN_DV), lambda s, i: (s, 0, 0, 0))
    args = [a.reshape(ns, t, a.shape[-1]) for a in ops[:8]]
    args.append(ops[8].reshape(ns, t // CHUNK, LANES, CHUNK))
    return pl.pallas_call(
        _gdn_body,
        grid=(ns, t // tr),
        in_specs=[seq(GDN_VW)] * 7
        + [seq(LANES), pl.BlockSpec((None, tr // CHUNK, LANES, CHUNK), lambda s, i: (s, i, 0, 0)),
           state],
        out_specs=[seq(GDN_VW), state],
        out_shape=[jax.ShapeDtypeStruct((ns, t, GDN_VW), F32),
                   jax.ShapeDtypeStruct((ns, GDN_HEADS, GDN_DK, GDN_DV), F32)],
        scratch_shapes=[pltpu.VMEM((GDN_HEADS, GDN_DK, GDN_DV), F32)],
        compiler_params=_params(("arbitrary", "arbitrary")),
        name="gdn",
    )(*args, s0)


def _proj_b_body(x_ref, g_ref, wq_ref, wkv_ref, wqm_ref, qg_ref, kg_ref,
                 qlo_ref, qhi_ref, kd_ref, vd_ref, kc_ref, vc_ref, qm_ref):
    h = _rms(x_ref[...], g_ref[...]).astype(BF16)
    qm_ref[...] = jnp.dot(h, wqm_ref[...], preferred_element_type=F32)
    q = jnp.dot(h, wq_ref[...], preferred_element_type=F32)
    kv = jnp.dot(h, wkv_ref[...], preferred_element_type=F32)
    lo = lax.broadcasted_iota(jnp.int32, (1, LANES), 1) < SWA_HD
    qg = qg_ref[...] * (SWA_HD ** -0.5)
    for p in range(SWA_QW // LANES):
        sl = slice(p * LANES, (p + 1) * LANES)
        x = q[:, sl]
        x2 = x * x
        m_lo = jnp.sum(jnp.where(lo, x2, 0.0), axis=-1, keepdims=True) * (1.0 / SWA_HD)
        m_hi = jnp.sum(jnp.where(lo, 0.0, x2), axis=-1, keepdims=True) * (1.0 / SWA_HD)
        qn = x * jnp.where(lo, lax.rsqrt(m_lo + EPS), lax.rsqrt(m_hi + EPS)) * qg
        qlo_ref[:, sl] = jnp.where(lo, qn, 0.0).astype(BF16)
        qhi_ref[:, sl] = jnp.where(lo, 0.0, qn).astype(BF16)
    kg = kg_ref[...]
    for p in range(SWA_KV_HEADS):
        sl = slice(p * LANES, (p + 1) * LANES)
        kn = _rms(kv[:, sl], kg)
        kc_ref[:, sl] = kn
        kd_ref[:, sl] = kn.astype(BF16)
    v = kv[:, SWA_KV_HEADS * LANES:]
    vc_ref[...] = v
    vd_ref[...] = v.astype(BF16)


def _proj_b_call(x, g, w, tm, keep_all):
    rows = x.shape[0]
    dup_w = SWA_KV_HEADS * LANES
    tile = lambda width: pl.BlockSpec((tm, width), lambda i: (i, 0))
    cache = tile(dup_w) if keep_all else pl.BlockSpec((tm, dup_w), lambda i: (0, 0))
    cache_shape = jax.ShapeDtypeStruct((rows if keep_all else tm, dup_w), F32)
    return pl.pallas_call(
        _proj_b_body,
        grid=(rows // tm,),
        in_specs=[tile(D_MODEL), _const_spec((1, D_MODEL))]
        + [_slab_spec(w["in_b"], 0, 0, SWA_QW), _const_spec(w["b_kv"].shape),
           _slab_spec(w["in_b"], 0, (SWA_QW + 2 * SWA_KVW) // MEM_W, MEM_W)]
        + [_const_spec((1, LANES))] * 2,
        out_specs=[tile(SWA_QW)] * 2 + [tile(dup_w)] * 2 + [cache] * 2 + [tile(MEM_W)],
        out_shape=[jax.ShapeDtypeStruct((rows, SWA_QW), BF16)] * 2
        + [jax.ShapeDtypeStruct((rows, dup_w), BF16)] * 2 + [cache_shape] * 2
        + [jax.ShapeDtypeStruct((rows, MEM_W), F32)],
        compiler_params=_params(("arbitrary",)),
        name="proj_b",
    )(x, g.reshape(1, D_MODEL), w["in_b"], w["b_kv"], w["in_b"], w["q_gain2"], w["k_gain2"])


class _SwaJob:
    def __init__(self, hist_valid, seqs, tile_in_seq, refs, kwin_ref, vwin_ref):
        (self.qlo_ref, self.qhi_ref, self.k_ref, self.v_ref, self.kh_ref, self.vh_ref,
         self.sink_ref) = refs
        self.hist_valid, self.seqs, self.tile_in_seq = hist_valid, seqs, tile_in_seq
        self.kwin_ref, self.vwin_ref = kwin_ref, vwin_ref
        self.tm = self.qlo_ref.shape[0]
        self.rows = self.tm // seqs
        self.units = [(sq, j, h) for sq in range(seqs) for j in range(self.rows // CHUNK)
                      for h in range(SWA_KV_HEADS)]

    def load_history(self):
        self.kwin_ref[:, 0:WINDOW, :] = self.kh_ref[...]
        self.vwin_ref[:, 0:WINDOW, :] = self.vh_ref[...]

    def scores(self):
        c, rows = CHUNK, self.rows
        slab = lambda p: slice(p * LANES, (p + 1) * LANES)
        for sq in range(self.seqs):
            self.kwin_ref[sq, WINDOW:WINDOW + rows, :] = self.k_ref[sq * rows:(sq + 1) * rows, :]
            self.vwin_ref[sq, WINDOW:WINDOW + rows, :] = self.v_ref[sq * rows:(sq + 1) * rows, :]
        out = []
        for sq, j, h in self.units:
            r = slice(sq * rows + j * c, sq * rows + (j + 1) * c)
            qs = jnp.concatenate([self.qlo_ref[r, slab(2 * h)], self.qhi_ref[r, slab(2 * h)],
                                  self.qlo_ref[r, slab(2 * h + 1)], self.qhi_ref[r, slab(2 * h + 1)]],
                                 axis=0)
            s = _mm_nt(qs, self.kwin_ref[sq, j * c:j * c + WINDOW + c, slab(h)])
            if not self.hist_valid and j * c < WINDOW:
                key_col = lax.broadcasted_iota(jnp.int32, (SWA_GROUP * c, WINDOW + c), 1)
                first_key = self.tile_in_seq * rows + j * c - WINDOW
                s = jnp.where(key_col + first_key >= 0, s, -jnp.inf)
            out.append(s)
        return out

    def softmax(self, scores):
        c = CHUNK
        sinks = [jnp.concatenate([jnp.full((c, 1), self.sink_ref[0, h * SWA_GROUP + g], F32)
                                  for g in range(SWA_GROUP)], axis=0)
                 for h in range(SWA_KV_HEADS)]
        out = []
        for s, (sq, j, h) in zip(scores, self.units):
            m = jnp.maximum(jnp.max(s, axis=-1, keepdims=True), sinks[h])
            p = jnp.exp(s - m)
            out.append((p / (jnp.sum(p, axis=-1, keepdims=True) + jnp.exp(sinks[h] - m)))
                       .astype(BF16))
        return out

    def values(self, probs):
        c, rows = CHUNK, self.rows
        slab = lambda p: slice(p * LANES, (p + 1) * LANES)
        lo = lax.broadcasted_iota(jnp.int32, (1, LANES), 1) < SWA_HD
        zero = jnp.zeros((), BF16)
        slabs = {}
        for p, (sq, j, h) in zip(probs, self.units):
            v = self.vwin_ref[sq, j * c:j * c + WINDOW + c, slab(h)]
            v_lo = jnp.where(lo, v, zero)
            v_hi = jnp.where(lo, zero, v)
            for half in range(2):
                slabs[sq, j, 2 * h + half] = (
                    _mm(p[(2 * half) * c:(2 * half + 1) * c], v_lo)
                    + _mm(p[(2 * half + 1) * c:(2 * half + 2) * c], v_hi)).astype(BF16)
        for ref in (self.kwin_ref, self.vwin_ref):
            ref[:, 0:WINDOW, :] = ref[:, rows:rows + WINDOW, :]
        return jnp.concatenate(
            [jnp.concatenate([slabs[sq, j, p] for p in range(SWA_QW // LANES)], axis=1)
             for sq in range(self.seqs) for j in range(rows // c)], axis=0)


def _tail_step(seqs, gated, swa, x_ref, om_ref, gate_refs, qm_ref, mk_ref, mv_ref, mqg_ref, wo_ref,
               gf_ref, wgu_ref, wd_ref, y_ref, mix_w, mix_r):
    tm = x_ref.shape[0]
    rows = tm // seqs
    d_ff = wd_ref.shape[0]
    half = d_ff // 2
    heads = [(sq, hd) for sq in range(seqs) for hd in range(MEM_HEADS)]
    rs = lambda sq: slice(sq * rows, (sq + 1) * rows)
    hs = lambda hd: slice(hd * MEM_HD, (hd + 1) * MEM_HD)

    x = x_ref[...] + jnp.dot(mix_r[...], wo_ref[...], preferred_element_type=F32)

    if swa:
        swa_scores = swa.scores()
    mqg = mqg_ref[...] * (MEM_HD ** -0.5)
    scores = [_mm_nt(_rms(qm_ref[rs(sq), hs(hd)], mqg), mk_ref[sq, :, hs(hd)])
              for sq, hd in heads]

    h = _rms(x, gf_ref[...]).astype(BF16)
    g0 = jnp.dot(h, wgu_ref[:, 0:half], preferred_element_type=F32)
    u0 = jnp.dot(h, wgu_ref[:, d_ff:d_ff + half], preferred_element_type=F32)

    probs = []
    for s in scores:
        p = jnp.exp(s - jnp.max(s, axis=-1, keepdims=True))
        probs.append((p / jnp.sum(p, axis=-1, keepdims=True)).astype(BF16))
    o_heads = [_mm(p, mv_ref[sq, :, hs(hd)]).astype(BF16) for p, (sq, hd) in zip(probs, heads)]
    o_mem = jnp.concatenate(
        [jnp.concatenate(o_heads[sq * MEM_HEADS:(sq + 1) * MEM_HEADS], axis=1)
         for sq in range(seqs)], axis=0)

    act0 = (_silu(g0) * u0).astype(BF16)
    g1 = jnp.dot(h, wgu_ref[:, half:d_ff], preferred_element_type=F32)
    u1 = jnp.dot(h, wgu_ref[:, d_ff + half:], preferred_element_type=F32)

    if gated:
        z_ref, og_ref, unperm_ref = gate_refs
        og = og_ref[...]
        parts = [(_rms(om_ref[:, hd * GDN_DV:(hd + 1) * GDN_DV], og)
                  * _silu(z_ref[:, hd * GDN_DV:(hd + 1) * GDN_DV])).astype(BF16)
                 for hd in range(GDN_HEADS)]
        mixed = jnp.concatenate(parts + [o_mem], axis=1)
        mix_w[...] = jnp.dot(unperm_ref[...], mixed, preferred_element_type=F32).astype(BF16)
    else:
        mix_w[...] = jnp.concatenate([swa.values(swa.softmax(swa_scores)), o_mem], axis=1)

    y = x + jnp.dot(act0, wd_ref[0:half, :], preferred_element_type=F32)
    act1 = (_silu(g1) * u1).astype(BF16)
    y_ref[...] = y + jnp.dot(act1, wd_ref[half:, :], preferred_element_type=F32)


def _tail_body(seqs, tiles_per_seq, hist_valid, x_ref, *refs):
    gated = hist_valid is None
    n_mixer = 4 if gated else 7
    mixer_refs, refs = refs[:n_mixer], refs[n_mixer:]
    qm_ref, mk_ref, mv_ref, mqg_ref, wo_ref, gf_ref, wgu_ref, wd_ref, y_ref = refs[:9]
    mix0_ref, mix1_ref = refs[9:11]
    i = pl.program_id(0)

    @pl.when(i == 0)
    def _():
        mix1_ref[...] = jnp.zeros_like(mix1_ref)

    if gated:
        om_ref, gate_refs, swa = mixer_refs[0], mixer_refs[1:], None
    else:
        om_ref, gate_refs = None, ()
        swa = _SwaJob(hist_valid, seqs, i % tiles_per_seq, mixer_refs, *refs[11:13])

        @pl.when(i % tiles_per_seq == 0)
        def _():
            swa.load_history()

    def step(mix_w, mix_r):
        _tail_step(seqs, gated, swa, x_ref, om_ref, gate_refs, qm_ref, mk_ref, mv_ref, mqg_ref,
                   wo_ref, gf_ref, wgu_ref, wd_ref, y_ref, mix_w, mix_r)

    @pl.when(i % 2 == 0)
    def _():
        step(mix0_ref, mix1_ref)

    @pl.when(i % 2 == 1)
    def _():
        step(mix1_ref, mix0_ref)


def _tail_call(x, gdn, swa, qm, mk, mv, mem_q_gain, w_out, g_ffn, w_gate_up, w_down, layer, tm,
               seqs):
    rows = x.shape[0]
    ns = mk.shape[1]
    n_tiles = rows // tm
    cur = lambda i: jnp.minimum(i, n_tiles - 1)
    prev = lambda i: jnp.maximum(i - 1, 0)
    tile = lambda width, which: pl.BlockSpec((tm, width), lambda i: (which(i), 0))
    tiles_per_seq = rows // (tm * ns) if seqs == 1 else 1
    per_seq = lambda *dims: pl.BlockSpec((seqs,) + dims,
                                         lambda i: (cur(i) // tiles_per_seq,) + (0,) * len(dims))
    mem = pl.BlockSpec((None, seqs, N_MEM, MEM_W),
                       lambda i: (layer, cur(i) // tiles_per_seq, 0, 0))
    if gdn:
        o_raw, z, o_gain = gdn
        hist_valid = None
        mixer_specs = [tile(GDN_VW, cur), tile(GDN_VW, cur), _const_spec((1, GDN_DV)),
                       _const_spec((tm, tm))]
        mixer_args = [o_raw, z, o_gain.reshape(1, GDN_DV), _chunk_perm(tm).T]
        scratch = []
    else:
        *mixer_args, sinks, hist_valid = swa
        dup_w = SWA_KV_HEADS * LANES
        mixer_specs = [tile(SWA_QW, cur), tile(SWA_QW, cur), tile(dup_w, cur), tile(dup_w, cur),
                       per_seq(WINDOW, dup_w), per_seq(WINDOW, dup_w),
                       pl.BlockSpec(memory_space=pltpu.SMEM)]
        mixer_args.append(sinks.reshape(1, SWA_HEADS))
        scratch = [pltpu.VMEM((seqs, WINDOW + tm // seqs, dup_w), BF16)] * 2
    return pl.pallas_call(
        functools.partial(_tail_body, seqs, tiles_per_seq, hist_valid),
        grid=(n_tiles + 1,),
        in_specs=[tile(D_MODEL, prev)] + mixer_specs + [
            tile(MEM_W, cur), mem, mem,
            _const_spec((1, MEM_HD)), _slab_spec(w_out, 0, 0, D_MODEL), _const_spec((1, D_MODEL)),
            _slab_spec(w_gate_up, layer, 0, w_gate_up.shape[2]),
            _slab_spec(w_down, layer, 0, D_MODEL),
        ],
        out_specs=tile(D_MODEL, prev),
        out_shape=jax.ShapeDtypeStruct((rows, D_MODEL), F32),
        scratch_shapes=[pltpu.VMEM((tm, w_out.shape[1]), BF16)] * 2 + scratch,
        compiler_params=_params(("arbitrary",)),
        name="tail",
    )(x, *mixer_args, qm, mk, mv, mem_q_gain.reshape(1, MEM_HD), w_out,
      g_ffn.reshape(1, D_MODEL), w_gate_up, w_down)


def _lane_row(vals, offset):
    return jnp.zeros((1, LANES), F32).at[0, offset:offset + vals.shape[0]].set(vals)


def _dup_heads(a):
    lead = a.shape[:-1]
    a = a.reshape(lead + (SWA_KV_HEADS, 1, SWA_HD))
    return jnp.broadcast_to(a, lead + (SWA_KV_HEADS, 2, SWA_HD)).reshape(lead + (SWA_KV_HEADS * LANES,))


def _undup_heads(a):
    lead = a.shape[:-1]
    return a.reshape(lead + (SWA_KV_HEADS, 2, SWA_HD))[..., 0, :]


def _trunk(x, mem_k, mem_v, gdn_conv, gdn_state, swa_k_hist, swa_v_hist, hist_valid, w):
    ns, t, _ = x.shape
    rows = ns * t
    tm = ROW_TILE
    seqs = max(1, tm // t)
    x2 = x.reshape(rows, D_MODEL)

    conv0 = jnp.pad(gdn_conv[:, :, None, :], ((0, 0), (0, 0), (SUBLANES - 1, 0), (0, 0)))
    conv0 = conv0.reshape(ns, (GDN_CONV - 1) * SUBLANES, GDN_QKV)
    z, qm, *gdn_ops, conv_new = _proj_a_call(x2, w["norm_mix"][0], w, conv0, tm, seqs)
    conv_new = conv_new[:, SUBLANES - 1::SUBLANES, :]
    o_raw, s_new = _gdn_call(gdn_ops, gdn_state, ns, min(t, GDN_TILE_ROWS))
    x2 = _tail_call(x2, (o_raw.reshape(rows, GDN_VW), z, w["o_norm_a"]), None, qm, mem_k, mem_v,
                    w["mem_q_norm"][0], w["out_a"], w["norm_ffn"][0], w["gate_up"], w["down"], 0,
                    tm, seqs)

    qlo, qhi, kd, vd, k_cache, v_cache, qm = _proj_b_call(x2, w["norm_mix"][1], w, PROJ_B_TILE,
                                                          ns > 1)
    swa = (qlo, qhi, kd, vd, _dup_heads(swa_k_hist).astype(BF16),
           _dup_heads(swa_v_hist).astype(BF16), w["sinks_b"], hist_valid)
    x2 = _tail_call(x2, None, swa, qm, mem_k, mem_v, w["mem_q_norm"][1], w["out_b"],
                    w["norm_ffn"][1], w["gate_up"], w["down"], 1, tm, seqs)

    keep = min(WINDOW, t) if ns == 1 else t
    k_new = _undup_heads(k_cache).reshape(ns, -1, SWA_KVW)[:, -keep:]
    v_new = _undup_heads(v_cache).reshape(ns, -1, SWA_KVW)[:, -keep:]
    return x2.reshape(ns, t, D_MODEL), conv_new, s_new, k_new, v_new


def kernel(x_prompt, x_sample, mem_prompt, cache_mem_k, cache_mem_v, state_gdn, state_gdn_conv, cache_swa_k, cache_swa_v, norm_mix, norm_ffn, mem_norm, w_mem_kv, mem_q_norm, mem_k_norm, w_in_a, conv_w_a, a_log, dt_bias, o_norm_a, w_out_a, w_in_b, q_norm_b, k_norm_b, sinks_b, w_out_b, w_gate_up, w_down):
    bsz = x_prompt.shape[0]
    dec = x_sample.shape[0]
    wa = w_in_a[0]
    ba_lo = GDN_QKV + GDN_VW
    qm_lo = ba_lo + 2 * GDN_HEADS
    wb = w_in_b[0]
    w = {
        "norm_mix": norm_mix, "norm_ffn": norm_ffn, "mem_q_norm": mem_q_norm,
        "in_a": w_in_a.astype(BF16), "in_b": w_in_b.astype(BF16),
        "a_ba": jnp.pad(wa[:, ba_lo:qm_lo], ((0, 0), (0, LANES - 2 * GDN_HEADS))).astype(BF16),
        "a_qm": wa[:, qm_lo:].astype(BF16),
        "b_kv": jnp.concatenate([_dup_heads(wb[:, SWA_QW:SWA_QW + SWA_KVW]),
                                 _dup_heads(wb[:, SWA_QW + SWA_KVW:SWA_QW + 2 * SWA_KVW])],
                                axis=1).astype(BF16),
        "q_gain2": jnp.tile(q_norm_b[0], 2).reshape(1, LANES),
        "k_gain2": jnp.tile(k_norm_b[0], 2).reshape(1, LANES),
        "conv_w_a": conv_w_a[0], "o_norm_a": o_norm_a[0],
        "alog_row": _lane_row(a_log[0], GDN_HEADS), "dtb_row": _lane_row(dt_bias[0], GDN_HEADS),
        "out_a": w_out_a.astype(BF16), "out_b": w_out_b.astype(BF16),
        "sinks_b": sinks_b[0],
        "gate_up": w_gate_up.astype(BF16), "down": w_down.astype(BF16),
    }

    mk, mv, mk_bf, mv_bf = _memkv_call(mem_prompt[0], mem_norm, w_mem_kv, mem_k_norm)
    depth = mk.shape[0]
    new_mem_k = mk.reshape(depth, bsz, N_MEM, MEM_HEADS, MEM_HD)
    new_mem_v = mv.reshape(depth, bsz, N_MEM, MEM_HEADS, MEM_HD)

    zero_conv = jnp.zeros((bsz, GDN_CONV - 1, GDN_QKV), F32)
    zero_state = jnp.zeros((bsz, GDN_HEADS, GDN_DK, GDN_DV), F32)
    zero_hist = jnp.zeros((bsz, WINDOW, SWA_KVW), F32)
    y_p, conv_p, state_p, k_p, v_p = _trunk(
        x_prompt, mk_bf.reshape(depth, bsz, N_MEM, MEM_W), mv_bf.reshape(depth, bsz, N_MEM, MEM_W),
        zero_conv, zero_state, zero_hist, zero_hist, False, w)

    y_s, conv_s, state_s, k_s, v_s = _trunk(
        x_sample, cache_mem_k.reshape(depth, dec, N_MEM, MEM_W).astype(BF16),
        cache_mem_v.reshape(depth, dec, N_MEM, MEM_W).astype(BF16), state_gdn_conv[0], state_gdn[0],
        cache_swa_k[0].reshape(dec, WINDOW, SWA_KVW), cache_swa_v[0].reshape(dec, WINDOW, SWA_KVW),
        True, w)

    kv_shape = lambda a: a.reshape(a.shape[0], a.shape[1], SWA_KV_HEADS, SWA_HD)[None]
    return (y_p, y_s, state_p[None], conv_p[None], state_s[None], conv_s[None],
            kv_shape(k_p), kv_shape(v_p), kv_shape(k_s), kv_shape(v_s), new_mem_k, new_mem_v)
```

```python
import functools

import jax
import jax.numpy as jnp
import numpy as np
from jax import lax
from jax.experimental import pallas as pl
from jax.experimental.pallas import tpu as pltpu

F32 = jnp.float32
BF16 = jnp.bfloat16

D_MODEL = 1024
CHUNK = 64
EPS = 1e-6
GDN_HEADS = 8
GDN_DK = 128
GDN_DV = 128
GDN_CONV = 4
GDN_QKV = GDN_HEADS * (2 * GDN_DK + GDN_DV)
GDN_VW = GDN_HEADS * GDN_DV
SWA_HEADS = 16
SWA_KV_HEADS = 4
SWA_HD = 64
SWA_GROUP = SWA_HEADS // SWA_KV_HEADS
SWA_QW = SWA_HEADS * SWA_HD
SWA_KVW = SWA_KV_HEADS * SWA_HD
WINDOW = 128
N_MEM = 256
MEM_HEADS = 4
MEM_HD = 128
MEM_W = MEM_HEADS * MEM_HD
LANES = 128
SUBLANES = 8
VMEM_LIMIT = 56 * 1024 * 1024
ROW_TILE = 256
PROJ_B_TILE = 512
SWA_TILE_ROWS = 4 * CHUNK
GDN_TILE_ROWS = 4 * CHUNK

_HI = lax.Precision.HIGHEST


def _mm(a, b):
    return jnp.dot(a.astype(BF16), b.astype(BF16), preferred_element_type=F32)


def _mm_nt(a, b):
    return lax.dot_general(a.astype(BF16), b.astype(BF16), (((1,), (1,)), ((), ())),
                           preferred_element_type=F32)


def _mm_tn(a, b):
    return lax.dot_general(a.astype(BF16), b.astype(BF16), (((0,), (0,)), ((), ())),
                           preferred_element_type=F32)


def _rms(x, g):
    return x * lax.rsqrt(jnp.mean(x * x, axis=-1, keepdims=True) + EPS) * g


def _sigmoid(x):
    return 1.0 / (1.0 + jnp.exp(-x))


def _silu(x):
    hx = 0.5 * x
    return hx * jnp.tanh(hx) + hx


def _softplus(x):
    return jnp.maximum(x, 0.0) + jnp.log1p(jnp.exp(-jnp.abs(x)))


def _const_spec(shape):
    nd = len(shape)
    return pl.BlockSpec(shape, lambda *_: (0,) * nd, pipeline_mode=pl.Buffered(1))


def _slab_spec(arr, layer, col_block, width):
    return pl.BlockSpec((None, arr.shape[1], width), lambda *_: (layer, 0, col_block),
                        pipeline_mode=pl.Buffered(1))


def _params(sem):
    return pltpu.CompilerParams(dimension_semantics=sem, vmem_limit_bytes=VMEM_LIMIT)


def _memkv_body(mem_ref, g_ref, w_ref, kg_ref, mk_ref, mv_ref, mkb_ref, mvb_ref):
    h = _rms(mem_ref[...], g_ref[...])
    kv = _mm(h, w_ref[...])
    kg = kg_ref[...]
    for hd in range(MEM_HEADS):
        sl = slice(hd * MEM_HD, (hd + 1) * MEM_HD)
        mk = _rms(kv[:, sl], kg)
        mk_ref[:, sl] = mk
        mkb_ref[:, sl] = mk.astype(BF16)
    mv_ref[...] = kv[:, MEM_W:]
    mvb_ref[...] = kv[:, MEM_W:].astype(BF16)


def _memkv_call(mem, mem_norm, w_mem_kv, mem_k_norm):
    depth = w_mem_kv.shape[0]
    out = jax.ShapeDtypeStruct((depth, N_MEM, MEM_W), F32)
    out_bf = jax.ShapeDtypeStruct((depth, N_MEM, MEM_W), BF16)
    return pl.pallas_call(
        _memkv_body,
        grid=(depth,),
        in_specs=[
            pl.BlockSpec((N_MEM, D_MODEL), lambda i: (0, 0)),
            pl.BlockSpec((None, 1, D_MODEL), lambda i: (i, 0, 0)),
            pl.BlockSpec((None, D_MODEL, 2 * MEM_W), lambda i: (i, 0, 0)),
            pl.BlockSpec((None, 1, MEM_HD), lambda i: (i, 0, 0)),
        ],
        out_specs=[pl.BlockSpec((None, N_MEM, MEM_W), lambda i: (i, 0, 0))] * 4,
        out_shape=[out, out, out_bf, out_bf],
        compiler_params=_params(("arbitrary",)),
        name="memkv",
    )(mem, mem_norm.reshape(depth, 1, D_MODEL), w_mem_kv, mem_k_norm.reshape(depth, 1, MEM_HD))


def _chunk_time(pos):
    return lax.shift_right_logical(pos, 3) + SUBLANES * (pos & (SUBLANES - 1))


def _chunk_perm(tm):
    pos = np.arange(tm)
    src = (pos // CHUNK) * CHUNK + (pos % CHUNK) // SUBLANES + SUBLANES * (pos % SUBLANES)
    return jnp.asarray(src[:, None] == pos[None, :], BF16)


def _proj_a_matmuls(x_ref, g_ref, perm_ref, wqkv_ref, wz_ref, wba_ref, wqm_ref,
                    z_ref, qm_ref, raw_ref, ba_ref):
    h = _rms(x_ref[...], g_ref[...]).astype(BF16)
    h = jnp.dot(perm_ref[...], h, preferred_element_type=F32).astype(BF16)
    z_ref[...] = jnp.dot(h, wz_ref[...], preferred_element_type=F32)
    qm_ref[...] = jnp.dot(h, wqm_ref[...], preferred_element_type=F32)
    ba_ref[...] = jnp.dot(h, wba_ref[...], preferred_element_type=F32)
    raw_ref[...] = jnp.dot(h, wqkv_ref[...], preferred_element_type=F32)


def _proj_a_rows(seqs, raw_ref, ba_ref, cw_ref, alog_ref, dtb_ref,
                 q_ref, k_ref, kb_ref, qd_ref, kd_ref, vb_ref, kbe_ref, cum_ref, cumt_ref,
                 convn_ref, hist_ref):
    c = CHUNK
    tm = raw_ref.shape[0]
    rows = tm // seqs
    n_hist = (GDN_CONV - 1) * SUBLANES

    t_row = _chunk_time(lax.broadcasted_iota(jnp.int32, (c, c), 0))
    t_col = _chunk_time(lax.broadcasted_iota(jnp.int32, (c, c), 1))
    tril = (t_row >= t_col).astype(F32)
    eye_l = (lax.broadcasted_iota(jnp.int32, (LANES, LANES), 0)
             == lax.broadcasted_iota(jnp.int32, (LANES, LANES), 1)).astype(F32)
    sub0 = lax.broadcasted_iota(jnp.int32, (SUBLANES, LANES), 0) == 0

    ba = ba_ref[...]
    beta_all = _sigmoid(ba)
    g_all = -jnp.exp(alog_ref[...]) * _softplus(ba + dtb_ref[...])

    for j in range(tm // c):
        rs = slice(j * c, (j + 1) * c)
        s, lr = (j * c) // rows, (j * c) % rows
        cum = jnp.dot(tril, g_all[rs], precision=_HI, preferred_element_type=F32)
        cum_ref[rs, :] = cum
        cumt_ref[j] = lax.dot_general(eye_l, jnp.concatenate([cum, cum], axis=0),
                                      (((1,), (1,)), ((), ())), precision=_HI,
                                      preferred_element_type=F32)
        e_cum = jnp.exp(cum)
        e_rest = jnp.exp(cum[c - 1:c, :] - cum)
        beta = beta_all[rs]

        def conv_act(lo):
            sl = slice(lo, lo + LANES)
            x = raw_ref[rs, sl]
            prev = (hist_ref[s, :, sl] if lr == 0
                    else raw_ref[j * c - n_hist:j * c, sl])
            shifted = [jnp.where(sub0,
                                 pltpu.roll(prev[i * SUBLANES:(i + 1) * SUBLANES], 1, 0),
                                 pltpu.roll(x[c - n_hist + i * SUBLANES:c - n_hist + (i + 1) * SUBLANES],
                                            1, 0)) for i in range(GDN_CONV - 1)]
            acc = x * cw_ref[GDN_CONV - 1:GDN_CONV, sl]
            for d in range(1, GDN_CONV):
                xd = jnp.concatenate(shifted[GDN_CONV - 1 - d:] + [x[0:c - d * SUBLANES]], axis=0)
                acc = acc + xd * cw_ref[GDN_CONV - 1 - d:GDN_CONV - d, sl]
            return _silu(acc)

        for hd in range(GDN_HEADS):
            hs = slice(hd * GDN_DK, (hd + 1) * GDN_DK)
            b_col = beta[:, hd:hd + 1]
            ec_col = e_cum[:, GDN_HEADS + hd:GDN_HEADS + hd + 1]
            er_col = e_rest[:, GDN_HEADS + hd:GDN_HEADS + hd + 1]
            q = conv_act(hd * GDN_DK)
            q = q * (lax.rsqrt(jnp.sum(q * q, axis=-1, keepdims=True) + EPS) * (GDN_DK ** -0.5))
            k = conv_act(GDN_HEADS * GDN_DK + hd * GDN_DK)
            k = k * lax.rsqrt(jnp.sum(k * k, axis=-1, keepdims=True) + EPS)
            v = conv_act(2 * GDN_HEADS * GDN_DK + hd * GDN_DV)
            kb = k * b_col
            q_ref[rs, hs] = q.astype(BF16)
            qd_ref[rs, hs] = (q * ec_col).astype(BF16)
            k_ref[rs, hs] = k.astype(BF16)
            kb_ref[rs, hs] = kb.astype(BF16)
            kd_ref[rs, hs] = (k * er_col).astype(BF16)
            kbe_ref[rs, hs] = kb * ec_col
            vb_ref[rs, hs] = v * b_col

    for s in range(seqs):
        tail = raw_ref[(s + 1) * rows - n_hist:(s + 1) * rows, :]
        hist_ref[s] = tail
        convn_ref[s] = tail


def _proj_a_body(seqs, tiles_per_seq, x_ref, g_ref, perm_ref, wqkv_ref, wz_ref, wba_ref, wqm_ref,
                 conv0_ref, cw_ref, alog_ref, dtb_ref,
                 z_ref, qm_ref, q_ref, k_ref, kb_ref, qd_ref, kd_ref, vb_ref, kbe_ref, cum_ref,
                 cumt_ref, convn_ref, raw0_ref, raw1_ref, ba0_ref, ba1_ref, hist_ref):
    i = pl.program_id(0)

    @pl.when(i == 0)
    def _():
        raw1_ref[...] = jnp.zeros_like(raw1_ref)
        ba1_ref[...] = jnp.zeros_like(ba1_ref)
        hist_ref[...] = jnp.zeros_like(hist_ref)

    @pl.when((i + tiles_per_seq - 1) % tiles_per_seq == 0)
    def _():
        hist_ref[...] = conv0_ref[...]

    def step(raw_w, ba_w, raw_r, ba_r):
        _proj_a_matmuls(x_ref, g_ref, perm_ref, wqkv_ref, wz_ref, wba_ref, wqm_ref,
                        z_ref, qm_ref, raw_w, ba_w)
        _proj_a_rows(seqs, raw_r, ba_r, cw_ref, alog_ref, dtb_ref,
                     q_ref, k_ref, kb_ref, qd_ref, kd_ref, vb_ref, kbe_ref, cum_ref, cumt_ref,
                     convn_ref, hist_ref)

    @pl.when(i % 2 == 0)
    def _():
        step(raw0_ref, ba0_ref, raw1_ref, ba1_ref)

    @pl.when(i % 2 == 1)
    def _():
        step(raw1_ref, ba1_ref, raw0_ref, ba0_ref)


def _proj_a_call(x, g, w, conv0, tm, seqs):
    rows = x.shape[0]
    ns = conv0.shape[0]
    n_tiles = rows // tm
    n_hist = (GDN_CONV - 1) * SUBLANES
    tiles_per_seq = rows // (ns * tm) if seqs == 1 else 1
    seq_of = (lambda i: i // tiles_per_seq) if seqs == 1 else (lambda i: i)
    cur = lambda i: jnp.minimum(i, n_tiles - 1)
    prev = lambda i: jnp.maximum(i - 1, 0)
    tile = lambda width, which: pl.BlockSpec((tm, width), lambda i: (which(i), 0))
    hist = pl.BlockSpec((seqs, n_hist, GDN_QKV), lambda i: (seq_of(prev(i)), 0, 0))
    wide_bf = jax.ShapeDtypeStruct((rows, GDN_VW), BF16)
    wide_f32 = jax.ShapeDtypeStruct((rows, GDN_VW), F32)
    perm = _chunk_perm(tm)
    return pl.pallas_call(
        functools.partial(_proj_a_body, seqs, tiles_per_seq),
        grid=(n_tiles + 1,),
        in_specs=[tile(D_MODEL, cur), _const_spec((1, D_MODEL)), _const_spec(perm.shape)]
        + [_slab_spec(w["in_a"], 0, 0, GDN_QKV), _slab_spec(w["in_a"], 0, GDN_QKV // GDN_VW, GDN_VW),
           _const_spec(w["a_ba"].shape), _const_spec(w["a_qm"].shape)]
        + [hist, _const_spec((GDN_CONV, GDN_QKV)), _const_spec((1, LANES)), _const_spec((1, LANES))],
        out_specs=[tile(GDN_VW, cur), tile(MEM_W, cur)] + [tile(GDN_VW, prev)] * 7
        + [tile(LANES, prev),
           pl.BlockSpec((tm // CHUNK, LANES, 2 * CHUNK), lambda i: (prev(i), 0, 0)), hist],
        out_shape=[wide_f32, jax.ShapeDtypeStruct((rows, MEM_W), F32)]
        + [wide_bf] * 5 + [wide_f32] * 2
        + [jax.ShapeDtypeStruct((rows, LANES), F32),
           jax.ShapeDtypeStruct((rows // CHUNK, LANES, 2 * CHUNK), F32),
           jax.ShapeDtypeStruct((ns, n_hist, GDN_QKV), F32)],
        scratch_shapes=[pltpu.VMEM((tm, GDN_QKV), F32)] * 2 + [pltpu.VMEM((tm, LANES), F32)] * 2
        + [pltpu.VMEM((seqs, n_hist, GDN_QKV), F32)],
        compiler_params=_params(("arbitrary",)),
        name="proj_a",
    )(x, g.reshape(1, D_MODEL), perm, w["in_a"], w["in_a"], w["a_ba"], w["a_qm"], conv0,
      w["conv_w_a"], w["alog_row"], w["dtb_row"])


def _gdn_chunk_solves(q_ref, k_ref, kb_ref, vb_ref, kbe_ref, cum_ref, cumt_ref, sol_ref, qk_ref):
    c = CHUNK
    nc = q_ref.shape[0] // c

    lane = lax.broadcasted_iota(jnp.int32, (c, LANES), 1)
    row = _chunk_time(lax.broadcasted_iota(jnp.int32, (c, LANES), 0))
    col = _chunk_time(lane & (c - 1))
    causal = row >= col
    strict = row > col
    low = lane < c
    eye2 = (row == col).astype(F32)

    pairs = [(j, hd) for j in range(nc) for hd in range(GDN_HEADS)]
    rs = lambda j: slice(j * c, (j + 1) * c)
    hs = lambda hd: slice(hd * GDN_DK, (hd + 1) * GDN_DK)
    gl = lambda hd: slice(GDN_HEADS + hd, GDN_HEADS + hd + 1)

    decays = [jnp.exp(jnp.where(causal, cum_ref[rs(j), gl(hd)] - cumt_ref[j, gl(hd), :], -jnp.inf))
              for j, hd in pairs]
    grams = [_mm_nt(jnp.concatenate([kb_ref[rs(j), hs(hd)], q_ref[rs(j), hs(hd)]], axis=0),
                    jnp.concatenate([k_ref[rs(j), hs(hd)]] * 2, axis=0))
             for j, hd in pairs]
    yield
    ps = [jnp.where(strict, -(g[:c] * d), 0.0) for g, d in zip(grams, decays)]
    for i, (g, d) in enumerate(zip(grams, decays)):
        qk_ref[i] = (g[c:] * d)[:, :c]

    ts = [eye2] * len(pairs)
    for _ in range(6):
        prods = [_mm(p[:, :c], jnp.where(low, p, t)) for t, p in zip(ts, ps)]
        yield
        ts = [t + r for t, r in zip(ts, prods)]
        ps = prods
    ns = [t - eye2 for t in ts]

    rhss = [jnp.concatenate([vb_ref[rs(j), hs(hd)], kbe_ref[rs(j), hs(hd)]], axis=1)
            for j, hd in pairs]
    yield
    for i, (n, rhs) in enumerate(zip(ns, rhss)):
        sol_ref[i] = rhs + _mm(n, jnp.concatenate([jnp.zeros_like(rhs, BF16), rhs.astype(BF16)],
                                                  axis=0))


def _gdn_recurrence(qd_ref, kd_ref, cum_ref, sol_ref, qk_ref, o_ref, s_ref):
    c = CHUNK
    nc = qd_ref.shape[0] // c
    heads = range(GDN_HEADS)
    rs = lambda j: slice(j * c, (j + 1) * c)
    hs = lambda hd: slice(hd * GDN_DK, (hd + 1) * GDN_DK)
    ss = [s_ref[hd] for hd in heads]
    for j in range(nc):
        sols = [sol_ref[j * GDN_HEADS + hd] for hd in heads]
        r1s = [_mm(jnp.concatenate([sols[hd][:, GDN_DV:].astype(BF16), qd_ref[rs(j), hs(hd)]],
                                   axis=0), ss[hd]) for hd in heads]
        yield
        us = [sols[hd][:, :GDN_DV] - r1s[hd][:c] for hd in heads]
        for hd in heads:
            o_ref[rs(j), hs(hd)] = r1s[hd][c:] + _mm(qk_ref[j * GDN_HEADS + hd], us[hd])
        decay = cum_ref[j * c + c - 1:j * c + c, :]
        ss = [ss[hd] * jnp.exp(decay[:, GDN_HEADS + hd:GDN_HEADS + hd + 1])
              + _mm_tn(kd_ref[rs(j), hs(hd)], us[hd]) for hd in heads]
        yield
    for hd in heads:
        s_ref[hd] = ss[hd]


def _gdn_body(skew, q_ref, k_ref, kb_ref, vb_ref, kbe_ref, cum_ref, cumt_ref, qd_ref, kd_ref,
              cumr_ref, s0_ref, o_ref, sn_ref, s_ref, sol0_ref, sol1_ref, qk0_ref, qk1_ref):
    i = pl.program_id(1)
    solve_args = (q_ref, k_ref, kb_ref, vb_ref, kbe_ref, cum_ref, cumt_ref)
    recur_args = (qd_ref, kd_ref, cumr_ref)

    if not skew:
        s_ref[...] = s0_ref[...]
        for _ in _gdn_chunk_solves(*solve_args, sol0_ref, qk0_ref):
            pass
        for _ in _gdn_recurrence(*recur_args, sol0_ref, qk0_ref, o_ref, s_ref):
            pass
        sn_ref[...] = s_ref[...]
        return

    @pl.when(i == 0)
    def _():
        sol1_ref[...] = jnp.zeros_like(sol1_ref)
        qk1_ref[...] = jnp.zeros_like(qk1_ref)
        s_ref[...] = jnp.zeros_like(s_ref)

    @pl.when(i == 1)
    def _():
        s_ref[...] = s0_ref[...]

    def step(sol_w, qk_w, sol_r, qk_r):
        solve = _gdn_chunk_solves(*solve_args, sol_w, qk_w)
        recur = _gdn_recurrence(*recur_args, sol_r, qk_r, o_ref, s_ref)
        live = [solve, recur]
        while live:
            for gen in list(live):
                if next(gen, live) is live:
                    live.remove(gen)
        sn_ref[...] = s_ref[...]

    @pl.when(i % 2 == 0)
    def _():
        step(sol0_ref, qk0_ref, sol1_ref, qk1_ref)

    @pl.when(i % 2 == 1)
    def _():
        step(sol1_ref, qk1_ref, sol0_ref, qk0_ref)


def _gdn_call(ops, s0, ns, tr):
    q, k, kb, qd, kd, vb, kbe, cum, cumt = ops
    t = q.shape[0] // ns
    n_tiles = t // tr
    skew = n_tiles > 1
    cur = (lambda i: jnp.minimum(i, n_tiles - 1)) if skew else (lambda i: i)
    prev = (lambda i: jnp.maximum(i - 1, 0)) if skew else (lambda i: i)
    seq = lambda width, which: pl.BlockSpec((None, tr, width), lambda s, i: (s, which(i), 0))
    state = pl.BlockSpec((None, GDN_HEADS, GDN_DK, GDN_DV), lambda s, i: (s, 0, 0, 0))
    rows3 = lambda a: a.reshape(ns, t, a.shape[-1])
    units = (tr // CHUNK) * GDN_HEADS
    return pl.pallas_call(
        functools.partial(_gdn_body, skew),
        grid=(ns, n_tiles + 1 if skew else n_tiles),
        in_specs=[seq(GDN_VW, cur)] * 5
        + [seq(LANES, cur),
           pl.BlockSpec((None, tr // CHUNK, LANES, 2 * CHUNK), lambda s, i: (s, cur(i), 0, 0)),
           seq(GDN_VW, prev), seq(GDN_VW, prev), seq(LANES, prev), state],
        out_specs=[seq(GDN_VW, prev), state],
        out_shape=[jax.ShapeDtypeStruct((ns, t, GDN_VW), F32),
                   jax.ShapeDtypeStruct((ns, GDN_HEADS, GDN_DK, GDN_DV), F32)],
        scratch_shapes=[pltpu.VMEM((GDN_HEADS, GDN_DK, GDN_DV), F32)]
        + [pltpu.VMEM((units, CHUNK, GDN_DV + GDN_DK), F32)] * 2
        + [pltpu.VMEM((units, CHUNK, CHUNK), F32)] * 2,
        compiler_params=_params(("arbitrary", "arbitrary")),
        name="gdn",
    )(rows3(q), rows3(k), rows3(kb), rows3(vb), rows3(kbe), rows3(cum),
      cumt.reshape(ns, t // CHUNK, LANES, 2 * CHUNK), rows3(qd), rows3(kd), rows3(cum), s0)


def _proj_b_body(x_ref, g_ref, wq_ref, wkv_ref, wqm_ref, qg_ref, kg_ref,
                 qlo_ref, qhi_ref, kd_ref, vd_ref, kc_ref, vc_ref, qm_ref):
    h = _rms(x_ref[...], g_ref[...]).astype(BF16)
    qm_ref[...] = jnp.dot(h, wqm_ref[...], preferred_element_type=F32)
    q = jnp.dot(h, wq_ref[...], preferred_element_type=F32)
    kv = jnp.dot(h, wkv_ref[...], preferred_element_type=F32)
    lo = lax.broadcasted_iota(jnp.int32, (1, LANES), 1) < SWA_HD
    qg = qg_ref[...] * (SWA_HD ** -0.5)
    for p in range(SWA_QW // LANES):
        sl = slice(p * LANES, (p + 1) * LANES)
        x = q[:, sl]
        x2 = x * x
        m_lo = jnp.sum(jnp.where(lo, x2, 0.0), axis=-1, keepdims=True) * (1.0 / SWA_HD)
        m_hi = jnp.sum(jnp.where(lo, 0.0, x2), axis=-1, keepdims=True) * (1.0 / SWA_HD)
        qn = x * jnp.where(lo, lax.rsqrt(m_lo + EPS), lax.rsqrt(m_hi + EPS)) * qg
        qlo_ref[:, sl] = jnp.where(lo, qn, 0.0).astype(BF16)
        qhi_ref[:, sl] = jnp.where(lo, 0.0, qn).astype(BF16)
    kg = kg_ref[...]
    for p in range(SWA_KV_HEADS):
        sl = slice(p * LANES, (p + 1) * LANES)
        kn = _rms(kv[:, sl], kg)
        kc_ref[:, sl] = kn
        kd_ref[:, sl] = kn.astype(BF16)
    v = kv[:, SWA_KV_HEADS * LANES:]
    vc_ref[...] = v
    vd_ref[...] = v.astype(BF16)


def _proj_b_call(x, g, w, tm, keep_all):
    rows = x.shape[0]
    dup_w = SWA_KV_HEADS * LANES
    tile = lambda width: pl.BlockSpec((tm, width), lambda i: (i, 0))
    cache = tile(dup_w) if keep_all else pl.BlockSpec((tm, dup_w), lambda i: (0, 0))
    cache_shape = jax.ShapeDtypeStruct((rows if keep_all else tm, dup_w), F32)
    return pl.pallas_call(
        _proj_b_body,
        grid=(rows // tm,),
        in_specs=[tile(D_MODEL), _const_spec((1, D_MODEL))]
        + [_slab_spec(w["in_b"], 0, 0, SWA_QW), _const_spec(w["b_kv"].shape),
           _slab_spec(w["in_b"], 0, (SWA_QW + 2 * SWA_KVW) // MEM_W, MEM_W)]
        + [_const_spec((1, LANES))] * 2,
        out_specs=[tile(SWA_QW)] * 2 + [tile(dup_w)] * 2 + [cache] * 2 + [tile(MEM_W)],
        out_shape=[jax.ShapeDtypeStruct((rows, SWA_QW), BF16)] * 2
        + [jax.ShapeDtypeStruct((rows, dup_w), BF16)] * 2 + [cache_shape] * 2
        + [jax.ShapeDtypeStruct((rows, MEM_W), F32)],
        compiler_params=_params(("arbitrary",)),
        name="proj_b",
    )(x, g.reshape(1, D_MODEL), w["in_b"], w["b_kv"], w["in_b"], w["q_gain2"], w["k_gain2"])


class _SwaJob:
    def __init__(self, hist_valid, seqs, tile_in_seq, refs, kwin_ref, vwin_ref):
        (self.qlo_ref, self.qhi_ref, self.k_ref, self.v_ref, self.kh_ref, self.vh_ref,
         self.sink_ref) = refs
        self.hist_valid, self.seqs, self.tile_in_seq = hist_valid, seqs, tile_in_seq
        self.kwin_ref, self.vwin_ref = kwin_ref, vwin_ref
        self.tm = self.qlo_ref.shape[0]
        self.rows = self.tm // seqs
        self.units = [(sq, j, h) for sq in range(seqs) for j in range(self.rows // CHUNK)
                      for h in range(SWA_KV_HEADS)]

    def load_history(self):
        self.kwin_ref[:, 0:WINDOW, :] = self.kh_ref[...]
        self.vwin_ref[:, 0:WINDOW, :] = self.vh_ref[...]

    def scores(self):
        c, rows = CHUNK, self.rows
        slab = lambda p: slice(p * LANES, (p + 1) * LANES)
        for sq in range(self.seqs):
            self.kwin_ref[sq, WINDOW:WINDOW + rows, :] = self.k_ref[sq * rows:(sq + 1) * rows, :]
            self.vwin_ref[sq, WINDOW:WINDOW + rows, :] = self.v_ref[sq * rows:(sq + 1) * rows, :]
        out = []
        for sq, j, h in self.units:
            r = slice(sq * rows + j * c, sq * rows + (j + 1) * c)
            qs = jnp.concatenate([self.qlo_ref[r, slab(2 * h)], self.qhi_ref[r, slab(2 * h)],
                                  self.qlo_ref[r, slab(2 * h + 1)], self.qhi_ref[r, slab(2 * h + 1)]],
                                 axis=0)
            s = _mm_nt(qs, self.kwin_ref[sq, j * c:j * c + WINDOW + c, slab(h)])
            if not self.hist_valid and j * c < WINDOW:
                key_col = lax.broadcasted_iota(jnp.int32, (SWA_GROUP * c, WINDOW + c), 1)
                first_key = self.tile_in_seq * rows + j * c - WINDOW
                s = jnp.where(key_col + first_key >= 0, s, -jnp.inf)
            out.append(s)
        return out

    def softmax(self, scores):
        c = CHUNK
        sinks = [jnp.concatenate([jnp.full((c, 1), self.sink_ref[0, h * SWA_GROUP + g], F32)
                                  for g in range(SWA_GROUP)], axis=0)
                 for h in range(SWA_KV_HEADS)]
        out = []
        for s, (sq, j, h) in zip(scores, self.units):
            m = jnp.maximum(jnp.max(s, axis=-1, keepdims=True), sinks[h])
            p = jnp.exp(s - m)
            out.append((p / (jnp.sum(p, axis=-1, keepdims=True) + jnp.exp(sinks[h] - m)))
                       .astype(BF16))
        return out

    def values(self, probs):
        c, rows = CHUNK, self.rows
        slab = lambda p: slice(p * LANES, (p + 1) * LANES)
        lo = lax.broadcasted_iota(jnp.int32, (1, LANES), 1) < SWA_HD
        zero = jnp.zeros((), BF16)
        slabs = {}
        for p, (sq, j, h) in zip(probs, self.units):
            v = self.vwin_ref[sq, j * c:j * c + WINDOW + c, slab(h)]
            v_lo = jnp.where(lo, v, zero)
            v_hi = jnp.where(lo, zero, v)
            for half in range(2):
                slabs[sq, j, 2 * h + half] = (
                    _mm(p[(2 * half) * c:(2 * half + 1) * c], v_lo)
                    + _mm(p[(2 * half + 1) * c:(2 * half + 2) * c], v_hi)).astype(BF16)
        for ref in (self.kwin_ref, self.vwin_ref):
            ref[:, 0:WINDOW, :] = ref[:, rows:rows + WINDOW, :]
        return jnp.concatenate(
            [jnp.concatenate([slabs[sq, j, p] for p in range(SWA_QW // LANES)], axis=1)
             for sq in range(self.seqs) for j in range(rows // c)], axis=0)


def _tail_step(seqs, gated, swa, x_ref, om_ref, gate_refs, qm_ref, mk_ref, mv_ref, mqg_ref, wo_ref,
               gf_ref, wgu_ref, wd_ref, y_ref, mix_w, mix_r):
    tm = x_ref.shape[0]
    rows = tm // seqs
    d_ff = wd_ref.shape[0]
    half = d_ff // 2
    heads = [(sq, hd) for sq in range(seqs) for hd in range(MEM_HEADS)]
    rs = lambda sq: slice(sq * rows, (sq + 1) * rows)
    hs = lambda hd: slice(hd * MEM_HD, (hd + 1) * MEM_HD)

    x = x_ref[...] + jnp.dot(mix_r[...], wo_ref[...], preferred_element_type=F32)

    if swa:
        swa_scores = swa.scores()
    mqg = mqg_ref[...] * (MEM_HD ** -0.5)
    scores = [_mm_nt(_rms(qm_ref[rs(sq), hs(hd)], mqg), mk_ref[sq, :, hs(hd)])
              for sq, hd in heads]

    h = _rms(x, gf_ref[...]).astype(BF16)
    g0 = jnp.dot(h, wgu_ref[:, 0:half], preferred_element_type=F32)
    u0 = jnp.dot(h, wgu_ref[:, d_ff:d_ff + half], preferred_element_type=F32)

    probs = []
    for s in scores:
        p = jnp.exp(s - jnp.max(s, axis=-1, keepdims=True))
        probs.append((p / jnp.sum(p, axis=-1, keepdims=True)).astype(BF16))
    o_heads = [_mm(p, mv_ref[sq, :, hs(hd)]).astype(BF16) for p, (sq, hd) in zip(probs, heads)]
    o_mem = jnp.concatenate(
        [jnp.concatenate(o_heads[sq * MEM_HEADS:(sq + 1) * MEM_HEADS], axis=1)
         for sq in range(seqs)], axis=0)

    act0 = (_silu(g0) * u0).astype(BF16)
    g1 = jnp.dot(h, wgu_ref[:, half:d_ff], preferred_element_type=F32)
    u1 = jnp.dot(h, wgu_ref[:, d_ff + half:], preferred_element_type=F32)

    if gated:
        z_ref, og_ref, unperm_ref = gate_refs
        og = og_ref[...]
        parts = [(_rms(om_ref[:, hd * GDN_DV:(hd + 1) * GDN_DV], og)
                  * _silu(z_ref[:, hd * GDN_DV:(hd + 1) * GDN_DV])).astype(BF16)
                 for hd in range(GDN_HEADS)]
        mixed = jnp.concatenate(parts + [o_mem], axis=1)
        mix_w[...] = jnp.dot(unperm_ref[...], mixed, preferred_element_type=F32).astype(BF16)
    else:
        mix_w[...] = jnp.concatenate([swa.values(swa.softmax(swa_scores)), o_mem], axis=1)

    y = x + jnp.dot(act0, wd_ref[0:half, :], preferred_element_type=F32)
    act1 = (_silu(g1) * u1).astype(BF16)
    y_ref[...] = y + jnp.dot(act1, wd_ref[half:, :], preferred_element_type=F32)


def _tail_body(seqs, tiles_per_seq, hist_valid, x_ref, *refs):
    gated = hist_valid is None
    n_mixer = 4 if gated else 7
    mixer_refs, refs = refs[:n_mixer], refs[n_mixer:]
    qm_ref, mk_ref, mv_ref, mqg_ref, wo_ref, gf_ref, wgu_ref, wd_ref, y_ref = refs[:9]
    mix0_ref, mix1_ref = refs[9:11]
    i = pl.program_id(0)

    @pl.when(i == 0)
    def _():
        mix1_ref[...] = jnp.zeros_like(mix1_ref)

    if gated:
        om_ref, gate_refs, swa = mixer_refs[0], mixer_refs[1:], None
    else:
        om_ref, gate_refs = None, ()
        swa = _SwaJob(hist_valid, seqs, i % tiles_per_seq, mixer_refs, *refs[11:13])

        @pl.when(i % tiles_per_seq == 0)
        def _():
            swa.load_history()

    def step(mix_w, mix_r):
        _tail_step(seqs, gated, swa, x_ref, om_ref, gate_refs, qm_ref, mk_ref, mv_ref, mqg_ref,
                   wo_ref, gf_ref, wgu_ref, wd_ref, y_ref, mix_w, mix_r)

    @pl.when(i % 2 == 0)
    def _():
        step(mix0_ref, mix1_ref)

    @pl.when(i % 2 == 1)
    def _():
        step(mix1_ref, mix0_ref)


def _tail_call(x, gdn, swa, qm, mk, mv, mem_q_gain, w_out, g_ffn, w_gate_up, w_down, layer, tm,
               seqs):
    rows = x.shape[0]
    ns = mk.shape[1]
    n_tiles = rows // tm
    cur = lambda i: jnp.minimum(i, n_tiles - 1)
    prev = lambda i: jnp.maximum(i - 1, 0)
    tile = lambda width, which: pl.BlockSpec((tm, width), lambda i: (which(i), 0))
    tiles_per_seq = rows // (tm * ns) if seqs == 1 else 1
    per_seq = lambda *dims: pl.BlockSpec((seqs,) + dims,
                                         lambda i: (cur(i) // tiles_per_seq,) + (0,) * len(dims))
    mem = pl.BlockSpec((None, seqs, N_MEM, MEM_W),
                       lambda i: (layer, cur(i) // tiles_per_seq, 0, 0))
    if gdn:
        o_raw, z, o_gain = gdn
        hist_valid = None
        mixer_specs = [tile(GDN_VW, cur), tile(GDN_VW, cur), _const_spec((1, GDN_DV)),
                       _const_spec((tm, tm))]
        mixer_args = [o_raw, z, o_gain.reshape(1, GDN_DV), _chunk_perm(tm).T]
        scratch = []
    else:
        *mixer_args, sinks, hist_valid = swa
        dup_w = SWA_KV_HEADS * LANES
        mixer_specs = [tile(SWA_QW, cur), tile(SWA_QW, cur), tile(dup_w, cur), tile(dup_w, cur),
                       per_seq(WINDOW, dup_w), per_seq(WINDOW, dup_w),
                       pl.BlockSpec(memory_space=pltpu.SMEM)]
        mixer_args.append(sinks.reshape(1, SWA_HEADS))
        scratch = [pltpu.VMEM((seqs, WINDOW + tm // seqs, dup_w), BF16)] * 2
    return pl.pallas_call(
        functools.partial(_tail_body, seqs, tiles_per_seq, hist_valid),
        grid=(n_tiles + 1,),
        in_specs=[tile(D_MODEL, prev)] + mixer_specs + [
            tile(MEM_W, cur), mem, mem,
            _const_spec((1, MEM_HD)), _slab_spec(w_out, 0, 0, D_MODEL), _const_spec((1, D_MODEL)),
            _slab_spec(w_gate_up, layer, 0, w_gate_up.shape[2]),
            _slab_spec(w_down, layer, 0, D_MODEL),
        ],
        out_specs=tile(D_MODEL, prev),
        out_shape=jax.ShapeDtypeStruct((rows, D_MODEL), F32),
        scratch_shapes=[pltpu.VMEM((tm, w_out.shape[1]), BF16)] * 2 + scratch,
        compiler_params=_params(("arbitrary",)),
        name="tail",
    )(x, *mixer_args, qm, mk, mv, mem_q_gain.reshape(1, MEM_HD), w_out,
      g_ffn.reshape(1, D_MODEL), w_gate_up, w_down)


def _lane_row(vals, offset):
    return jnp.zeros((1, LANES), F32).at[0, offset:offset + vals.shape[0]].set(vals)


def _dup_heads(a):
    lead = a.shape[:-1]
    a = a.reshape(lead + (SWA_KV_HEADS, 1, SWA_HD))
    return jnp.broadcast_to(a, lead + (SWA_KV_HEADS, 2, SWA_HD)).reshape(lead + (SWA_KV_HEADS * LANES,))


def _undup_heads(a):
    lead = a.shape[:-1]
    return a.reshape(lead + (SWA_KV_HEADS, 2, SWA_HD))[..., 0, :]


def _trunk(x, mem_k, mem_v, gdn_conv, gdn_state, swa_k_hist, swa_v_hist, hist_valid, w):
    ns, t, _ = x.shape
    rows = ns * t
    tm = ROW_TILE
    seqs = max(1, tm // t)
    x2 = x.reshape(rows, D_MODEL)

    conv0 = jnp.pad(gdn_conv[:, :, None, :], ((0, 0), (0, 0), (SUBLANES - 1, 0), (0, 0)))
    conv0 = conv0.reshape(ns, (GDN_CONV - 1) * SUBLANES, GDN_QKV)
    z, qm, *gdn_ops, conv_new = _proj_a_call(x2, w["norm_mix"][0], w, conv0, tm, seqs)
    conv_new = conv_new[:, SUBLANES - 1::SUBLANES, :]
    o_raw, s_new = _gdn_call(gdn_ops, gdn_state, ns, min(t, GDN_TILE_ROWS))
    x2 = _tail_call(x2, (o_raw.reshape(rows, GDN_VW), z, w["o_norm_a"]), None, qm, mem_k, mem_v,
                    w["mem_q_norm"][0], w["out_a"], w["norm_ffn"][0], w["gate_up"], w["down"], 0,
                    tm, seqs)

    qlo, qhi, kd, vd, k_cache, v_cache, qm = _proj_b_call(x2, w["norm_mix"][1], w, PROJ_B_TILE,
                                                          ns > 1)
    swa = (qlo, qhi, kd, vd, _dup_heads(swa_k_hist).astype(BF16),
           _dup_heads(swa_v_hist).astype(BF16), w["sinks_b"], hist_valid)
    x2 = _tail_call(x2, None, swa, qm, mem_k, mem_v, w["mem_q_norm"][1], w["out_b"],
                    w["norm_ffn"][1], w["gate_up"], w["down"], 1, tm, seqs)

    keep = min(WINDOW, t) if ns == 1 else t
    k_new = _undup_heads(k_cache).reshape(ns, -1, SWA_KVW)[:, -keep:]
    v_new = _undup_heads(v_cache).reshape(ns, -1, SWA_KVW)[:, -keep:]
    return x2.reshape(ns, t, D_MODEL), conv_new, s_new, k_new, v_new


def kernel(x_prompt, x_sample, mem_prompt, cache_mem_k, cache_mem_v, state_gdn, state_gdn_conv, cache_swa_k, cache_swa_v, norm_mix, norm_ffn, mem_norm, w_mem_kv, mem_q_norm, mem_k_norm, w_in_a, conv_w_a, a_log, dt_bias, o_norm_a, w_out_a, w_in_b, q_norm_b, k_norm_b, sinks_b, w_out_b, w_gate_up, w_down):
    bsz = x_prompt.shape[0]
    dec = x_sample.shape[0]
    wa = w_in_a[0]
    ba_lo = GDN_QKV + GDN_VW
    qm_lo = ba_lo + 2 * GDN_HEADS
    wb = w_in_b[0]
    w = {
        "norm_mix": norm_mix, "norm_ffn": norm_ffn, "mem_q_norm": mem_q_norm,
        "in_a": w_in_a.astype(BF16), "in_b": w_in_b.astype(BF16),
        "a_ba": jnp.pad(wa[:, ba_lo:qm_lo], ((0, 0), (0, LANES - 2 * GDN_HEADS))).astype(BF16),
        "a_qm": wa[:, qm_lo:].astype(BF16),
        "b_kv": jnp.concatenate([_dup_heads(wb[:, SWA_QW:SWA_QW + SWA_KVW]),
                                 _dup_heads(wb[:, SWA_QW + SWA_KVW:SWA_QW + 2 * SWA_KVW])],
                                axis=1).astype(BF16),
        "q_gain2": jnp.tile(q_norm_b[0], 2).reshape(1, LANES),
        "k_gain2": jnp.tile(k_norm_b[0], 2).reshape(1, LANES),
        "conv_w_a": conv_w_a[0], "o_norm_a": o_norm_a[0],
        "alog_row": _lane_row(a_log[0], GDN_HEADS), "dtb_row": _lane_row(dt_bias[0], GDN_HEADS),
        "out_a": w_out_a.astype(BF16), "out_b": w_out_b.astype(BF16),
        "sinks_b": sinks_b[0],
        "gate_up": w_gate_up.astype(BF16), "down": w_down.astype(BF16),
    }

    mk, mv, mk_bf, mv_bf = _memkv_call(mem_prompt[0], mem_norm, w_mem_kv, mem_k_norm)
    depth = mk.shape[0]
    new_mem_k = mk.reshape(depth, bsz, N_MEM, MEM_HEADS, MEM_HD)
    new_mem_v = mv.reshape(depth, bsz, N_MEM, MEM_HEADS, MEM_HD)

    zero_conv = jnp.zeros((bsz, GDN_CONV - 1, GDN_QKV), F32)
    zero_state = jnp.zeros((bsz, GDN_HEADS, GDN_DK, GDN_DV), F32)
    zero_hist = jnp.zeros((bsz, WINDOW, SWA_KVW), F32)
    y_p, conv_p, state_p, k_p, v_p = _trunk(
        x_prompt, mk_bf.reshape(depth, bsz, N_MEM, MEM_W), mv_bf.reshape(depth, bsz, N_MEM, MEM_W),
        zero_conv, zero_state, zero_hist, zero_hist, False, w)

    y_s, conv_s, state_s, k_s, v_s = _trunk(
        x_sample, cache_mem_k.reshape(depth, dec, N_MEM, MEM_W).astype(BF16),
        cache_mem_v.reshape(depth, dec, N_MEM, MEM_W).astype(BF16), state_gdn_conv[0], state_gdn[0],
        cache_swa_k[0].reshape(dec, WINDOW, SWA_KVW), cache_swa_v[0].reshape(dec, WINDOW, SWA_KVW),
        True, w)

    kv_shape = lambda a: a.reshape(a.shape[0], a.shape[1], SWA_KV_HEADS, SWA_HD)[None]
    return (y_p, y_s, state_p[None], conv_p[None], state_s[None], conv_s[None],
            kv_shape(k_p), kv_shape(v_p), kv_shape(k_s), kv_shape(v_s), new_mem_k, new_mem_v)
```

```python
import functools

import jax
import jax.numpy as jnp
import numpy as np
from jax import lax
from jax.experimental import pallas as pl
from jax.experimental.pallas import tpu as pltpu

F32 = jnp.float32
BF16 = jnp.bfloat16

D_MODEL = 1024
CHUNK = 64
EPS = 1e-6
GDN_HEADS = 8
GDN_DK = 128
GDN_DV = 128
GDN_CONV = 4
GDN_QKV = GDN_HEADS * (2 * GDN_DK + GDN_DV)
GDN_VW = GDN_HEADS * GDN_DV
SWA_HEADS = 16
SWA_KV_HEADS = 4
SWA_HD = 64
SWA_GROUP = SWA_HEADS // SWA_KV_HEADS
SWA_QW = SWA_HEADS * SWA_HD
SWA_KVW = SWA_KV_HEADS * SWA_HD
WINDOW = 128
N_MEM = 256
MEM_HEADS = 4
MEM_HD = 128
MEM_W = MEM_HEADS * MEM_HD
LANES = 128
SUBLANES = 8
VMEM_LIMIT = 56 * 1024 * 1024
ROW_TILE = 256
PROJ_B_TILE = 512
SWA_TILE_ROWS = 4 * CHUNK
GDN_TILE_ROWS = 4 * CHUNK

_HI = lax.Precision.HIGHEST


def _mm(a, b):
    return jnp.dot(a.astype(BF16), b.astype(BF16), preferred_element_type=F32)


def _mm_nt(a, b):
    return lax.dot_general(a.astype(BF16), b.astype(BF16), (((1,), (1,)), ((), ())),
                           preferred_element_type=F32)


def _mm_tn(a, b):
    return lax.dot_general(a.astype(BF16), b.astype(BF16), (((0,), (0,)), ((), ())),
                           preferred_element_type=F32)


def _rms(x, g):
    return x * lax.rsqrt(jnp.mean(x * x, axis=-1, keepdims=True) + EPS) * g


def _sigmoid(x):
    return 1.0 / (1.0 + jnp.exp(-x))


def _silu(x):
    hx = 0.5 * x
    return hx * jnp.tanh(hx) + hx


def _softplus(x):
    return jnp.maximum(x, 0.0) + jnp.log1p(jnp.exp(-jnp.abs(x)))


def _const_spec(shape):
    nd = len(shape)
    return pl.BlockSpec(shape, lambda *_: (0,) * nd, pipeline_mode=pl.Buffered(1))


def _slab_spec(arr, layer, col_block, width):
    return pl.BlockSpec((None, arr.shape[1], width), lambda *_: (layer, 0, col_block),
                        pipeline_mode=pl.Buffered(1))


def _params(sem):
    return pltpu.CompilerParams(dimension_semantics=sem, vmem_limit_bytes=VMEM_LIMIT)


def _memkv_body(mem_ref, g_ref, w_ref, kg_ref, mk_ref, mv_ref, mkb_ref, mvb_ref):
    h = _rms(mem_ref[...], g_ref[...])
    kv = _mm(h, w_ref[...])
    kg = kg_ref[...]
    for hd in range(MEM_HEADS):
        sl = slice(hd * MEM_HD, (hd + 1) * MEM_HD)
        mk = _rms(kv[:, sl], kg)
        mk_ref[:, sl] = mk
        mkb_ref[:, sl] = mk.astype(BF16)
    mv_ref[...] = kv[:, MEM_W:]
    mvb_ref[...] = kv[:, MEM_W:].astype(BF16)


def _memkv_call(mem, mem_norm, w_mem_kv, mem_k_norm):
    depth = w_mem_kv.shape[0]
    out = jax.ShapeDtypeStruct((depth, N_MEM, MEM_W), F32)
    out_bf = jax.ShapeDtypeStruct((depth, N_MEM, MEM_W), BF16)
    return pl.pallas_call(
        _memkv_body,
        grid=(depth,),
        in_specs=[
            pl.BlockSpec((N_MEM, D_MODEL), lambda i: (0, 0)),
            pl.BlockSpec((None, 1, D_MODEL), lambda i: (i, 0, 0)),
            pl.BlockSpec((None, D_MODEL, 2 * MEM_W), lambda i: (i, 0, 0)),
            pl.BlockSpec((None, 1, MEM_HD), lambda i: (i, 0, 0)),
        ],
        out_specs=[pl.BlockSpec((None, N_MEM, MEM_W), lambda i: (i, 0, 0))] * 4,
        out_shape=[out, out, out_bf, out_bf],
        compiler_params=_params(("arbitrary",)),
        name="memkv",
    )(mem, mem_norm.reshape(depth, 1, D_MODEL), w_mem_kv, mem_k_norm.reshape(depth, 1, MEM_HD))


def _chunk_time(pos):
    return lax.shift_right_logical(pos, 3) + SUBLANES * (pos & (SUBLANES - 1))


def _chunk_perm(tm):
    pos = np.arange(tm)
    src = (pos // CHUNK) * CHUNK + (pos % CHUNK) // SUBLANES + SUBLANES * (pos % SUBLANES)
    return jnp.asarray(src[:, None] == pos[None, :], BF16)


def _proj_a_matmuls(x_ref, g_ref, perm_ref, wqkv_ref, wz_ref, wba_ref, wqm_ref,
                    z_ref, qm_ref, raw_ref, ba_ref):
    h = _rms(x_ref[...], g_ref[...]).astype(BF16)
    h = jnp.dot(perm_ref[...], h, preferred_element_type=F32).astype(BF16)
    z_ref[...] = jnp.dot(h, wz_ref[...], preferred_element_type=F32)
    qm_ref[...] = jnp.dot(h, wqm_ref[...], preferred_element_type=F32)
    ba_ref[...] = jnp.dot(h, wba_ref[...], preferred_element_type=F32)
    raw_ref[...] = jnp.dot(h, wqkv_ref[...], preferred_element_type=F32)


def _proj_a_rows(seqs, raw_ref, ba_ref, cw_ref, alog_ref, dtb_ref,
                 q_ref, k_ref, kb_ref, qd_ref, kd_ref, vb_ref, kbe_ref, cum_ref, cumt_ref,
                 convn_ref, hist_ref):
    c = CHUNK
    tm = raw_ref.shape[0]
    rows = tm // seqs
    n_hist = (GDN_CONV - 1) * SUBLANES

    t_row = _chunk_time(lax.broadcasted_iota(jnp.int32, (c, c), 0))
    t_col = _chunk_time(lax.broadcasted_iota(jnp.int32, (c, c), 1))
    tril = (t_row >= t_col).astype(F32)
    eye_l = (lax.broadcasted_iota(jnp.int32, (LANES, LANES), 0)
             == lax.broadcasted_iota(jnp.int32, (LANES, LANES), 1)).astype(F32)
    sub0 = lax.broadcasted_iota(jnp.int32, (SUBLANES, LANES), 0) == 0

    ba = ba_ref[...]
    beta_all = _sigmoid(ba)
    g_all = -jnp.exp(alog_ref[...]) * _softplus(ba + dtb_ref[...])

    for j in range(tm // c):
        rs = slice(j * c, (j + 1) * c)
        s, lr = (j * c) // rows, (j * c) % rows
        cum = jnp.dot(tril, g_all[rs], precision=_HI, preferred_element_type=F32)
        cum_ref[rs, :] = cum
        cumt_ref[j] = lax.dot_general(eye_l, jnp.concatenate([cum, cum], axis=0),
                                      (((1,), (1,)), ((), ())), precision=_HI,
                                      preferred_element_type=F32)
        e_cum = jnp.exp(cum)
        e_rest = jnp.exp(cum[c - 1:c, :] - cum)
        beta = beta_all[rs]

        def conv_act(lo):
            sl = slice(lo, lo + LANES)
            x = raw_ref[rs, sl]
            prev = (hist_ref[s, :, sl] if lr == 0
                    else raw_ref[j * c - n_hist:j * c, sl])
            shifted = [jnp.where(sub0,
                                 pltpu.roll(prev[i * SUBLANES:(i + 1) * SUBLANES], 1, 0),
                                 pltpu.roll(x[c - n_hist + i * SUBLANES:c - n_hist + (i + 1) * SUBLANES],
                                            1, 0)) for i in range(GDN_CONV - 1)]
            acc = x * cw_ref[GDN_CONV - 1:GDN_CONV, sl]
            for d in range(1, GDN_CONV):
                xd = jnp.concatenate(shifted[GDN_CONV - 1 - d:] + [x[0:c - d * SUBLANES]], axis=0)
                acc = acc + xd * cw_ref[GDN_CONV - 1 - d:GDN_CONV - d, sl]
            return _silu(acc)

        for hd in range(GDN_HEADS):
            hs = slice(hd * GDN_DK, (hd + 1) * GDN_DK)
            b_col = beta[:, hd:hd + 1]
            ec_col = e_cum[:, GDN_HEADS + hd:GDN_HEADS + hd + 1]
            er_col = e_rest[:, GDN_HEADS + hd:GDN_HEADS + hd + 1]
            q = conv_act(hd * GDN_DK)
            q = q * (lax.rsqrt(jnp.sum(q * q, axis=-1, keepdims=True) + EPS) * (GDN_DK ** -0.5))
            k = conv_act(GDN_HEADS * GDN_DK + hd * GDN_DK)
            k = k * lax.rsqrt(jnp.sum(k * k, axis=-1, keepdims=True) + EPS)
            v = conv_act(2 * GDN_HEADS * GDN_DK + hd * GDN_DV)
            kb = k * b_col
            q_ref[rs, hs] = q.astype(BF16)
            qd_ref[rs, hs] = (q * ec_col).astype(BF16)
            k_ref[rs, hs] = k.astype(BF16)
            kb_ref[rs, hs] = kb.astype(BF16)
            kd_ref[rs, hs] = (k * er_col).astype(BF16)
            kbe_ref[rs, hs] = kb * ec_col
            vb_ref[rs, hs] = v * b_col

    for s in range(seqs):
        tail = raw_ref[(s + 1) * rows - n_hist:(s + 1) * rows, :]
        hist_ref[s] = tail
        convn_ref[s] = tail


def _proj_a_body(seqs, tiles_per_seq, x_ref, g_ref, perm_ref, wqkv_ref, wz_ref, wba_ref, wqm_ref,
                 conv0_ref, cw_ref, alog_ref, dtb_ref,
                 z_ref, qm_ref, q_ref, k_ref, kb_ref, qd_ref, kd_ref, vb_ref, kbe_ref, cum_ref,
                 cumt_ref, convn_ref, raw0_ref, raw1_ref, ba0_ref, ba1_ref, hist_ref):
    i = pl.program_id(0)

    @pl.when(i == 0)
    def _():
        raw1_ref[...] = jnp.zeros_like(raw1_ref)
        ba1_ref[...] = jnp.zeros_like(ba1_ref)
        hist_ref[...] = jnp.zeros_like(hist_ref)

    @pl.when((i + tiles_per_seq - 1) % tiles_per_seq == 0)
    def _():
        hist_ref[...] = conv0_ref[...]

    def step(raw_w, ba_w, raw_r, ba_r):
        _proj_a_matmuls(x_ref, g_ref, perm_ref, wqkv_ref, wz_ref, wba_ref, wqm_ref,
                        z_ref, qm_ref, raw_w, ba_w)
        _proj_a_rows(seqs, raw_r, ba_r, cw_ref, alog_ref, dtb_ref,
                     q_ref, k_ref, kb_ref, qd_ref, kd_ref, vb_ref, kbe_ref, cum_ref, cumt_ref,
                     convn_ref, hist_ref)

    @pl.when(i % 2 == 0)
    def _():
        step(raw0_ref, ba0_ref, raw1_ref, ba1_ref)

    @pl.when(i % 2 == 1)
    def _():
        step(raw1_ref, ba1_ref, raw0_ref, ba0_ref)


def _proj_a_call(x, g, w, conv0, tm, seqs):
    rows = x.shape[0]
    ns = conv0.shape[0]
    n_tiles = rows // tm
    n_hist = (GDN_CONV - 1) * SUBLANES
    tiles_per_seq = rows // (ns * tm) if seqs == 1 else 1
    seq_of = (lambda i: i // tiles_per_seq) if seqs == 1 else (lambda i: i)
    cur = lambda i: jnp.minimum(i, n_tiles - 1)
    prev = lambda i: jnp.maximum(i - 1, 0)
    tile = lambda width, which: pl.BlockSpec((tm, width), lambda i: (which(i), 0))
    hist = pl.BlockSpec((seqs, n_hist, GDN_QKV), lambda i: (seq_of(prev(i)), 0, 0))
    wide_bf = jax.ShapeDtypeStruct((rows, GDN_VW), BF16)
    wide_f32 = jax.ShapeDtypeStruct((rows, GDN_VW), F32)
    perm = _chunk_perm(tm)
    return pl.pallas_call(
        functools.partial(_proj_a_body, seqs, tiles_per_seq),
        grid=(n_tiles + 1,),
        in_specs=[tile(D_MODEL, cur), _const_spec((1, D_MODEL)), _const_spec(perm.shape)]
        + [_slab_spec(w["in_a"], 0, 0, GDN_QKV), _slab_spec(w["in_a"], 0, GDN_QKV // GDN_VW, GDN_VW),
           _const_spec(w["a_ba"].shape), _const_spec(w["a_qm"].shape)]
        + [hist, _const_spec((GDN_CONV, GDN_QKV)), _const_spec((1, LANES)), _const_spec((1, LANES))],
        out_specs=[tile(GDN_VW, cur), tile(MEM_W, cur)] + [tile(GDN_VW, prev)] * 7
        + [tile(LANES, prev),
           pl.BlockSpec((tm // CHUNK, LANES, 2 * CHUNK), lambda i: (prev(i), 0, 0)), hist],
        out_shape=[wide_f32, jax.ShapeDtypeStruct((rows, MEM_W), F32)]
        + [wide_bf] * 5 + [wide_f32] * 2
        + [jax.ShapeDtypeStruct((rows, LANES), F32),
           jax.ShapeDtypeStruct((rows // CHUNK, LANES, 2 * CHUNK), F32),
           jax.ShapeDtypeStruct((ns, n_hist, GDN_QKV), F32)],
        scratch_shapes=[pltpu.VMEM((tm, GDN_QKV), F32)] * 2 + [pltpu.VMEM((tm, LANES), F32)] * 2
        + [pltpu.VMEM((seqs, n_hist, GDN_QKV), F32)],
        compiler_params=_params(("arbitrary",)),
        name="proj_a",
    )(x, g.reshape(1, D_MODEL), perm, w["in_a"], w["in_a"], w["a_ba"], w["a_qm"], conv0,
      w["conv_w_a"], w["alog_row"], w["dtb_row"])


def _gdn_chunk_solves(q_ref, k_ref, kb_ref, vb_ref, kbe_ref, cum_ref, cumt_ref, sol_ref, qk_ref):
    c = CHUNK
    nc = q_ref.shape[0] // c

    lane = lax.broadcasted_iota(jnp.int32, (c, LANES), 1)
    row = _chunk_time(lax.broadcasted_iota(jnp.int32, (c, LANES), 0))
    col = _chunk_time(lane & (c - 1))
    causal = row >= col
    strict = row > col
    low = lane < c
    eye2 = (row == col).astype(F32)

    pairs = [(j, hd) for j in range(nc) for hd in range(GDN_HEADS)]
    rs = lambda j: slice(j * c, (j + 1) * c)
    hs = lambda hd: slice(hd * GDN_DK, (hd + 1) * GDN_DK)
    gl = lambda hd: slice(GDN_HEADS + hd, GDN_HEADS + hd + 1)

    decays = [jnp.exp(jnp.where(causal, cum_ref[rs(j), gl(hd)] - cumt_ref[j, gl(hd), :], -jnp.inf))
              for j, hd in pairs]
    grams = [_mm_nt(jnp.concatenate([kb_ref[rs(j), hs(hd)], q_ref[rs(j), hs(hd)]], axis=0),
                    jnp.concatenate([k_ref[rs(j), hs(hd)]] * 2, axis=0))
             for j, hd in pairs]
    yield
    ps = [jnp.where(strict, -(g[:c] * d), 0.0) for g, d in zip(grams, decays)]
    for i, (g, d) in enumerate(zip(grams, decays)):
        qk_ref[i] = (g[c:] * d)[:, :c]

    ts = [eye2] * len(pairs)
    for _ in range(6):
        prods = [_mm(p[:, :c], jnp.where(low, p, t)) for t, p in zip(ts, ps)]
        yield
        ts = [t + r for t, r in zip(ts, prods)]
        ps = prods
    ns = [t - eye2 for t in ts]

    rhss = [jnp.concatenate([vb_ref[rs(j), hs(hd)], kbe_ref[rs(j), hs(hd)]], axis=1)
            for j, hd in pairs]
    yield
    for i, (n, rhs) in enumerate(zip(ns, rhss)):
        sol_ref[i] = rhs + _mm(n, jnp.concatenate([jnp.zeros_like(rhs, BF16), rhs.astype(BF16)],
                                                  axis=0))


def _gdn_recurrence(qd_ref, kd_ref, cum_ref, sol_ref, qk_ref, o_ref, s_ref):
    c = CHUNK
    nc = qd_ref.shape[0] // c
    heads = range(GDN_HEADS)
    rs = lambda j: slice(j * c, (j + 1) * c)
    hs = lambda hd: slice(hd * GDN_DK, (hd + 1) * GDN_DK)
    ss = [s_ref[hd] for hd in heads]
    for j in range(nc):
        sols = [sol_ref[j * GDN_HEADS + hd] for hd in heads]
        r1s = [_mm(jnp.concatenate([sols[hd][:, GDN_DV:].astype(BF16), qd_ref[rs(j), hs(hd)]],
                                   axis=0), ss[hd]) for hd in heads]
        yield
        us = [sols[hd][:, :GDN_DV] - r1s[hd][:c] for hd in heads]
        for hd in heads:
            o_ref[rs(j), hs(hd)] = r1s[hd][c:] + _mm(qk_ref[j * GDN_HEADS + hd], us[hd])
        decay = cum_ref[j * c + c - 1:j * c + c, :]
        ss = [ss[hd] * jnp.exp(decay[:, GDN_HEADS + hd:GDN_HEADS + hd + 1])
              + _mm_tn(kd_ref[rs(j), hs(hd)], us[hd]) for hd in heads]
        yield
    for hd in heads:
        s_ref[hd] = ss[hd]


def _gdn_body(skew, q_ref, k_ref, kb_ref, vb_ref, kbe_ref, cum_ref, cumt_ref, qd_ref, kd_ref,
              cumr_ref, s0_ref, o_ref, sn_ref, s_ref, sol0_ref, sol1_ref, qk0_ref, qk1_ref):
    i = pl.program_id(1)
    solve_args = (q_ref, k_ref, kb_ref, vb_ref, kbe_ref, cum_ref, cumt_ref)
    recur_args = (qd_ref, kd_ref, cumr_ref)

    if not skew:
        s_ref[...] = s0_ref[...]
        for _ in _gdn_chunk_solves(*solve_args, sol0_ref, qk0_ref):
            pass
        for _ in _gdn_recurrence(*recur_args, sol0_ref, qk0_ref, o_ref, s_ref):
            pass
        sn_ref[...] = s_ref[...]
        return

    @pl.when(i == 0)
    def _():
        sol1_ref[...] = jnp.zeros_like(sol1_ref)
        qk1_ref[...] = jnp.zeros_like(qk1_ref)
        s_ref[...] = jnp.zeros_like(s_ref)

    @pl.when(i == 1)
    def _():
        s_ref[...] = s0_ref[...]

    def step(sol_w, qk_w, sol_r, qk_r):
        solve = _gdn_chunk_solves(*solve_args, sol_w, qk_w)
        recur = _gdn_recurrence(*recur_args, sol_r, qk_r, o_ref, s_ref)
        live = [solve, recur]
        while live:
            for gen in list(live):
                if next(gen, live) is live:
                    live.remove(gen)
        sn_ref[...] = s_ref[...]

    @pl.when(i % 2 == 0)
    def _():
        step(sol0_ref, qk0_ref, sol1_ref, qk1_ref)

    @pl.when(i % 2 == 1)
    def _():
        step(sol1_ref, qk1_ref, sol0_ref, qk0_ref)


def _gdn_call(ops, s0, ns, tr):
    q, k, kb, qd, kd, vb, kbe, cum, cumt = ops
    t = q.shape[0] // ns
    n_tiles = t // tr
    skew = n_tiles > 1
    cur = (lambda i: jnp.minimum(i, n_tiles - 1)) if skew else (lambda i: i)
    prev = (lambda i: jnp.maximum(i - 1, 0)) if skew else (lambda i: i)
    seq = lambda width, which: pl.BlockSpec((None, tr, width), lambda s, i: (s, which(i), 0))
    state = pl.BlockSpec((None, GDN_HEADS, GDN_DK, GDN_DV), lambda s, i: (s, 0, 0, 0))
    rows3 = lambda a: a.reshape(ns, t, a.shape[-1])
    units = (tr // CHUNK) * GDN_HEADS
    return pl.pallas_call(
        functools.partial(_gdn_body, skew),
        grid=(ns, n_tiles + 1 if skew else n_tiles),
        in_specs=[seq(GDN_VW, cur)] * 5
        + [seq(LANES, cur),
           pl.BlockSpec((None, tr // CHUNK, LANES, 2 * CHUNK), lambda s, i: (s, cur(i), 0, 0)),
           seq(GDN_VW, prev), seq(GDN_VW, prev), seq(LANES, prev), state],
        out_specs=[seq(GDN_VW, prev), state],
        out_shape=[jax.ShapeDtypeStruct((ns, t, GDN_VW), F32),
                   jax.ShapeDtypeStruct((ns, GDN_HEADS, GDN_DK, GDN_DV), F32)],
        scratch_shapes=[pltpu.VMEM((GDN_HEADS, GDN_DK, GDN_DV), F32)]
        + [pltpu.VMEM((units, CHUNK, GDN_DV + GDN_DK), F32)] * 2
        + [pltpu.VMEM((units, CHUNK, CHUNK), F32)] * 2,
        compiler_params=_params(("arbitrary", "arbitrary")),
        name="gdn",
    )(rows3(q), rows3(k), rows3(kb), rows3(vb), rows3(kbe), rows3(cum),
      cumt.reshape(ns, t // CHUNK, LANES, 2 * CHUNK), rows3(qd), rows3(kd), rows3(cum), s0)


def _proj_b_body(x_ref, g_ref, wq_ref, wkv_ref, wqm_ref, qg_ref, kg_ref,
                 qlo_ref, qhi_ref, kd_ref, vd_ref, kc_ref, vc_ref, qm_ref):
    h = _rms(x_ref[...], g_ref[...]).astype(BF16)
    qm_ref[...] = jnp.dot(h, wqm_ref[...], preferred_element_type=F32)
    q = jnp.dot(h, wq_ref[...], preferred_element_type=F32)
    kv = jnp.dot(h, wkv_ref[...], preferred_element_type=F32)
    lo = lax.broadcasted_iota(jnp.int32, (1, LANES), 1) < SWA_HD
    qg = qg_ref[...] * (SWA_HD ** -0.5)
    for p in range(SWA_QW // LANES):
        sl = slice(p * LANES, (p + 1) * LANES)
        x = q[:, sl]
        x2 = x * x
        m_lo = jnp.sum(jnp.where(lo, x2, 0.0), axis=-1, keepdims=True) * (1.0 / SWA_HD)
        m_hi = jnp.sum(jnp.where(lo, 0.0, x2), axis=-1, keepdims=True) * (1.0 / SWA_HD)
        qn = x * jnp.where(lo, lax.rsqrt(m_lo + EPS), lax.rsqrt(m_hi + EPS)) * qg
        qlo_ref[:, sl] = jnp.where(lo, qn, 0.0).astype(BF16)
        qhi_ref[:, sl] = jnp.where(lo, 0.0, qn).astype(BF16)
    kg = kg_ref[...]
    for p in range(SWA_KV_HEADS):
        sl = slice(p * LANES, (p + 1) * LANES)
        kn = _rms(kv[:, sl], kg)
        kc_ref[:, sl] = kn
        kd_ref[:, sl] = kn.astype(BF16)
    v = kv[:, SWA_KV_HEADS * LANES:]
    vc_ref[...] = v
    vd_ref[...] = v.astype(BF16)


def _proj_b_call(x, g, w, tm, keep_all):
    rows = x.shape[0]
    dup_w = SWA_KV_HEADS * LANES
    tile = lambda width: pl.BlockSpec((tm, width), lambda i: (i, 0))
    cache = tile(dup_w) if keep_all else pl.BlockSpec((tm, dup_w), lambda i: (0, 0))
    cache_shape = jax.ShapeDtypeStruct((rows if keep_all else tm, dup_w), F32)
    return pl.pallas_call(
        _proj_b_body,
        grid=(rows // tm,),
        in_specs=[tile(D_MODEL), _const_spec((1, D_MODEL))]
        + [_slab_spec(w["in_b"], 0, 0, SWA_QW), _const_spec(w["b_kv"].shape),
           _slab_spec(w["in_b"], 0, (SWA_QW + 2 * SWA_KVW) // MEM_W, MEM_W)]
        + [_const_spec((1, LANES))] * 2,
        out_specs=[tile(SWA_QW)] * 2 + [tile(dup_w)] * 2 + [cache] * 2 + [tile(MEM_W)],
        out_shape=[jax.ShapeDtypeStruct((rows, SWA_QW), BF16)] * 2
        + [jax.ShapeDtypeStruct((rows, dup_w), BF16)] * 2 + [cache_shape] * 2
        + [jax.ShapeDtypeStruct((rows, MEM_W), F32)],
        compiler_params=_params(("arbitrary",)),
        name="proj_b",
    )(x, g.reshape(1, D_MODEL), w["in_b"], w["b_kv"], w["in_b"], w["q_gain2"], w["k_gain2"])


class _SwaJob:
    def __init__(self, hist_valid, seqs, tile_in_seq, refs, kwin_ref, vwin_ref):
        (self.qlo_ref, self.qhi_ref, self.k_ref, self.v_ref, self.kh_ref, self.vh_ref,
         self.sink_ref) = refs
        self.hist_valid, self.seqs, self.tile_in_seq = hist_valid, seqs, tile_in_seq
        self.kwin_ref, self.vwin_ref = kwin_ref, vwin_ref
        self.tm = self.qlo_ref.shape[0]
        self.rows = self.tm // seqs
        self.units = [(sq, j, h) for sq in range(seqs) for j in range(self.rows // CHUNK)
                      for h in range(SWA_KV_HEADS)]

    def load_history(self):
        self.kwin_ref[:, 0:WINDOW, :] = self.kh_ref[...]
        self.vwin_ref[:, 0:WINDOW, :] = self.vh_ref[...]

    def scores(self):
        c, rows = CHUNK, self.rows
        slab = lambda p: slice(p * LANES, (p + 1) * LANES)
        for sq in range(self.seqs):
            self.kwin_ref[sq, WINDOW:WINDOW + rows, :] = self.k_ref[sq * rows:(sq + 1) * rows, :]
            self.vwin_ref[sq, WINDOW:WINDOW + rows, :] = self.v_ref[sq * rows:(sq + 1) * rows, :]
        out = []
        for sq, j, h in self.units:
            r = slice(sq * rows + j * c, sq * rows + (j + 1) * c)
            qs = jnp.concatenate([self.qlo_ref[r, slab(2 * h)], self.qhi_ref[r, slab(2 * h)],
                                  self.qlo_ref[r, slab(2 * h + 1)], self.qhi_ref[r, slab(2 * h + 1)]],
                                 axis=0)
            s = _mm_nt(qs, self.kwin_ref[sq, j * c:j * c + WINDOW + c, slab(h)])
            if not self.hist_valid and j * c < WINDOW:
                key_col = lax.broadcasted_iota(jnp.int32, (SWA_GROUP * c, WINDOW + c), 1)
                first_key = self.tile_in_seq * rows + j * c - WINDOW
                s = jnp.where(key_col + first_key >= 0, s, -jnp.inf)
            out.append(s)
        return out

    def softmax(self, scores):
        c = CHUNK
        sinks = [jnp.concatenate([jnp.full((c, 1), self.sink_ref[0, h * SWA_GROUP + g], F32)
                                  for g in range(SWA_GROUP)], axis=0)
                 for h in range(SWA_KV_HEADS)]
        out = []
        for s, (sq, j, h) in zip(scores, self.units):
            m = jnp.maximum(jnp.max(s, axis=-1, keepdims=True), sinks[h])
            p = jnp.exp(s - m)
            out.append((p / (jnp.sum(p, axis=-1, keepdims=True) + jnp.exp(sinks[h] - m)))
                       .astype(BF16))
        return out

    def values(self, probs):
        c, rows = CHUNK, self.rows
        slab = lambda p: slice(p * LANES, (p + 1) * LANES)
        lo = lax.broadcasted_iota(jnp.int32, (1, LANES), 1) < SWA_HD
        zero = jnp.zeros((), BF16)
        slabs = {}
        for p, (sq, j, h) in zip(probs, self.units):
            v = self.vwin_ref[sq, j * c:j * c + WINDOW + c, slab(h)]
            v_lo = jnp.where(lo, v, zero)
            v_hi = jnp.where(lo, zero, v)
            even = _mm(jnp.concatenate([p[0:c], p[2 * c:3 * c]], axis=0), v_lo)
            odd = _mm(jnp.concatenate([p[c:2 * c], p[3 * c:4 * c]], axis=0), v_hi)
            for half in range(2):
                slabs[sq, j, 2 * h + half] = (even[half * c:(half + 1) * c]
                                              + odd[half * c:(half + 1) * c]).astype(BF16)
        for ref in (self.kwin_ref, self.vwin_ref):
            ref[:, 0:WINDOW, :] = ref[:, rows:rows + WINDOW, :]
        return jnp.concatenate(
            [jnp.concatenate([slabs[sq, j, p] for p in range(SWA_QW // LANES)], axis=1)
             for sq in range(self.seqs) for j in range(rows // c)], axis=0)


def _tail_step(seqs, gated, swa, x_ref, om_ref, gate_refs, qm_ref, mk_ref, mv_ref, mqg_ref, wo_ref,
               gf_ref, wgu_ref, wd_ref, y_ref, mix_w, mix_r):
    tm = x_ref.shape[0]
    rows = tm // seqs
    d_ff = wd_ref.shape[0]
    half = d_ff // 2
    heads = [(sq, hd) for sq in range(seqs) for hd in range(MEM_HEADS)]
    rs = lambda sq: slice(sq * rows, (sq + 1) * rows)
    hs = lambda hd: slice(hd * MEM_HD, (hd + 1) * MEM_HD)

    x = x_ref[...] + jnp.dot(mix_r[...], wo_ref[...], preferred_element_type=F32)

    if swa:
        swa_scores = swa.scores()
    mqg = mqg_ref[...] * (MEM_HD ** -0.5)
    scores = [_mm_nt(_rms(qm_ref[rs(sq), hs(hd)], mqg), mk_ref[sq, :, hs(hd)])
              for sq, hd in heads]

    h = _rms(x, gf_ref[...]).astype(BF16)
    g0 = jnp.dot(h, wgu_ref[:, 0:half], preferred_element_type=F32)
    u0 = jnp.dot(h, wgu_ref[:, d_ff:d_ff + half], preferred_element_type=F32)

    probs = []
    for s in scores:
        p = jnp.exp(s - jnp.max(s, axis=-1, keepdims=True))
        probs.append((p / jnp.sum(p, axis=-1, keepdims=True)).astype(BF16))
    o_heads = [_mm(p, mv_ref[sq, :, hs(hd)]).astype(BF16) for p, (sq, hd) in zip(probs, heads)]
    o_mem = jnp.concatenate(
        [jnp.concatenate(o_heads[sq * MEM_HEADS:(sq + 1) * MEM_HEADS], axis=1)
         for sq in range(seqs)], axis=0)

    act0 = (_silu(g0) * u0).astype(BF16)
    g1 = jnp.dot(h, wgu_ref[:, half:d_ff], preferred_element_type=F32)
    u1 = jnp.dot(h, wgu_ref[:, d_ff + half:], preferred_element_type=F32)

    if gated:
        z_ref, og_ref, unperm_ref = gate_refs
        og = og_ref[...]
        parts = [(_rms(om_ref[:, hd * GDN_DV:(hd + 1) * GDN_DV], og)
                  * _silu(z_ref[:, hd * GDN_DV:(hd + 1) * GDN_DV])).astype(BF16)
                 for hd in range(GDN_HEADS)]
        mixed = jnp.concatenate(parts + [o_mem], axis=1)
        mix_w[...] = jnp.dot(unperm_ref[...], mixed, preferred_element_type=F32).astype(BF16)
    else:
        mix_w[...] = jnp.concatenate([swa.values(swa.softmax(swa_scores)), o_mem], axis=1)

    y = x + jnp.dot(act0, wd_ref[0:half, :], preferred_element_type=F32)
    act1 = (_silu(g1) * u1).astype(BF16)
    y_ref[...] = y + jnp.dot(act1, wd_ref[half:, :], preferred_element_type=F32)


def _tail_body(seqs, tiles_per_seq, hist_valid, x_ref, *refs):
    gated = hist_valid is None
    n_mixer = 4 if gated else 7
    mixer_refs, refs = refs[:n_mixer], refs[n_mixer:]
    qm_ref, mk_ref, mv_ref, mqg_ref, wo_ref, gf_ref, wgu_ref, wd_ref, y_ref = refs[:9]
    mix0_ref, mix1_ref = refs[9:11]
    i = pl.program_id(0)

    @pl.when(i == 0)
    def _():
        mix1_ref[...] = jnp.zeros_like(mix1_ref)

    if gated:
        om_ref, gate_refs, swa = mixer_refs[0], mixer_refs[1:], None
    else:
        om_ref, gate_refs = None, ()
        swa = _SwaJob(hist_valid, seqs, i % tiles_per_seq, mixer_refs, *refs[11:13])

        @pl.when(i % tiles_per_seq == 0)
        def _():
            swa.load_history()

    def step(mix_w, mix_r):
        _tail_step(seqs, gated, swa, x_ref, om_ref, gate_refs, qm_ref, mk_ref, mv_ref, mqg_ref,
                   wo_ref, gf_ref, wgu_ref, wd_ref, y_ref, mix_w, mix_r)

    @pl.when(i % 2 == 0)
    def _():
        step(mix0_ref, mix1_ref)

    @pl.when(i % 2 == 1)
    def _():
        step(mix1_ref, mix0_ref)


def _tail_call(x, gdn, swa, qm, mk, mv, mem_q_gain, w_out, g_ffn, w_gate_up, w_down, layer, tm,
               seqs):
    rows = x.shape[0]
    ns = mk.shape[1]
    n_tiles = rows // tm
    cur = lambda i: jnp.minimum(i, n_tiles - 1)
    prev = lambda i: jnp.maximum(i - 1, 0)
    tile = lambda width, which: pl.BlockSpec((tm, width), lambda i: (which(i), 0))
    tiles_per_seq = rows // (tm * ns) if seqs == 1 else 1
    per_seq = lambda *dims: pl.BlockSpec((seqs,) + dims,
                                         lambda i: (cur(i) // tiles_per_seq,) + (0,) * len(dims))
    mem = pl.BlockSpec((None, seqs, N_MEM, MEM_W),
                       lambda i: (layer, cur(i) // tiles_per_seq, 0, 0))
    if gdn:
        o_raw, z, o_gain = gdn
        hist_valid = None
        mixer_specs = [tile(GDN_VW, cur), tile(GDN_VW, cur), _const_spec((1, GDN_DV)),
                       _const_spec((tm, tm))]
        mixer_args = [o_raw, z, o_gain.reshape(1, GDN_DV), _chunk_perm(tm).T]
        scratch = []
    else:
        *mixer_args, sinks, hist_valid = swa
        dup_w = SWA_KV_HEADS * LANES
        mixer_specs = [tile(SWA_QW, cur), tile(SWA_QW, cur), tile(dup_w, cur), tile(dup_w, cur),
                       per_seq(WINDOW, dup_w), per_seq(WINDOW, dup_w),
                       pl.BlockSpec(memory_space=pltpu.SMEM)]
        mixer_args.append(sinks.reshape(1, SWA_HEADS))
        scratch = [pltpu.VMEM((seqs, WINDOW + tm // seqs, dup_w), BF16)] * 2
    return pl.pallas_call(
        functools.partial(_tail_body, seqs, tiles_per_seq, hist_valid),
        grid=(n_tiles + 1,),
        in_specs=[tile(D_MODEL, prev)] + mixer_specs + [
            tile(MEM_W, cur), mem, mem,
            _const_spec((1, MEM_HD)), _slab_spec(w_out, 0, 0, D_MODEL), _const_spec((1, D_MODEL)),
            _slab_spec(w_gate_up, layer, 0, w_gate_up.shape[2]),
            _slab_spec(w_down, layer, 0, D_MODEL),
        ],
        out_specs=tile(D_MODEL, prev),
        out_shape=jax.ShapeDtypeStruct((rows, D_MODEL), F32),
        scratch_shapes=[pltpu.VMEM((tm, w_out.shape[1]), BF16)] * 2 + scratch,
        compiler_params=_params(("arbitrary",)),
        name="tail",
    )(x, *mixer_args, qm, mk, mv, mem_q_gain.reshape(1, MEM_HD), w_out,
      g_ffn.reshape(1, D_MODEL), w_gate_up, w_down)


def _lane_row(vals, offset):
    return jnp.zeros((1, LANES), F32).at[0, offset:offset + vals.shape[0]].set(vals)


def _dup_heads(a):
    lead = a.shape[:-1]
    a = a.reshape(lead + (SWA_KV_HEADS, 1, SWA_HD))
    return jnp.broadcast_to(a, lead + (SWA_KV_HEADS, 2, SWA_HD)).reshape(lead + (SWA_KV_HEADS * LANES,))


def _undup_heads(a):
    lead = a.shape[:-1]
    return a.reshape(lead + (SWA_KV_HEADS, 2, SWA_HD))[..., 0, :]


def _trunk(x, mem_k, mem_v, gdn_conv, gdn_state, swa_k_hist, swa_v_hist, hist_valid, w):
    ns, t, _ = x.shape
    rows = ns * t
    tm = ROW_TILE
    seqs = max(1, tm // t)
    x2 = x.reshape(rows, D_MODEL)

    conv0 = jnp.pad(gdn_conv[:, :, None, :], ((0, 0), (0, 0), (SUBLANES - 1, 0), (0, 0)))
    conv0 = conv0.reshape(ns, (GDN_CONV - 1) * SUBLANES, GDN_QKV)
    z, qm, *gdn_ops, conv_new = _proj_a_call(x2, w["norm_mix"][0], w, conv0, tm, seqs)
    conv_new = conv_new[:, SUBLANES - 1::SUBLANES, :]
    o_raw, s_new = _gdn_call(gdn_ops, gdn_state, ns, min(t, GDN_TILE_ROWS))
    x2 = _tail_call(x2, (o_raw.reshape(rows, GDN_VW), z, w["o_norm_a"]), None, qm, mem_k, mem_v,
                    w["mem_q_norm"][0], w["out_a"], w["norm_ffn"][0], w["gate_up"], w["down"], 0,
                    tm, seqs)

    qlo, qhi, kd, vd, k_cache, v_cache, qm = _proj_b_call(x2, w["norm_mix"][1], w, PROJ_B_TILE,
                                                          ns > 1)
    swa = (qlo, qhi, kd, vd, _dup_heads(swa_k_hist).astype(BF16),
           _dup_heads(swa_v_hist).astype(BF16), w["sinks_b"], hist_valid)
    x2 = _tail_call(x2, None, swa, qm, mem_k, mem_v, w["mem_q_norm"][1], w["out_b"],
                    w["norm_ffn"][1], w["gate_up"], w["down"], 1, tm, seqs)

    keep = min(WINDOW, t) if ns == 1 else t
    k_new = _undup_heads(k_cache).reshape(ns, -1, SWA_KVW)[:, -keep:]
    v_new = _undup_heads(v_cache).reshape(ns, -1, SWA_KVW)[:, -keep:]
    return x2.reshape(ns, t, D_MODEL), conv_new, s_new, k_new, v_new


def kernel(x_prompt, x_sample, mem_prompt, cache_mem_k, cache_mem_v, state_gdn, state_gdn_conv, cache_swa_k, cache_swa_v, norm_mix, norm_ffn, mem_norm, w_mem_kv, mem_q_norm, mem_k_norm, w_in_a, conv_w_a, a_log, dt_bias, o_norm_a, w_out_a, w_in_b, q_norm_b, k_norm_b, sinks_b, w_out_b, w_gate_up, w_down):
    bsz = x_prompt.shape[0]
    dec = x_sample.shape[0]
    wa = w_in_a[0]
    ba_lo = GDN_QKV + GDN_VW
    qm_lo = ba_lo + 2 * GDN_HEADS
    wb = w_in_b[0]
    w = {
        "norm_mix": norm_mix, "norm_ffn": norm_ffn, "mem_q_norm": mem_q_norm,
        "in_a": w_in_a.astype(BF16), "in_b": w_in_b.astype(BF16),
        "a_ba": jnp.pad(wa[:, ba_lo:qm_lo], ((0, 0), (0, LANES - 2 * GDN_HEADS))).astype(BF16),
        "a_qm": wa[:, qm_lo:].astype(BF16),
        "b_kv": jnp.concatenate([_dup_heads(wb[:, SWA_QW:SWA_QW + SWA_KVW]),
                                 _dup_heads(wb[:, SWA_QW + SWA_KVW:SWA_QW + 2 * SWA_KVW])],
                                axis=1).astype(BF16),
        "q_gain2": jnp.tile(q_norm_b[0], 2).reshape(1, LANES),
        "k_gain2": jnp.tile(k_norm_b[0], 2).reshape(1, LANES),
        "conv_w_a": conv_w_a[0], "o_norm_a": o_norm_a[0],
        "alog_row": _lane_row(a_log[0], GDN_HEADS), "dtb_row": _lane_row(dt_bias[0], GDN_HEADS),
        "out_a": w_out_a.astype(BF16), "out_b": w_out_b.astype(BF16),
        "sinks_b": sinks_b[0],
        "gate_up": w_gate_up.astype(BF16), "down": w_down.astype(BF16),
    }

    mk, mv, mk_bf, mv_bf = _memkv_call(mem_prompt[0], mem_norm, w_mem_kv, mem_k_norm)
    depth = mk.shape[0]
    new_mem_k = mk.reshape(depth, bsz, N_MEM, MEM_HEADS, MEM_HD)
    new_mem_v = mv.reshape(depth, bsz, N_MEM, MEM_HEADS, MEM_HD)

    zero_conv = jnp.zeros((bsz, GDN_CONV - 1, GDN_QKV), F32)
    zero_state = jnp.zeros((bsz, GDN_HEADS, GDN_DK, GDN_DV), F32)
    zero_hist = jnp.zeros((bsz, WINDOW, SWA_KVW), F32)
    y_p, conv_p, state_p, k_p, v_p = _trunk(
        x_prompt, mk_bf.reshape(depth, bsz, N_MEM, MEM_W), mv_bf.reshape(depth, bsz, N_MEM, MEM_W),
        zero_conv, zero_state, zero_hist, zero_hist, False, w)

    y_s, conv_s, state_s, k_s, v_s = _trunk(
        x_sample, cache_mem_k.reshape(depth, dec, N_MEM, MEM_W).astype(BF16),
        cache_mem_v.reshape(depth, dec, N_MEM, MEM_W).astype(BF16), state_gdn_conv[0], state_gdn[0],
        cache_swa_k[0].reshape(dec, WINDOW, SWA_KVW), cache_swa_v[0].reshape(dec, WINDOW, SWA_KVW),
        True, w)

    kv_shape = lambda a: a.reshape(a.shape[0], a.shape[1], SWA_KV_HEADS, SWA_HD)[None]
    return (y_p, y_s, state_p[None], conv_p[None], state_s[None], conv_s[None],
            kv_shape(k_p), kv_shape(v_p), kv_shape(k_s), kv_shape(v_s), new_mem_k, new_mem_v)
```

```python
import functools

import jax
import jax.numpy as jnp
import numpy as np
from jax import lax
from jax.experimental import pallas as pl
from jax.experimental.pallas import tpu as pltpu

F32 = jnp.float32
BF16 = jnp.bfloat16

D_MODEL = 1024
CHUNK = 64
EPS = 1e-6
GDN_HEADS = 8
GDN_DK = 128
GDN_DV = 128
GDN_CONV = 4
GDN_QKV = GDN_HEADS * (2 * GDN_DK + GDN_DV)
GDN_VW = GDN_HEADS * GDN_DV
SWA_HEADS = 16
SWA_KV_HEADS = 4
SWA_HD = 64
SWA_GROUP = SWA_HEADS // SWA_KV_HEADS
SWA_QW = SWA_HEADS * SWA_HD
SWA_KVW = SWA_KV_HEADS * SWA_HD
WINDOW = 128
N_MEM = 256
MEM_HEADS = 4
MEM_HD = 128
MEM_W = MEM_HEADS * MEM_HD
LANES = 128
SUBLANES = 8
VMEM_LIMIT = 56 * 1024 * 1024
ROW_TILE = 256
PROJ_B_TILE = 512
SWA_TILE_ROWS = 4 * CHUNK
GDN_TILE_ROWS = 4 * CHUNK

_HI = lax.Precision.HIGHEST


def _mm(a, b):
    return jnp.dot(a.astype(BF16), b.astype(BF16), preferred_element_type=F32)


def _mm_nt(a, b):
    return lax.dot_general(a.astype(BF16), b.astype(BF16), (((1,), (1,)), ((), ())),
                           preferred_element_type=F32)


def _mm_tn(a, b):
    return lax.dot_general(a.astype(BF16), b.astype(BF16), (((0,), (0,)), ((), ())),
                           preferred_element_type=F32)


def _rms(x, g):
    return x * lax.rsqrt(jnp.mean(x * x, axis=-1, keepdims=True) + EPS) * g


def _sigmoid(x):
    return 1.0 / (1.0 + jnp.exp(-x))


def _silu(x):
    hx = 0.5 * x
    return hx * jnp.tanh(hx) + hx


def _softplus(x):
    return jnp.maximum(x, 0.0) + jnp.log1p(jnp.exp(-jnp.abs(x)))


def _const_spec(shape):
    nd = len(shape)
    return pl.BlockSpec(shape, lambda *_: (0,) * nd, pipeline_mode=pl.Buffered(1))


def _slab_spec(arr, layer, col_block, width):
    return pl.BlockSpec((None, arr.shape[1], width), lambda *_: (layer, 0, col_block),
                        pipeline_mode=pl.Buffered(1))


def _params(sem):
    return pltpu.CompilerParams(dimension_semantics=sem, vmem_limit_bytes=VMEM_LIMIT)


def _memkv_body(mem_ref, g_ref, w_ref, kg_ref, mk_ref, mv_ref, mkb_ref, mvb_ref):
    h = _rms(mem_ref[...], g_ref[...])
    kv = _mm(h, w_ref[...])
    kg = kg_ref[...]
    for hd in range(MEM_HEADS):
        sl = slice(hd * MEM_HD, (hd + 1) * MEM_HD)
        mk = _rms(kv[:, sl], kg)
        mk_ref[:, sl] = mk
        mkb_ref[:, sl] = mk.astype(BF16)
    mv_ref[...] = kv[:, MEM_W:]
    mvb_ref[...] = kv[:, MEM_W:].astype(BF16)


def _memkv_call(mem, mem_norm, w_mem_kv, mem_k_norm):
    depth = w_mem_kv.shape[0]
    out = jax.ShapeDtypeStruct((depth, N_MEM, MEM_W), F32)
    out_bf = jax.ShapeDtypeStruct((depth, N_MEM, MEM_W), BF16)
    return pl.pallas_call(
        _memkv_body,
        grid=(depth,),
        in_specs=[
            pl.BlockSpec((N_MEM, D_MODEL), lambda i: (0, 0)),
            pl.BlockSpec((None, 1, D_MODEL), lambda i: (i, 0, 0)),
            pl.BlockSpec((None, D_MODEL, 2 * MEM_W), lambda i: (i, 0, 0)),
            pl.BlockSpec((None, 1, MEM_HD), lambda i: (i, 0, 0)),
        ],
        out_specs=[pl.BlockSpec((None, N_MEM, MEM_W), lambda i: (i, 0, 0))] * 4,
        out_shape=[out, out, out_bf, out_bf],
        compiler_params=_params(("arbitrary",)),
        name="memkv",
    )(mem, mem_norm.reshape(depth, 1, D_MODEL), w_mem_kv, mem_k_norm.reshape(depth, 1, MEM_HD))


def _chunk_time(pos):
    return lax.shift_right_logical(pos, 3) + SUBLANES * (pos & (SUBLANES - 1))


def _chunk_perm(tm):
    pos = np.arange(tm)
    src = (pos // CHUNK) * CHUNK + (pos % CHUNK) // SUBLANES + SUBLANES * (pos % SUBLANES)
    return jnp.asarray(src[:, None] == pos[None, :], BF16)


def _proj_a_matmuls(x_ref, g_ref, perm_ref, wqkv_ref, wz_ref, wba_ref, wqm_ref,
                    z_ref, qm_ref, raw_ref, ba_ref):
    h = _rms(x_ref[...], g_ref[...]).astype(BF16)
    h = jnp.dot(perm_ref[...], h, preferred_element_type=F32).astype(BF16)
    z_ref[...] = jnp.dot(h, wz_ref[...], preferred_element_type=F32)
    qm_ref[...] = jnp.dot(h, wqm_ref[...], preferred_element_type=F32)
    ba_ref[...] = jnp.dot(h, wba_ref[...], preferred_element_type=F32)
    raw_ref[...] = jnp.dot(h, wqkv_ref[...], preferred_element_type=F32)


def _proj_a_rows(seqs, raw_ref, ba_ref, cw_ref, alog_ref, dtb_ref,
                 q_ref, k_ref, kb_ref, qd_ref, kd_ref, vb_ref, kbe_ref, cum_ref, cumt_ref,
                 convn_ref, hist_ref):
    c = CHUNK
    tm = raw_ref.shape[0]
    rows = tm // seqs
    n_hist = (GDN_CONV - 1) * SUBLANES

    t_row = _chunk_time(lax.broadcasted_iota(jnp.int32, (c, c), 0))
    t_col = _chunk_time(lax.broadcasted_iota(jnp.int32, (c, c), 1))
    tril = (t_row >= t_col).astype(F32)
    eye_l = (lax.broadcasted_iota(jnp.int32, (LANES, LANES), 0)
             == lax.broadcasted_iota(jnp.int32, (LANES, LANES), 1)).astype(F32)
    sub0 = lax.broadcasted_iota(jnp.int32, (SUBLANES, LANES), 0) == 0

    ba = ba_ref[...]
    beta_all = _sigmoid(ba)
    g_all = -jnp.exp(alog_ref[...]) * _softplus(ba + dtb_ref[...])

    for j in range(tm // c):
        rs = slice(j * c, (j + 1) * c)
        s, lr = (j * c) // rows, (j * c) % rows
        cum = jnp.dot(tril, g_all[rs], precision=_HI, preferred_element_type=F32)
        cum_ref[rs, :] = cum
        cumt_ref[j] = lax.dot_general(eye_l, jnp.concatenate([cum, cum], axis=0),
                                      (((1,), (1,)), ((), ())), precision=_HI,
                                      preferred_element_type=F32)
        e_cum = jnp.exp(cum)
        e_rest = jnp.exp(cum[c - 1:c, :] - cum)
        beta = beta_all[rs]

        def conv_act(lo):
            sl = slice(lo, lo + LANES)
            x = raw_ref[rs, sl]
            prev = (hist_ref[s, :, sl] if lr == 0
                    else raw_ref[j * c - n_hist:j * c, sl])
            shifted = [jnp.where(sub0,
                                 pltpu.roll(prev[i * SUBLANES:(i + 1) * SUBLANES], 1, 0),
                                 pltpu.roll(x[c - n_hist + i * SUBLANES:c - n_hist + (i + 1) * SUBLANES],
                                            1, 0)) for i in range(GDN_CONV - 1)]
            acc = x * cw_ref[GDN_CONV - 1:GDN_CONV, sl]
            for d in range(1, GDN_CONV):
                xd = jnp.concatenate(shifted[GDN_CONV - 1 - d:] + [x[0:c - d * SUBLANES]], axis=0)
                acc = acc + xd * cw_ref[GDN_CONV - 1 - d:GDN_CONV - d, sl]
            return _silu(acc)

        for hd in range(GDN_HEADS):
            hs = slice(hd * GDN_DK, (hd + 1) * GDN_DK)
            b_col = beta[:, hd:hd + 1]
            ec_col = e_cum[:, GDN_HEADS + hd:GDN_HEADS + hd + 1]
            er_col = e_rest[:, GDN_HEADS + hd:GDN_HEADS + hd + 1]
            q = conv_act(hd * GDN_DK)
            q = q * (lax.rsqrt(jnp.sum(q * q, axis=-1, keepdims=True) + EPS) * (GDN_DK ** -0.5))
            k = conv_act(GDN_HEADS * GDN_DK + hd * GDN_DK)
            k = k * lax.rsqrt(jnp.sum(k * k, axis=-1, keepdims=True) + EPS)
            v = conv_act(2 * GDN_HEADS * GDN_DK + hd * GDN_DV)
            kb = k * b_col
            q_ref[rs, hs] = q.astype(BF16)
            qd_ref[rs, hs] = (q * ec_col).astype(BF16)
            k_ref[rs, hs] = k.astype(BF16)
            kb_ref[rs, hs] = kb.astype(BF16)
            kd_ref[rs, hs] = (k * er_col).astype(BF16)
            kbe_ref[rs, hs] = kb * ec_col
            vb_ref[rs, hs] = v * b_col

    for s in range(seqs):
        tail = raw_ref[(s + 1) * rows - n_hist:(s + 1) * rows, :]
        hist_ref[s] = tail
        convn_ref[s] = tail


def _proj_a_body(seqs, tiles_per_seq, x_ref, g_ref, perm_ref, wqkv_ref, wz_ref, wba_ref, wqm_ref,
                 conv0_ref, cw_ref, alog_ref, dtb_ref,
                 z_ref, qm_ref, q_ref, k_ref, kb_ref, qd_ref, kd_ref, vb_ref, kbe_ref, cum_ref,
                 cumt_ref, convn_ref, raw0_ref, raw1_ref, ba0_ref, ba1_ref, hist_ref):
    i = pl.program_id(0)

    @pl.when(i == 0)
    def _():
        raw1_ref[...] = jnp.zeros_like(raw1_ref)
        ba1_ref[...] = jnp.zeros_like(ba1_ref)
        hist_ref[...] = jnp.zeros_like(hist_ref)

    @pl.when((i + tiles_per_seq - 1) % tiles_per_seq == 0)
    def _():
        hist_ref[...] = conv0_ref[...]

    def step(raw_w, ba_w, raw_r, ba_r):
        _proj_a_matmuls(x_ref, g_ref, perm_ref, wqkv_ref, wz_ref, wba_ref, wqm_ref,
                        z_ref, qm_ref, raw_w, ba_w)
        _proj_a_rows(seqs, raw_r, ba_r, cw_ref, alog_ref, dtb_ref,
                     q_ref, k_ref, kb_ref, qd_ref, kd_ref, vb_ref, kbe_ref, cum_ref, cumt_ref,
                     convn_ref, hist_ref)

    @pl.when(i % 2 == 0)
    def _():
        step(raw0_ref, ba0_ref, raw1_ref, ba1_ref)

    @pl.when(i % 2 == 1)
    def _():
        step(raw1_ref, ba1_ref, raw0_ref, ba0_ref)


def _proj_a_call(x, g, w, conv0, tm, seqs):
    rows = x.shape[0]
    ns = conv0.shape[0]
    n_tiles = rows // tm
    n_hist = (GDN_CONV - 1) * SUBLANES
    tiles_per_seq = rows // (ns * tm) if seqs == 1 else 1
    seq_of = (lambda i: i // tiles_per_seq) if seqs == 1 else (lambda i: i)
    cur = lambda i: jnp.minimum(i, n_tiles - 1)
    prev = lambda i: jnp.maximum(i - 1, 0)
    tile = lambda width, which: pl.BlockSpec((tm, width), lambda i: (which(i), 0))
    hist = pl.BlockSpec((seqs, n_hist, GDN_QKV), lambda i: (seq_of(prev(i)), 0, 0))
    wide_bf = jax.ShapeDtypeStruct((rows, GDN_VW), BF16)
    wide_f32 = jax.ShapeDtypeStruct((rows, GDN_VW), F32)
    perm = _chunk_perm(tm)
    return pl.pallas_call(
        functools.partial(_proj_a_body, seqs, tiles_per_seq),
        grid=(n_tiles + 1,),
        in_specs=[tile(D_MODEL, cur), _const_spec((1, D_MODEL)), _const_spec(perm.shape)]
        + [_slab_spec(w["in_a"], 0, 0, GDN_QKV), _slab_spec(w["in_a"], 0, GDN_QKV // GDN_VW, GDN_VW),
           _const_spec(w["a_ba"].shape), _const_spec(w["a_qm"].shape)]
        + [hist, _const_spec((GDN_CONV, GDN_QKV)), _const_spec((1, LANES)), _const_spec((1, LANES))],
        out_specs=[tile(GDN_VW, cur), tile(MEM_W, cur)] + [tile(GDN_VW, prev)] * 7
        + [tile(LANES, prev),
           pl.BlockSpec((tm // CHUNK, LANES, 2 * CHUNK), lambda i: (prev(i), 0, 0)), hist],
        out_shape=[wide_f32, jax.ShapeDtypeStruct((rows, MEM_W), F32)]
        + [wide_bf] * 5 + [wide_f32] * 2
        + [jax.ShapeDtypeStruct((rows, LANES), F32),
           jax.ShapeDtypeStruct((rows // CHUNK, LANES, 2 * CHUNK), F32),
           jax.ShapeDtypeStruct((ns, n_hist, GDN_QKV), F32)],
        scratch_shapes=[pltpu.VMEM((tm, GDN_QKV), F32)] * 2 + [pltpu.VMEM((tm, LANES), F32)] * 2
        + [pltpu.VMEM((seqs, n_hist, GDN_QKV), F32)],
        compiler_params=_params(("arbitrary",)),
        name="proj_a",
    )(x, g.reshape(1, D_MODEL), perm, w["in_a"], w["in_a"], w["a_ba"], w["a_qm"], conv0,
      w["conv_w_a"], w["alog_row"], w["dtb_row"])


def _gdn_chunk_solves(q_ref, k_ref, kb_ref, vb_ref, kbe_ref, cum_ref, cumt_ref, sol_ref, qk_ref):
    c = CHUNK
    nc = q_ref.shape[0] // c

    row = _chunk_time(lax.broadcasted_iota(jnp.int32, (c, c), 0))
    col = _chunk_time(lax.broadcasted_iota(jnp.int32, (c, c), 1))
    causal = row >= col
    strict = row > col
    lane = lax.broadcasted_iota(jnp.int32, (c, LANES), 1)
    low = lane < c
    eye2 = (lax.broadcasted_iota(jnp.int32, (c, LANES), 0) == (lane & (c - 1))).astype(F32)
    pad = jnp.zeros((c, LANES - c), F32)

    pairs = [(j, hd) for j in range(nc) for hd in range(GDN_HEADS)]
    rs = lambda j: slice(j * c, (j + 1) * c)
    hs = lambda hd: slice(hd * GDN_DK, (hd + 1) * GDN_DK)
    gl = lambda hd: slice(GDN_HEADS + hd, GDN_HEADS + hd + 1)

    decays = [jnp.exp(jnp.where(causal, cum_ref[rs(j), gl(hd)] - cumt_ref[j, gl(hd), 0:c],
                                -jnp.inf)) for j, hd in pairs]
    grams = [_mm_nt(jnp.concatenate([kb_ref[rs(j), hs(hd)], q_ref[rs(j), hs(hd)]], axis=0),
                    k_ref[rs(j), hs(hd)]) for j, hd in pairs]
    yield
    ps = [jnp.concatenate([jnp.where(strict, -(g[:c] * d), 0.0), pad], axis=1)
          for g, d in zip(grams, decays)]
    for i, (g, d) in enumerate(zip(grams, decays)):
        qk_ref[i] = g[c:] * d

    ts = [eye2] * len(pairs)
    for _ in range(6):
        prods = [_mm(p[:, :c], jnp.where(low, p, t)) for t, p in zip(ts, ps)]
        yield
        ts = [t + r for t, r in zip(ts, prods)]
        ps = prods
    ns = [pltpu.roll(t - eye2, c, 1)[:, :c] for t in ts]

    rhss = [jnp.concatenate([vb_ref[rs(j), hs(hd)], kbe_ref[rs(j), hs(hd)]], axis=1)
            for j, hd in pairs]
    yield
    for i, (n, rhs) in enumerate(zip(ns, rhss)):
        sol_ref[i] = rhs + _mm(n, rhs)


def _gdn_recurrence(qd_ref, kd_ref, cum_ref, sol_ref, qk_ref, o_ref, s_ref):
    c = CHUNK
    nc = qd_ref.shape[0] // c
    heads = range(GDN_HEADS)
    rs = lambda j: slice(j * c, (j + 1) * c)
    hs = lambda hd: slice(hd * GDN_DK, (hd + 1) * GDN_DK)
    ss = [s_ref[hd] for hd in heads]
    for j in range(nc):
        sols = [sol_ref[j * GDN_HEADS + hd] for hd in heads]
        r1s = [_mm(jnp.concatenate([sols[hd][:, GDN_DV:].astype(BF16), qd_ref[rs(j), hs(hd)]],
                                   axis=0), ss[hd]) for hd in heads]
        yield
        us = [sols[hd][:, :GDN_DV] - r1s[hd][:c] for hd in heads]
        for hd in heads:
            o_ref[rs(j), hs(hd)] = r1s[hd][c:] + _mm(qk_ref[j * GDN_HEADS + hd], us[hd])
        decay = cum_ref[j * c + c - 1:j * c + c, :]
        ss = [ss[hd] * jnp.exp(decay[:, GDN_HEADS + hd:GDN_HEADS + hd + 1])
              + _mm_tn(kd_ref[rs(j), hs(hd)], us[hd]) for hd in heads]
        yield
    for hd in heads:
        s_ref[hd] = ss[hd]


def _gdn_body(skew, q_ref, k_ref, kb_ref, vb_ref, kbe_ref, cum_ref, cumt_ref, qd_ref, kd_ref,
              cumr_ref, s0_ref, o_ref, sn_ref, s_ref, sol0_ref, sol1_ref, qk0_ref, qk1_ref):
    i = pl.program_id(1)
    solve_args = (q_ref, k_ref, kb_ref, vb_ref, kbe_ref, cum_ref, cumt_ref)
    recur_args = (qd_ref, kd_ref, cumr_ref)

    if not skew:
        s_ref[...] = s0_ref[...]
        for _ in _gdn_chunk_solves(*solve_args, sol0_ref, qk0_ref):
            pass
        for _ in _gdn_recurrence(*recur_args, sol0_ref, qk0_ref, o_ref, s_ref):
            pass
        sn_ref[...] = s_ref[...]
        return

    @pl.when(i == 0)
    def _():
        sol1_ref[...] = jnp.zeros_like(sol1_ref)
        qk1_ref[...] = jnp.zeros_like(qk1_ref)
        s_ref[...] = jnp.zeros_like(s_ref)

    @pl.when(i == 1)
    def _():
        s_ref[...] = s0_ref[...]

    def step(sol_w, qk_w, sol_r, qk_r):
        solve = _gdn_chunk_solves(*solve_args, sol_w, qk_w)
        recur = _gdn_recurrence(*recur_args, sol_r, qk_r, o_ref, s_ref)
        live = [solve, recur]
        while live:
            for gen in list(live):
                if next(gen, live) is live:
                    live.remove(gen)
        sn_ref[...] = s_ref[...]

    @pl.when(i % 2 == 0)
    def _():
        step(sol0_ref, qk0_ref, sol1_ref, qk1_ref)

    @pl.when(i % 2 == 1)
    def _():
        step(sol1_ref, qk1_ref, sol0_ref, qk0_ref)


def _gdn_call(ops, s0, ns, tr):
    q, k, kb, qd, kd, vb, kbe, cum, cumt = ops
    t = q.shape[0] // ns
    n_tiles = t // tr
    skew = n_tiles > 1
    cur = (lambda i: jnp.minimum(i, n_tiles - 1)) if skew else (lambda i: i)
    prev = (lambda i: jnp.maximum(i - 1, 0)) if skew else (lambda i: i)
    seq = lambda width, which: pl.BlockSpec((None, tr, width), lambda s, i: (s, which(i), 0))
    state = pl.BlockSpec((None, GDN_HEADS, GDN_DK, GDN_DV), lambda s, i: (s, 0, 0, 0))
    rows3 = lambda a: a.reshape(ns, t, a.shape[-1])
    units = (tr // CHUNK) * GDN_HEADS
    return pl.pallas_call(
        functools.partial(_gdn_body, skew),
        grid=(ns, n_tiles + 1 if skew else n_tiles),
        in_specs=[seq(GDN_VW, cur)] * 5
        + [seq(LANES, cur),
           pl.BlockSpec((None, tr // CHUNK, LANES, 2 * CHUNK), lambda s, i: (s, cur(i), 0, 0)),
           seq(GDN_VW, prev), seq(GDN_VW, prev), seq(LANES, prev), state],
        out_specs=[seq(GDN_VW, prev), state],
        out_shape=[jax.ShapeDtypeStruct((ns, t, GDN_VW), F32),
                   jax.ShapeDtypeStruct((ns, GDN_HEADS, GDN_DK, GDN_DV), F32)],
        scratch_shapes=[pltpu.VMEM((GDN_HEADS, GDN_DK, GDN_DV), F32)]
        + [pltpu.VMEM((units, CHUNK, GDN_DV + GDN_DK), F32)] * 2
        + [pltpu.VMEM((units, CHUNK, CHUNK), F32)] * 2,
        compiler_params=_params(("arbitrary", "arbitrary")),
        name="gdn",
    )(rows3(q), rows3(k), rows3(kb), rows3(vb), rows3(kbe), rows3(cum),
      cumt.reshape(ns, t // CHUNK, LANES, 2 * CHUNK), rows3(qd), rows3(kd), rows3(cum), s0)


def _proj_b_body(x_ref, g_ref, wq_ref, wkv_ref, wqm_ref, qg_ref, kg_ref,
                 qlo_ref, qhi_ref, kd_ref, vd_ref, kc_ref, vc_ref, qm_ref):
    h = _rms(x_ref[...], g_ref[...]).astype(BF16)
    qm_ref[...] = jnp.dot(h, wqm_ref[...], preferred_element_type=F32)
    q = jnp.dot(h, wq_ref[...], preferred_element_type=F32)
    kv = jnp.dot(h, wkv_ref[...], preferred_element_type=F32)
    lo = lax.broadcasted_iota(jnp.int32, (1, LANES), 1) < SWA_HD
    qg = qg_ref[...] * (SWA_HD ** -0.5)
    for p in range(SWA_QW // LANES):
        sl = slice(p * LANES, (p + 1) * LANES)
        x = q[:, sl]
        x2 = x * x
        m_lo = jnp.sum(jnp.where(lo, x2, 0.0), axis=-1, keepdims=True) * (1.0 / SWA_HD)
        m_hi = jnp.sum(jnp.where(lo, 0.0, x2), axis=-1, keepdims=True) * (1.0 / SWA_HD)
        qn = x * jnp.where(lo, lax.rsqrt(m_lo + EPS), lax.rsqrt(m_hi + EPS)) * qg
        qlo_ref[:, sl] = jnp.where(lo, qn, 0.0).astype(BF16)
        qhi_ref[:, sl] = jnp.where(lo, 0.0, qn).astype(BF16)
    kg = kg_ref[...]
    for p in range(SWA_KV_HEADS):
        sl = slice(p * LANES, (p + 1) * LANES)
        kn = _rms(kv[:, sl], kg)
        kc_ref[:, sl] = kn
        kd_ref[:, sl] = kn.astype(BF16)
    v = kv[:, SWA_KV_HEADS * LANES:]
    vc_ref[...] = v
    vd_ref[...] = v.astype(BF16)


def _proj_b_call(x, g, w, tm, keep_all):
    rows = x.shape[0]
    dup_w = SWA_KV_HEADS * LANES
    tile = lambda width: pl.BlockSpec((tm, width), lambda i: (i, 0))
    cache = tile(dup_w) if keep_all else pl.BlockSpec((tm, dup_w), lambda i: (0, 0))
    cache_shape = jax.ShapeDtypeStruct((rows if keep_all else tm, dup_w), F32)
    return pl.pallas_call(
        _proj_b_body,
        grid=(rows // tm,),
        in_specs=[tile(D_MODEL), _const_spec((1, D_MODEL))]
        + [_slab_spec(w["in_b"], 0, 0, SWA_QW), _const_spec(w["b_kv"].shape),
           _slab_spec(w["in_b"], 0, (SWA_QW + 2 * SWA_KVW) // MEM_W, MEM_W)]
        + [_const_spec((1, LANES))] * 2,
        out_specs=[tile(SWA_QW)] * 2 + [tile(dup_w)] * 2 + [cache] * 2 + [tile(MEM_W)],
        out_shape=[jax.ShapeDtypeStruct((rows, SWA_QW), BF16)] * 2
        + [jax.ShapeDtypeStruct((rows, dup_w), BF16)] * 2 + [cache_shape] * 2
        + [jax.ShapeDtypeStruct((rows, MEM_W), F32)],
        compiler_params=_params(("arbitrary",)),
        name="proj_b",
    )(x, g.reshape(1, D_MODEL), w["in_b"], w["b_kv"], w["in_b"], w["q_gain2"], w["k_gain2"])


class _SwaJob:
    def __init__(self, hist_valid, seqs, tile_in_seq, refs, kwin_ref, vwin_ref):
        (self.qlo_ref, self.qhi_ref, self.k_ref, self.v_ref, self.kh_ref, self.vh_ref,
         self.sink_ref) = refs
        self.hist_valid, self.seqs, self.tile_in_seq = hist_valid, seqs, tile_in_seq
        self.kwin_ref, self.vwin_ref = kwin_ref, vwin_ref
        self.tm = self.qlo_ref.shape[0]
        self.rows = self.tm // seqs
        self.units = [(sq, j, h) for sq in range(seqs) for j in range(self.rows // CHUNK)
                      for h in range(SWA_KV_HEADS)]

    def load_history(self):
        self.kwin_ref[:, 0:WINDOW, :] = self.kh_ref[...]
        self.vwin_ref[:, 0:WINDOW, :] = self.vh_ref[...]

    def scores(self):
        c, rows = CHUNK, self.rows
        slab = lambda p: slice(p * LANES, (p + 1) * LANES)
        for sq in range(self.seqs):
            self.kwin_ref[sq, WINDOW:WINDOW + rows, :] = self.k_ref[sq * rows:(sq + 1) * rows, :]
            self.vwin_ref[sq, WINDOW:WINDOW + rows, :] = self.v_ref[sq * rows:(sq + 1) * rows, :]
        out = []
        for sq, j, h in self.units:
            r = slice(sq * rows + j * c, sq * rows + (j + 1) * c)
            qs = jnp.concatenate([self.qlo_ref[r, slab(2 * h)], self.qhi_ref[r, slab(2 * h)],
                                  self.qlo_ref[r, slab(2 * h + 1)], self.qhi_ref[r, slab(2 * h + 1)]],
                                 axis=0)
            s = _mm_nt(qs, self.kwin_ref[sq, j * c:j * c + WINDOW + c, slab(h)])
            if not self.hist_valid and j * c < WINDOW:
                key_col = lax.broadcasted_iota(jnp.int32, (SWA_GROUP * c, WINDOW + c), 1)
                first_key = self.tile_in_seq * rows + j * c - WINDOW
                s = jnp.where(key_col + first_key >= 0, s, -jnp.inf)
            out.append(s)
        return out

    def softmax(self, scores):
        c = CHUNK
        sinks = [jnp.concatenate([jnp.full((c, 1), self.sink_ref[0, h * SWA_GROUP + g], F32)
                                  for g in range(SWA_GROUP)], axis=0)
                 for h in range(SWA_KV_HEADS)]
        out = []
        for s, (sq, j, h) in zip(scores, self.units):
            m = jnp.maximum(jnp.max(s, axis=-1, keepdims=True), sinks[h])
            p = jnp.exp(s - m)
            out.append((p / (jnp.sum(p, axis=-1, keepdims=True) + jnp.exp(sinks[h] - m)))
                       .astype(BF16))
        return out

    def values(self, probs):
        c, rows = CHUNK, self.rows
        slab = lambda p: slice(p * LANES, (p + 1) * LANES)
        lo = lax.broadcasted_iota(jnp.int32, (1, LANES), 1) < SWA_HD
        slabs = {}
        for p, (sq, j, h) in zip(probs, self.units):
            o = _mm(p, self.vwin_ref[sq, j * c:j * c + WINDOW + c, slab(h)])
            for half in range(2):
                slabs[sq, j, 2 * h + half] = jnp.where(
                    lo, o[2 * half * c:(2 * half + 1) * c],
                    o[(2 * half + 1) * c:(2 * half + 2) * c]).astype(BF16)
        for ref in (self.kwin_ref, self.vwin_ref):
            ref[:, 0:WINDOW, :] = ref[:, rows:rows + WINDOW, :]
        return jnp.concatenate(
            [jnp.concatenate([slabs[sq, j, p] for p in range(SWA_QW // LANES)], axis=1)
             for sq in range(self.seqs) for j in range(rows // c)], axis=0)


def _tail_step(seqs, gated, swa, x_ref, om_ref, gate_refs, qm_ref, mk_ref, mv_ref, mqg_ref, wo_ref,
               gf_ref, wgu_ref, wd_ref, y_ref, mix_w, mix_r):
    tm = x_ref.shape[0]
    rows = tm // seqs
    d_ff = wd_ref.shape[0]
    half = d_ff // 2
    heads = [(sq, hd) for sq in range(seqs) for hd in range(MEM_HEADS)]
    rs = lambda sq: slice(sq * rows, (sq + 1) * rows)
    hs = lambda hd: slice(hd * MEM_HD, (hd + 1) * MEM_HD)

    x = x_ref[...] + jnp.dot(mix_r[...], wo_ref[...], preferred_element_type=F32)

    if swa:
        swa_scores = swa.scores()
    mqg = mqg_ref[...] * (MEM_HD ** -0.5)
    scores = [_mm_nt(_rms(qm_ref[rs(sq), hs(hd)], mqg), mk_ref[sq, :, hs(hd)])
              for sq, hd in heads]

    h = _rms(x, gf_ref[...]).astype(BF16)
    g0 = jnp.dot(h, wgu_ref[:, 0:half], preferred_element_type=F32)
    u0 = jnp.dot(h, wgu_ref[:, d_ff:d_ff + half], preferred_element_type=F32)

    probs = []
    for s in scores:
        p = jnp.exp(s - jnp.max(s, axis=-1, keepdims=True))
        probs.append((p / jnp.sum(p, axis=-1, keepdims=True)).astype(BF16))
    o_heads = [_mm(p, mv_ref[sq, :, hs(hd)]).astype(BF16) for p, (sq, hd) in zip(probs, heads)]
    o_mem = jnp.concatenate(
        [jnp.concatenate(o_heads[sq * MEM_HEADS:(sq + 1) * MEM_HEADS], axis=1)
         for sq in range(seqs)], axis=0)

    act0 = (_silu(g0) * u0).astype(BF16)
    g1 = jnp.dot(h, wgu_ref[:, half:d_ff], preferred_element_type=F32)
    u1 = jnp.dot(h, wgu_ref[:, d_ff + half:], preferred_element_type=F32)

    if gated:
        z_ref, og_ref, unperm_ref = gate_refs
        og = og_ref[...]
        parts = [(_rms(om_ref[:, hd * GDN_DV:(hd + 1) * GDN_DV], og)
                  * _silu(z_ref[:, hd * GDN_DV:(hd + 1) * GDN_DV])).astype(BF16)
                 for hd in range(GDN_HEADS)]
        mixed = jnp.concatenate(parts + [o_mem], axis=1)
        mix_w[...] = jnp.dot(unperm_ref[...], mixed, preferred_element_type=F32).astype(BF16)
    else:
        mix_w[...] = jnp.concatenate([swa.values(swa.softmax(swa_scores)), o_mem], axis=1)

    y = x + jnp.dot(act0, wd_ref[0:half, :], preferred_element_type=F32)
    act1 = (_silu(g1) * u1).astype(BF16)
    y_ref[...] = y + jnp.dot(act1, wd_ref[half:, :], preferred_element_type=F32)


def _tail_body(seqs, tiles_per_seq, hist_valid, x_ref, *refs):
    gated = hist_valid is None
    n_mixer = 4 if gated else 7
    mixer_refs, refs = refs[:n_mixer], refs[n_mixer:]
    qm_ref, mk_ref, mv_ref, mqg_ref, wo_ref, gf_ref, wgu_ref, wd_ref, y_ref = refs[:9]
    mix0_ref, mix1_ref = refs[9:11]
    i = pl.program_id(0)

    @pl.when(i == 0)
    def _():
        mix1_ref[...] = jnp.zeros_like(mix1_ref)

    if gated:
        om_ref, gate_refs, swa = mixer_refs[0], mixer_refs[1:], None
    else:
        om_ref, gate_refs = None, ()
        swa = _SwaJob(hist_valid, seqs, i % tiles_per_seq, mixer_refs, *refs[11:13])

        @pl.when(i % tiles_per_seq == 0)
        def _():
            swa.load_history()

    def step(mix_w, mix_r):
        _tail_step(seqs, gated, swa, x_ref, om_ref, gate_refs, qm_ref, mk_ref, mv_ref, mqg_ref,
                   wo_ref, gf_ref, wgu_ref, wd_ref, y_ref, mix_w, mix_r)

    @pl.when(i % 2 == 0)
    def _():
        step(mix0_ref, mix1_ref)

    @pl.when(i % 2 == 1)
    def _():
        step(mix1_ref, mix0_ref)


def _tail_call(x, gdn, swa, qm, mk, mv, mem_q_gain, w_out, g_ffn, w_gate_up, w_down, layer, tm,
               seqs):
    rows = x.shape[0]
    ns = mk.shape[1]
    n_tiles = rows // tm
    cur = lambda i: jnp.minimum(i, n_tiles - 1)
    prev = lambda i: jnp.maximum(i - 1, 0)
    tile = lambda width, which: pl.BlockSpec((tm, width), lambda i: (which(i), 0))
    tiles_per_seq = rows // (tm * ns) if seqs == 1 else 1
    per_seq = lambda *dims: pl.BlockSpec((seqs,) + dims,
                                         lambda i: (cur(i) // tiles_per_seq,) + (0,) * len(dims))
    mem = pl.BlockSpec((None, seqs, N_MEM, MEM_W),
                       lambda i: (layer, cur(i) // tiles_per_seq, 0, 0))
    if gdn:
        o_raw, z, o_gain = gdn
        hist_valid = None
        mixer_specs = [tile(GDN_VW, cur), tile(GDN_VW, cur), _const_spec((1, GDN_DV)),
                       _const_spec((tm, tm))]
        mixer_args = [o_raw, z, o_gain.reshape(1, GDN_DV), _chunk_perm(tm).T]
        scratch = []
    else:
        *mixer_args, sinks, hist_valid = swa
        dup_w = SWA_KV_HEADS * LANES
        mixer_specs = [tile(SWA_QW, cur), tile(SWA_QW, cur), tile(dup_w, cur), tile(dup_w, cur),
                       per_seq(WINDOW, dup_w), per_seq(WINDOW, dup_w),
                       pl.BlockSpec(memory_space=pltpu.SMEM)]
        mixer_args.append(sinks.reshape(1, SWA_HEADS))
        scratch = [pltpu.VMEM((seqs, WINDOW + tm // seqs, dup_w), BF16)] * 2
    return pl.pallas_call(
        functools.partial(_tail_body, seqs, tiles_per_seq, hist_valid),
        grid=(n_tiles + 1,),
        in_specs=[tile(D_MODEL, prev)] + mixer_specs + [
            tile(MEM_W, cur), mem, mem,
            _const_spec((1, MEM_HD)), _slab_spec(w_out, 0, 0, D_MODEL), _const_spec((1, D_MODEL)),
            _slab_spec(w_gate_up, layer, 0, w_gate_up.shape[2]),
            _slab_spec(w_down, layer, 0, D_MODEL),
        ],
        out_specs=tile(D_MODEL, prev),
        out_shape=jax.ShapeDtypeStruct((rows, D_MODEL), F32),
        scratch_shapes=[pltpu.VMEM((tm, w_out.shape[1]), BF16)] * 2 + scratch,
        compiler_params=_params(("arbitrary",)),
        name="tail",
    )(x, *mixer_args, qm, mk, mv, mem_q_gain.reshape(1, MEM_HD), w_out,
      g_ffn.reshape(1, D_MODEL), w_gate_up, w_down)


def _lane_row(vals, offset):
    return jnp.zeros((1, LANES), F32).at[0, offset:offset + vals.shape[0]].set(vals)


def _dup_heads(a):
    lead = a.shape[:-1]
    a = a.reshape(lead + (SWA_KV_HEADS, 1, SWA_HD))
    return jnp.broadcast_to(a, lead + (SWA_KV_HEADS, 2, SWA_HD)).reshape(lead + (SWA_KV_HEADS * LANES,))


def _undup_heads(a):
    lead = a.shape[:-1]
    return a.reshape(lead + (SWA_KV_HEADS, 2, SWA_HD))[..., 0, :]


def _trunk(x, mem_k, mem_v, gdn_conv, gdn_state, swa_k_hist, swa_v_hist, hist_valid, w):
    ns, t, _ = x.shape
    rows = ns * t
    tm = ROW_TILE
    seqs = max(1, tm // t)
    x2 = x.reshape(rows, D_MODEL)

    conv0 = jnp.pad(gdn_conv[:, :, None, :], ((0, 0), (0, 0), (SUBLANES - 1, 0), (0, 0)))
    conv0 = conv0.reshape(ns, (GDN_CONV - 1) * SUBLANES, GDN_QKV)
    z, qm, *gdn_ops, conv_new = _proj_a_call(x2, w["norm_mix"][0], w, conv0, tm, seqs)
    conv_new = conv_new[:, SUBLANES - 1::SUBLANES, :]
    o_raw, s_new = _gdn_call(gdn_ops, gdn_state, ns, min(t, GDN_TILE_ROWS))
    x2 = _tail_call(x2, (o_raw.reshape(rows, GDN_VW), z, w["o_norm_a"]), None, qm, mem_k, mem_v,
                    w["mem_q_norm"][0], w["out_a"], w["norm_ffn"][0], w["gate_up"], w["down"], 0,
                    tm, seqs)

    qlo, qhi, kd, vd, k_cache, v_cache, qm = _proj_b_call(x2, w["norm_mix"][1], w, PROJ_B_TILE,
                                                          ns > 1)
    swa = (qlo, qhi, kd, vd, _dup_heads(swa_k_hist).astype(BF16),
           _dup_heads(swa_v_hist).astype(BF16), w["sinks_b"], hist_valid)
    x2 = _tail_call(x2, None, swa, qm, mem_k, mem_v, w["mem_q_norm"][1], w["out_b"],
                    w["norm_ffn"][1], w["gate_up"], w["down"], 1, tm, seqs)

    keep = min(WINDOW, t) if ns == 1 else t
    k_new = _undup_heads(k_cache).reshape(ns, -1, SWA_KVW)[:, -keep:]
    v_new = _undup_heads(v_cache).reshape(ns, -1, SWA_KVW)[:, -keep:]
    return x2.reshape(ns, t, D_MODEL), conv_new, s_new, k_new, v_new


def kernel(x_prompt, x_sample, mem_prompt, cache_mem_k, cache_mem_v, state_gdn, state_gdn_conv, cache_swa_k, cache_swa_v, norm_mix, norm_ffn, mem_norm, w_mem_kv, mem_q_norm, mem_k_norm, w_in_a, conv_w_a, a_log, dt_bias, o_norm_a, w_out_a, w_in_b, q_norm_b, k_norm_b, sinks_b, w_out_b, w_gate_up, w_down):
    bsz = x_prompt.shape[0]
    dec = x_sample.shape[0]
    wa = w_in_a[0]
    ba_lo = GDN_QKV + GDN_VW
    qm_lo = ba_lo + 2 * GDN_HEADS
    wb = w_in_b[0]
    w = {
        "norm_mix": norm_mix, "norm_ffn": norm_ffn, "mem_q_norm": mem_q_norm,
        "in_a": w_in_a.astype(BF16), "in_b": w_in_b.astype(BF16),
        "a_ba": jnp.pad(wa[:, ba_lo:qm_lo], ((0, 0), (0, LANES - 2 * GDN_HEADS))).astype(BF16),
        "a_qm": wa[:, qm_lo:].astype(BF16),
        "b_kv": jnp.concatenate([_dup_heads(wb[:, SWA_QW:SWA_QW + SWA_KVW]),
                                 _dup_heads(wb[:, SWA_QW + SWA_KVW:SWA_QW + 2 * SWA_KVW])],
                                axis=1).astype(BF16),
        "q_gain2": jnp.tile(q_norm_b[0], 2).reshape(1, LANES),
        "k_gain2": jnp.tile(k_norm_b[0], 2).reshape(1, LANES),
        "conv_w_a": conv_w_a[0], "o_norm_a": o_norm_a[0],
        "alog_row": _lane_row(a_log[0], GDN_HEADS), "dtb_row": _lane_row(dt_bias[0], GDN_HEADS),
        "out_a": w_out_a.astype(BF16), "out_b": w_out_b.astype(BF16),
        "sinks_b": sinks_b[0],
        "gate_up": w_gate_up.astype(BF16), "down": w_down.astype(BF16),
    }

    mk, mv, mk_bf, mv_bf = _memkv_call(mem_prompt[0], mem_norm, w_mem_kv, mem_k_norm)
    depth = mk.shape[0]
    new_mem_k = mk.reshape(depth, bsz, N_MEM, MEM_HEADS, MEM_HD)
    new_mem_v = mv.reshape(depth, bsz, N_MEM, MEM_HEADS, MEM_HD)

    zero_conv = jnp.zeros((bsz, GDN_CONV - 1, GDN_QKV), F32)
    zero_state = jnp.zeros((bsz, GDN_HEADS, GDN_DK, GDN_DV), F32)
    zero_hist = jnp.zeros((bsz, WINDOW, SWA_KVW), F32)
    y_p, conv_p, state_p, k_p, v_p = _trunk(
        x_prompt, mk_bf.reshape(depth, bsz, N_MEM, MEM_W), mv_bf.reshape(depth, bsz, N_MEM, MEM_W),
        zero_conv, zero_state, zero_hist, zero_hist, False, w)

    y_s, conv_s, state_s, k_s, v_s = _trunk(
        x_sample, cache_mem_k.reshape(depth, dec, N_MEM, MEM_W).astype(BF16),
        cache_mem_v.reshape(depth, dec, N_MEM, MEM_W).astype(BF16), state_gdn_conv[0], state_gdn[0],
        cache_swa_k[0].reshape(dec, WINDOW, SWA_KVW), cache_swa_v[0].reshape(dec, WINDOW, SWA_KVW),
        True, w)

    kv_shape = lambda a: a.reshape(a.shape[0], a.shape[1], SWA_KV_HEADS, SWA_HD)[None]
    return (y_p, y_s, state_p[None], conv_p[None], state_s[None], conv_s[None],
            kv_shape(k_p), kv_shape(v_p), kv_shape(k_s), kv_shape(v_s), new_mem_k, new_mem_v)
```

```python
import functools

import jax
import jax.numpy as jnp
import numpy as np
from jax import lax
from jax.experimental import pallas as pl
from jax.experimental.pallas import tpu as pltpu

F32 = jnp.float32
BF16 = jnp.bfloat16

D_MODEL = 1024
CHUNK = 64
EPS = 1e-6
GDN_HEADS = 8
GDN_DK = 128
GDN_DV = 128
GDN_CONV = 4
GDN_QKV = GDN_HEADS * (2 * GDN_DK + GDN_DV)
GDN_VW = GDN_HEADS * GDN_DV
SWA_HEADS = 16
SWA_KV_HEADS = 4
SWA_HD = 64
SWA_GROUP = SWA_HEADS // SWA_KV_HEADS
SWA_QW = SWA_HEADS * SWA_HD
SWA_KVW = SWA_KV_HEADS * SWA_HD
WINDOW = 128
N_MEM = 256
MEM_HEADS = 4
MEM_HD = 128
MEM_W = MEM_HEADS * MEM_HD
LANES = 128
SUBLANES = 8
VMEM_LIMIT = 56 * 1024 * 1024
ROW_TILE = 256
PROJ_B_TILE = 512
GDN_TILE_ROWS = 4 * CHUNK


def _bf16_pieces(x):
    hi = x.astype(BF16)
    rest = x - hi.astype(F32)
    mid = rest.astype(BF16)
    return hi, mid, (rest - mid.astype(F32)).astype(BF16)


def _mm(a, b):
    return jnp.dot(a.astype(BF16), b.astype(BF16), preferred_element_type=F32)


def _mm_nt(a, b):
    return lax.dot_general(a.astype(BF16), b.astype(BF16), (((1,), (1,)), ((), ())),
                           preferred_element_type=F32)


def _mm_tn(a, b):
    return lax.dot_general(a.astype(BF16), b.astype(BF16), (((0,), (0,)), ((), ())),
                           preferred_element_type=F32)


def _rms(x, g):
    return x * lax.rsqrt(jnp.mean(x * x, axis=-1, keepdims=True) + EPS) * g


def _sigmoid(x):
    return 1.0 / (1.0 + jnp.exp(-x))


def _silu(x):
    hx = 0.5 * x
    return hx * jnp.tanh(hx) + hx


def _softplus(x):
    return jnp.maximum(x, 0.0) + jnp.log1p(jnp.exp(-jnp.abs(x)))


def _const_spec(shape):
    nd = len(shape)
    return pl.BlockSpec(shape, lambda *_: (0,) * nd, pipeline_mode=pl.Buffered(1))


def _slab_spec(arr, layer, col_block, width):
    return pl.BlockSpec((None, arr.shape[1], width), lambda *_: (layer, 0, col_block),
                        pipeline_mode=pl.Buffered(1))


def _params(sem):
    return pltpu.CompilerParams(dimension_semantics=sem, vmem_limit_bytes=VMEM_LIMIT)


def _memkv_body(mem_ref, g_ref, w_ref, kg_ref, mk_ref, mv_ref, mkb_ref, mvb_ref):
    h = _rms(mem_ref[...], g_ref[...])
    kv = _mm(h, w_ref[...])
    kg = kg_ref[...]
    for hd in range(MEM_HEADS):
        sl = slice(hd * MEM_HD, (hd + 1) * MEM_HD)
        mk = _rms(kv[:, sl], kg)
        mk_ref[:, sl] = mk
        mkb_ref[:, sl] = mk.astype(BF16)
    mv_ref[...] = kv[:, MEM_W:]
    mvb_ref[...] = kv[:, MEM_W:].astype(BF16)


def _memkv_call(mem, mem_norm, w_mem_kv, mem_k_norm):
    depth = w_mem_kv.shape[0]
    out = jax.ShapeDtypeStruct((depth, N_MEM, MEM_W), F32)
    out_bf = jax.ShapeDtypeStruct((depth, N_MEM, MEM_W), BF16)
    return pl.pallas_call(
        _memkv_body,
        grid=(depth,),
        in_specs=[
            pl.BlockSpec((N_MEM, D_MODEL), lambda i: (0, 0)),
            pl.BlockSpec((None, 1, D_MODEL), lambda i: (i, 0, 0)),
            pl.BlockSpec((None, D_MODEL, 2 * MEM_W), lambda i: (i, 0, 0)),
            pl.BlockSpec((None, 1, MEM_HD), lambda i: (i, 0, 0)),
        ],
        out_specs=[pl.BlockSpec((None, N_MEM, MEM_W), lambda i: (i, 0, 0))] * 4,
        out_shape=[out, out, out_bf, out_bf],
        compiler_params=_params(("arbitrary",)),
        name="memkv",
    )(mem, mem_norm.reshape(depth, 1, D_MODEL), w_mem_kv, mem_k_norm.reshape(depth, 1, MEM_HD))


def _chunk_time(pos):
    return lax.shift_right_logical(pos, 3) + SUBLANES * (pos & (SUBLANES - 1))


def _chunk_perm(tm):
    pos = np.arange(tm)
    src = (pos // CHUNK) * CHUNK + (pos % CHUNK) // SUBLANES + SUBLANES * (pos % SUBLANES)
    return jnp.asarray(src[:, None] == pos[None, :], BF16)


def _proj_a_matmuls(x_ref, g_ref, perm_ref, wqkv_ref, wz_ref, wba_ref, wqm_ref,
                    z_ref, qm_ref, raw_ref, ba_ref):
    h = _rms(x_ref[...], g_ref[...]).astype(BF16)
    h = jnp.dot(perm_ref[...], h, preferred_element_type=F32).astype(BF16)
    z_ref[...] = jnp.dot(h, wz_ref[...], preferred_element_type=F32)
    qm_ref[...] = jnp.dot(h, wqm_ref[...], preferred_element_type=F32)
    ba_ref[...] = jnp.dot(h, wba_ref[...], preferred_element_type=F32)
    raw_ref[...] = jnp.dot(h, wqkv_ref[...], preferred_element_type=F32)


def _proj_a_rows(seqs, raw_ref, ba_ref, cw_ref, alog_ref, dtb_ref,
                 q_ref, k_ref, kb_ref, qd_ref, kd_ref, vb_ref, kbe_ref, cum_ref, cumt_ref,
                 convn_ref, hist_ref):
    c = CHUNK
    tm = raw_ref.shape[0]
    rows = tm // seqs
    n_hist = (GDN_CONV - 1) * SUBLANES

    t_row = _chunk_time(lax.broadcasted_iota(jnp.int32, (c, c), 0))
    t_col = _chunk_time(lax.broadcasted_iota(jnp.int32, (c, c), 1))
    tril = (t_row >= t_col).astype(BF16)
    eye_l = (lax.broadcasted_iota(jnp.int32, (LANES, LANES), 0)
             == lax.broadcasted_iota(jnp.int32, (LANES, LANES), 1)).astype(BF16)
    sub0 = lax.broadcasted_iota(jnp.int32, (SUBLANES, LANES), 0) == 0

    ba = ba_ref[...]
    beta_all = _sigmoid(ba)
    g_all = -jnp.exp(alog_ref[...]) * _softplus(ba + dtb_ref[...])

    for j in range(tm // c):
        rs = slice(j * c, (j + 1) * c)
        s, lr = (j * c) // rows, (j * c) % rows
        cum = sum(jnp.dot(tril, piece, preferred_element_type=F32)
                  for piece in _bf16_pieces(g_all[rs]))
        cum_ref[rs, :] = cum
        cumt_ref[j] = sum(_mm_nt(eye_l, piece) for piece in _bf16_pieces(cum))
        e_cum = jnp.exp(cum)
        e_rest = jnp.exp(cum[c - 1:c, :] - cum)
        beta = beta_all[rs]

        def conv_act(lo):
            sl = slice(lo, lo + LANES)
            x = raw_ref[rs, sl]
            prev = (hist_ref[s, :, sl] if lr == 0
                    else raw_ref[j * c - n_hist:j * c, sl])
            shifted = [jnp.where(sub0,
                                 pltpu.roll(prev[i * SUBLANES:(i + 1) * SUBLANES], 1, 0),
                                 pltpu.roll(x[c - n_hist + i * SUBLANES:c - n_hist + (i + 1) * SUBLANES],
                                            1, 0)) for i in range(GDN_CONV - 1)]
            acc = x * cw_ref[GDN_CONV - 1:GDN_CONV, sl]
            for d in range(1, GDN_CONV):
                xd = jnp.concatenate(shifted[GDN_CONV - 1 - d:] + [x[0:c - d * SUBLANES]], axis=0)
                acc = acc + xd * cw_ref[GDN_CONV - 1 - d:GDN_CONV - d, sl]
            return _silu(acc)

        for hd in range(GDN_HEADS):
            hs = slice(hd * GDN_DK, (hd + 1) * GDN_DK)
            b_col = beta[:, hd:hd + 1]
            ec_col = e_cum[:, GDN_HEADS + hd:GDN_HEADS + hd + 1]
            er_col = e_rest[:, GDN_HEADS + hd:GDN_HEADS + hd + 1]
            q = conv_act(hd * GDN_DK)
            q = q * (lax.rsqrt(jnp.sum(q * q, axis=-1, keepdims=True) + EPS) * (GDN_DK ** -0.5))
            k = conv_act(GDN_HEADS * GDN_DK + hd * GDN_DK)
            k = k * lax.rsqrt(jnp.sum(k * k, axis=-1, keepdims=True) + EPS)
            v = conv_act(2 * GDN_HEADS * GDN_DK + hd * GDN_DV)
            kb = k * b_col
            q_ref[rs, hs] = q.astype(BF16)
            qd_ref[rs, hs] = (q * ec_col).astype(BF16)
            k_ref[rs, hs] = k.astype(BF16)
            kb_ref[rs, hs] = kb.astype(BF16)
            kd_ref[rs, hs] = (k * er_col).astype(BF16)
            kbe_ref[rs, hs] = kb * ec_col
            vb_ref[rs, hs] = v * b_col

    for s in range(seqs):
        tail = raw_ref[(s + 1) * rows - n_hist:(s + 1) * rows, :]
        hist_ref[s] = tail
        convn_ref[s] = tail


def _proj_a_body(seqs, tiles_per_seq, x_ref, g_ref, perm_ref, wqkv_ref, wz_ref, wba_ref, wqm_ref,
                 conv0_ref, cw_ref, alog_ref, dtb_ref,
                 z_ref, qm_ref, q_ref, k_ref, kb_ref, qd_ref, kd_ref, vb_ref, kbe_ref, cum_ref,
                 cumt_ref, convn_ref, raw0_ref, raw1_ref, ba0_ref, ba1_ref, hist_ref):
    i = pl.program_id(0)

    @pl.when(i == 0)
    def _():
        raw1_ref[...] = jnp.zeros_like(raw1_ref)
        ba1_ref[...] = jnp.zeros_like(ba1_ref)
        hist_ref[...] = jnp.zeros_like(hist_ref)

    @pl.when((i + tiles_per_seq - 1) % tiles_per_seq == 0)
    def _():
        hist_ref[...] = conv0_ref[...]

    def step(raw_w, ba_w, raw_r, ba_r):
        _proj_a_matmuls(x_ref, g_ref, perm_ref, wqkv_ref, wz_ref, wba_ref, wqm_ref,
                        z_ref, qm_ref, raw_w, ba_w)
        _proj_a_rows(seqs, raw_r, ba_r, cw_ref, alog_ref, dtb_ref,
                     q_ref, k_ref, kb_ref, qd_ref, kd_ref, vb_ref, kbe_ref, cum_ref, cumt_ref,
                     convn_ref, hist_ref)

    @pl.when(i % 2 == 0)
    def _():
        step(raw0_ref, ba0_ref, raw1_ref, ba1_ref)

    @pl.when(i % 2 == 1)
    def _():
        step(raw1_ref, ba1_ref, raw0_ref, ba0_ref)


def _proj_a_call(x, g, w, conv0, tm, seqs):
    rows = x.shape[0]
    ns = conv0.shape[0]
    n_tiles = rows // tm
    n_hist = (GDN_CONV - 1) * SUBLANES
    tiles_per_seq = rows // (ns * tm) if seqs == 1 else 1
    seq_of = (lambda i: i // tiles_per_seq) if seqs == 1 else (lambda i: i)
    cur = lambda i: jnp.minimum(i, n_tiles - 1)
    prev = lambda i: jnp.maximum(i - 1, 0)
    tile = lambda width, which: pl.BlockSpec((tm, width), lambda i: (which(i), 0))
    hist = pl.BlockSpec((seqs, n_hist, GDN_QKV), lambda i: (seq_of(prev(i)), 0, 0))
    wide_bf = jax.ShapeDtypeStruct((rows, GDN_VW), BF16)
    wide_f32 = jax.ShapeDtypeStruct((rows, GDN_VW), F32)
    perm = _chunk_perm(tm)
    return pl.pallas_call(
        functools.partial(_proj_a_body, seqs, tiles_per_seq),
        grid=(n_tiles + 1,),
        in_specs=[tile(D_MODEL, cur), _const_spec((1, D_MODEL)), _const_spec(perm.shape)]
        + [_slab_spec(w["in_a"], 0, 0, GDN_QKV), _slab_spec(w["in_a"], 0, GDN_QKV // GDN_VW, GDN_VW),
           _const_spec(w["a_ba"].shape), _const_spec(w["a_qm"].shape)]
        + [hist, _const_spec((GDN_CONV, GDN_QKV)), _const_spec((1, LANES)), _const_spec((1, LANES))],
        out_specs=[tile(GDN_VW, cur), tile(MEM_W, cur)] + [tile(GDN_VW, prev)] * 7
        + [tile(LANES, prev),
           pl.BlockSpec((tm // CHUNK, LANES, CHUNK), lambda i: (prev(i), 0, 0)), hist],
        out_shape=[wide_f32, jax.ShapeDtypeStruct((rows, MEM_W), F32)]
        + [wide_bf] * 5 + [wide_f32] * 2
        + [jax.ShapeDtypeStruct((rows, LANES), F32),
           jax.ShapeDtypeStruct((rows // CHUNK, LANES, CHUNK), F32),
           jax.ShapeDtypeStruct((ns, n_hist, GDN_QKV), F32)],
        scratch_shapes=[pltpu.VMEM((tm, GDN_QKV), F32)] * 2 + [pltpu.VMEM((tm, LANES), F32)] * 2
        + [pltpu.VMEM((seqs, n_hist, GDN_QKV), F32)],
        compiler_params=_params(("arbitrary",)),
        name="proj_a",
    )(x, g.reshape(1, D_MODEL), perm, w["in_a"], w["in_a"], w["a_ba"], w["a_qm"], conv0,
      w["conv_w_a"], w["alog_row"], w["dtb_row"])


def _gdn_chunk_solves(q_ref, k_ref, kb_ref, vb_ref, kbe_ref, cum_ref, cumt_ref, sol_ref, qk_ref):
    c = CHUNK
    nc = q_ref.shape[0] // c

    row = _chunk_time(lax.broadcasted_iota(jnp.int32, (c, c), 0))
    col = _chunk_time(lax.broadcasted_iota(jnp.int32, (c, c), 1))
    causal = row >= col
    strict = row > col
    lane = lax.broadcasted_iota(jnp.int32, (c, LANES), 1)
    low = lane < c
    eye2 = (lax.broadcasted_iota(jnp.int32, (c, LANES), 0) == (lane & (c - 1))).astype(F32)
    pad = jnp.zeros((c, LANES - c), F32)

    pairs = [(j, hd) for j in range(nc) for hd in range(GDN_HEADS)]
    rs = lambda j: slice(j * c, (j + 1) * c)
    hs = lambda hd: slice(hd * GDN_DK, (hd + 1) * GDN_DK)
    gl = lambda hd: slice(GDN_HEADS + hd, GDN_HEADS + hd + 1)

    decays = [jnp.exp(jnp.where(causal, cum_ref[rs(j), gl(hd)] - cumt_ref[j, gl(hd), :],
                                -jnp.inf)) for j, hd in pairs]
    grams = [_mm_nt(jnp.concatenate([kb_ref[rs(j), hs(hd)], q_ref[rs(j), hs(hd)]], axis=0),
                    k_ref[rs(j), hs(hd)]) for j, hd in pairs]
    yield
    ps = [jnp.concatenate([jnp.where(strict, -(g[:c] * d), 0.0), pad], axis=1)
          for g, d in zip(grams, decays)]
    for i, (g, d) in enumerate(zip(grams, decays)):
        qk_ref[i] = g[c:] * d

    ts = [eye2] * len(pairs)
    for _ in range(6):
        prods = [_mm(p[:, :c], jnp.where(low, p, t)) for t, p in zip(ts, ps)]
        yield
        ts = [t + r for t, r in zip(ts, prods)]
        ps = prods
    ns = [pltpu.roll(t - eye2, c, 1)[:, :c] for t in ts]

    rhss = [jnp.concatenate([vb_ref[rs(j), hs(hd)], kbe_ref[rs(j), hs(hd)]], axis=1)
            for j, hd in pairs]
    yield
    for i, (n, rhs) in enumerate(zip(ns, rhss)):
        sol_ref[i] = rhs + _mm(n, rhs)


def _gdn_recurrence(qd_ref, kd_ref, cum_ref, sol_ref, qk_ref, o_ref, s_ref):
    c = CHUNK
    nc = qd_ref.shape[0] // c
    heads = range(GDN_HEADS)
    rs = lambda j: slice(j * c, (j + 1) * c)
    hs = lambda hd: slice(hd * GDN_DK, (hd + 1) * GDN_DK)
    ss = [s_ref[hd] for hd in heads]
    for j in range(nc):
        sols = [sol_ref[j * GDN_HEADS + hd] for hd in heads]
        r1s = [_mm(jnp.concatenate([sols[hd][:, GDN_DV:].astype(BF16), qd_ref[rs(j), hs(hd)]],
                                   axis=0), ss[hd]) for hd in heads]
        yield
        us = [sols[hd][:, :GDN_DV] - r1s[hd][:c] for hd in heads]
        for hd in heads:
            o_ref[rs(j), hs(hd)] = r1s[hd][c:] + _mm(qk_ref[j * GDN_HEADS + hd], us[hd])
        decay = cum_ref[j * c + c - 1:j * c + c, :]
        ss = [ss[hd] * jnp.exp(decay[:, GDN_HEADS + hd:GDN_HEADS + hd + 1])
              + _mm_tn(kd_ref[rs(j), hs(hd)], us[hd]) for hd in heads]
        yield
    for hd in heads:
        s_ref[hd] = ss[hd]


def _gdn_body(skew, q_ref, k_ref, kb_ref, vb_ref, kbe_ref, cum_ref, cumt_ref, qd_ref, kd_ref,
              cumr_ref, s0_ref, o_ref, sn_ref, s_ref, sol0_ref, sol1_ref, qk0_ref, qk1_ref):
    i = pl.program_id(1)
    solve_args = (q_ref, k_ref, kb_ref, vb_ref, kbe_ref, cum_ref, cumt_ref)
    recur_args = (qd_ref, kd_ref, cumr_ref)

    if not skew:
        s_ref[...] = s0_ref[...]
        for _ in _gdn_chunk_solves(*solve_args, sol0_ref, qk0_ref):
            pass
        for _ in _gdn_recurrence(*recur_args, sol0_ref, qk0_ref, o_ref, s_ref):
            pass
        sn_ref[...] = s_ref[...]
        return

    @pl.when(i == 0)
    def _():
        sol1_ref[...] = jnp.zeros_like(sol1_ref)
        qk1_ref[...] = jnp.zeros_like(qk1_ref)
        s_ref[...] = jnp.zeros_like(s_ref)

    @pl.when(i == 1)
    def _():
        s_ref[...] = s0_ref[...]

    def step(sol_w, qk_w, sol_r, qk_r):
        solve = _gdn_chunk_solves(*solve_args, sol_w, qk_w)
        recur = _gdn_recurrence(*recur_args, sol_r, qk_r, o_ref, s_ref)
        live = [solve, recur]
        while live:
            for gen in list(live):
                if next(gen, live) is live:
                    live.remove(gen)
        sn_ref[...] = s_ref[...]

    @pl.when(i % 2 == 0)
    def _():
        step(sol0_ref, qk0_ref, sol1_ref, qk1_ref)

    @pl.when(i % 2 == 1)
    def _():
        step(sol1_ref, qk1_ref, sol0_ref, qk0_ref)


def _gdn_call(ops, s0, ns, tr):
    q, k, kb, qd, kd, vb, kbe, cum, cumt = ops
    t = q.shape[0] // ns
    n_tiles = t // tr
    skew = n_tiles > 1
    cur = (lambda i: jnp.minimum(i, n_tiles - 1)) if skew else (lambda i: i)
    prev = (lambda i: jnp.maximum(i - 1, 0)) if skew else (lambda i: i)
    seq = lambda width, which: pl.BlockSpec((None, tr, width), lambda s, i: (s, which(i), 0))
    state = pl.BlockSpec((None, GDN_HEADS, GDN_DK, GDN_DV), lambda s, i: (s, 0, 0, 0))
    rows3 = lambda a: a.reshape(ns, t, a.shape[-1])
    units = (tr // CHUNK) * GDN_HEADS
    return pl.pallas_call(
        functools.partial(_gdn_body, skew),
        grid=(ns, n_tiles + 1 if skew else n_tiles),
        in_specs=[seq(GDN_VW, cur)] * 5
        + [seq(LANES, cur),
           pl.BlockSpec((None, tr // CHUNK, LANES, CHUNK), lambda s, i: (s, cur(i), 0, 0)),
           seq(GDN_VW, prev), seq(GDN_VW, prev), seq(LANES, prev), state],
        out_specs=[seq(GDN_VW, prev), state],
        out_shape=[jax.ShapeDtypeStruct((ns, t, GDN_VW), F32),
                   jax.ShapeDtypeStruct((ns, GDN_HEADS, GDN_DK, GDN_DV), F32)],
        scratch_shapes=[pltpu.VMEM((GDN_HEADS, GDN_DK, GDN_DV), F32)]
        + [pltpu.VMEM((units, CHUNK, GDN_DV + GDN_DK), F32)] * 2
        + [pltpu.VMEM((units, CHUNK, CHUNK), F32)] * 2,
        compiler_params=_params(("arbitrary", "arbitrary")),
        name="gdn",
    )(rows3(q), rows3(k), rows3(kb), rows3(vb), rows3(kbe), rows3(cum),
      cumt.reshape(ns, t // CHUNK, LANES, CHUNK), rows3(qd), rows3(kd), rows3(cum), s0)


def _proj_b_body(x_ref, g_ref, wq_ref, wkv_ref, wqm_ref, qg_ref, kg_ref,
                 qlo_ref, qhi_ref, kd_ref, vd_ref, kc_ref, vc_ref, qm_ref):
    h = _rms(x_ref[...], g_ref[...]).astype(BF16)
    qm_ref[...] = jnp.dot(h, wqm_ref[...], preferred_element_type=F32)
    q = jnp.dot(h, wq_ref[...], preferred_element_type=F32)
    kv = jnp.dot(h, wkv_ref[...], preferred_element_type=F32)
    lo = lax.broadcasted_iota(jnp.int32, (1, LANES), 1) < SWA_HD
    qg = qg_ref[...] * (SWA_HD ** -0.5)
    for p in range(SWA_QW // LANES):
        sl = slice(p * LANES, (p + 1) * LANES)
        x = q[:, sl]
        x2 = x * x
        m_lo = jnp.sum(jnp.where(lo, x2, 0.0), axis=-1, keepdims=True) * (1.0 / SWA_HD)
        m_hi = jnp.sum(jnp.where(lo, 0.0, x2), axis=-1, keepdims=True) * (1.0 / SWA_HD)
        qn = x * jnp.where(lo, lax.rsqrt(m_lo + EPS), lax.rsqrt(m_hi + EPS)) * qg
        qlo_ref[:, sl] = jnp.where(lo, qn, 0.0).astype(BF16)
        qhi_ref[:, sl] = jnp.where(lo, 0.0, qn).astype(BF16)
    kg = kg_ref[...]
    for p in range(SWA_KV_HEADS):
        sl = slice(p * LANES, (p + 1) * LANES)
        kn = _rms(kv[:, sl], kg)
        kc_ref[:, sl] = kn
        kd_ref[:, sl] = kn.astype(BF16)
    v = kv[:, SWA_KV_HEADS * LANES:]
    vc_ref[...] = v
    vd_ref[...] = v.astype(BF16)


def _proj_b_call(x, g, w, tm, keep_all):
    rows = x.shape[0]
    dup_w = SWA_KV_HEADS * LANES
    tile = lambda width: pl.BlockSpec((tm, width), lambda i: (i, 0))
    cache = tile(dup_w) if keep_all else pl.BlockSpec((tm, dup_w), lambda i: (0, 0))
    cache_shape = jax.ShapeDtypeStruct((rows if keep_all else tm, dup_w), F32)
    return pl.pallas_call(
        _proj_b_body,
        grid=(rows // tm,),
        in_specs=[tile(D_MODEL), _const_spec((1, D_MODEL))]
        + [_slab_spec(w["in_b"], 0, 0, SWA_QW), _const_spec(w["b_kv"].shape),
           _slab_spec(w["in_b"], 0, (SWA_QW + 2 * SWA_KVW) // MEM_W, MEM_W)]
        + [_const_spec((1, LANES))] * 2,
        out_specs=[tile(SWA_QW)] * 2 + [tile(dup_w)] * 2 + [cache] * 2 + [tile(MEM_W)],
        out_shape=[jax.ShapeDtypeStruct((rows, SWA_QW), BF16)] * 2
        + [jax.ShapeDtypeStruct((rows, dup_w), BF16)] * 2 + [cache_shape] * 2
        + [jax.ShapeDtypeStruct((rows, MEM_W), F32)],
        compiler_params=_params(("arbitrary",)),
        name="proj_b",
    )(x, g.reshape(1, D_MODEL), w["in_b"], w["b_kv"], w["in_b"], w["q_gain2"], w["k_gain2"])


class _SwaJob:
    def __init__(self, hist_valid, seqs, tile_in_seq, refs, kwin_ref, vwin_ref):
        (self.qlo_ref, self.qhi_ref, self.k_ref, self.v_ref, self.kh_ref, self.vh_ref,
         self.sink_ref) = refs
        self.hist_valid, self.seqs, self.tile_in_seq = hist_valid, seqs, tile_in_seq
        self.kwin_ref, self.vwin_ref = kwin_ref, vwin_ref
        self.tm = self.qlo_ref.shape[0]
        self.rows = self.tm // seqs
        self.units = [(sq, j, h) for sq in range(seqs) for j in range(self.rows // CHUNK)
                      for h in range(SWA_KV_HEADS)]

    def load_history(self):
        self.kwin_ref[:, 0:WINDOW, :] = self.kh_ref[...]
        self.vwin_ref[:, 0:WINDOW, :] = self.vh_ref[...]

    def scores(self):
        c, rows = CHUNK, self.rows
        slab = lambda p: slice(p * LANES, (p + 1) * LANES)
        for sq in range(self.seqs):
            self.kwin_ref[sq, WINDOW:WINDOW + rows, :] = self.k_ref[sq * rows:(sq + 1) * rows, :]
            self.vwin_ref[sq, WINDOW:WINDOW + rows, :] = self.v_ref[sq * rows:(sq + 1) * rows, :]
        out = []
        for sq, j, h in self.units:
            r = slice(sq * rows + j * c, sq * rows + (j + 1) * c)
            qs = jnp.concatenate([self.qlo_ref[r, slab(2 * h)], self.qhi_ref[r, slab(2 * h)],
                                  self.qlo_ref[r, slab(2 * h + 1)], self.qhi_ref[r, slab(2 * h + 1)]],
                                 axis=0)
            s = _mm_nt(qs, self.kwin_ref[sq, j * c:j * c + WINDOW + c, slab(h)])
            if not self.hist_valid and j * c < WINDOW:
                key_col = lax.broadcasted_iota(jnp.int32, (SWA_GROUP * c, WINDOW + c), 1)
                first_key = self.tile_in_seq * rows + j * c - WINDOW
                s = jnp.where(key_col + first_key >= 0, s, -jnp.inf)
            out.append(s)
        return out

    def softmax(self, scores):
        c = CHUNK
        sinks = [jnp.concatenate([jnp.full((c, 1), self.sink_ref[0, h * SWA_GROUP + g], F32)
                                  for g in range(SWA_GROUP)], axis=0)
                 for h in range(SWA_KV_HEADS)]
        out = []
        for s, (sq, j, h) in zip(scores, self.units):
            m = jnp.maximum(jnp.max(s, axis=-1, keepdims=True), sinks[h])
            p = jnp.exp(s - m)
            out.append((p / (jnp.sum(p, axis=-1, keepdims=True) + jnp.exp(sinks[h] - m)))
                       .astype(BF16))
        return out

    def values(self, probs):
        c, rows = CHUNK, self.rows
        slab = lambda p: slice(p * LANES, (p + 1) * LANES)
        lo = lax.broadcasted_iota(jnp.int32, (1, LANES), 1) < SWA_HD
        slabs = {}
        for p, (sq, j, h) in zip(probs, self.units):
            o = _mm(p, self.vwin_ref[sq, j * c:j * c + WINDOW + c, slab(h)])
            for half in range(2):
                slabs[sq, j, 2 * h + half] = jnp.where(
                    lo, o[2 * half * c:(2 * half + 1) * c],
                    o[(2 * half + 1) * c:(2 * half + 2) * c]).astype(BF16)
        for ref in (self.kwin_ref, self.vwin_ref):
            ref[:, 0:WINDOW, :] = ref[:, rows:rows + WINDOW, :]
        return jnp.concatenate(
            [jnp.concatenate([slabs[sq, j, p] for p in range(SWA_QW // LANES)], axis=1)
             for sq in range(self.seqs) for j in range(rows // c)], axis=0)


def _tail_step(seqs, gated, swa, x_ref, om_ref, gate_refs, qm_ref, mk_ref, mv_ref, mqg_ref, wo_ref,
               gf_ref, wgu_ref, wd_ref, y_ref, mix_w, mix_r):
    tm = x_ref.shape[0]
    rows = tm // seqs
    d_ff = wd_ref.shape[0]
    half = d_ff // 2
    heads = [(sq, hd) for sq in range(seqs) for hd in range(MEM_HEADS)]
    rs = lambda sq: slice(sq * rows, (sq + 1) * rows)
    hs = lambda hd: slice(hd * MEM_HD, (hd + 1) * MEM_HD)

    x = x_ref[...] + jnp.dot(mix_r[...], wo_ref[...], preferred_element_type=F32)

    if swa:
        swa_scores = swa.scores()
    mqg = mqg_ref[...] * (MEM_HD ** -0.5)
    scores = [_mm_nt(_rms(qm_ref[rs(sq), hs(hd)], mqg), mk_ref[sq, :, hs(hd)])
              for sq, hd in heads]

    h = _rms(x, gf_ref[...]).astype(BF16)
    g0 = jnp.dot(h, wgu_ref[:, 0:half], preferred_element_type=F32)
    u0 = jnp.dot(h, wgu_ref[:, d_ff:d_ff + half], preferred_element_type=F32)

    probs = []
    for s in scores:
        p = jnp.exp(s - jnp.max(s, axis=-1, keepdims=True))
        probs.append((p / jnp.sum(p, axis=-1, keepdims=True)).astype(BF16))
    o_heads = [_mm(p, mv_ref[sq, :, hs(hd)]).astype(BF16) for p, (sq, hd) in zip(probs, heads)]
    o_mem = jnp.concatenate(
        [jnp.concatenate(o_heads[sq * MEM_HEADS:(sq + 1) * MEM_HEADS], axis=1)
         for sq in range(seqs)], axis=0)

    act0 = (_silu(g0) * u0).astype(BF16)
    g1 = jnp.dot(h, wgu_ref[:, half:d_ff], preferred_element_type=F32)
    u1 = jnp.dot(h, wgu_ref[:, d_ff + half:], preferred_element_type=F32)

    if gated:
        z_ref, og_ref, unperm_ref = gate_refs
        og = og_ref[...]
        parts = [(_rms(om_ref[:, hd * GDN_DV:(hd + 1) * GDN_DV], og)
                  * _silu(z_ref[:, hd * GDN_DV:(hd + 1) * GDN_DV])).astype(BF16)
                 for hd in range(GDN_HEADS)]
        mixed = jnp.concatenate(parts + [o_mem], axis=1)
        mix_w[...] = jnp.dot(unperm_ref[...], mixed, preferred_element_type=F32).astype(BF16)
    else:
        mix_w[...] = jnp.concatenate([swa.values(swa.softmax(swa_scores)), o_mem], axis=1)

    y = x + jnp.dot(act0, wd_ref[0:half, :], preferred_element_type=F32)
    act1 = (_silu(g1) * u1).astype(BF16)
    y_ref[...] = y + jnp.dot(act1, wd_ref[half:, :], preferred_element_type=F32)


def _tail_body(seqs, tiles_per_seq, hist_valid, x_ref, *refs):
    gated = hist_valid is None
    n_mixer = 4 if gated else 7
    mixer_refs, refs = refs[:n_mixer], refs[n_mixer:]
    qm_ref, mk_ref, mv_ref, mqg_ref, wo_ref, gf_ref, wgu_ref, wd_ref, y_ref = refs[:9]
    mix0_ref, mix1_ref = refs[9:11]
    i = pl.program_id(0)

    @pl.when(i == 0)
    def _():
        mix1_ref[...] = jnp.zeros_like(mix1_ref)

    if gated:
        om_ref, gate_refs, swa = mixer_refs[0], mixer_refs[1:], None
    else:
        om_ref, gate_refs = None, ()
        swa = _SwaJob(hist_valid, seqs, i % tiles_per_seq, mixer_refs, *refs[11:13])

        @pl.when(i % tiles_per_seq == 0)
        def _():
            swa.load_history()

    def step(mix_w, mix_r):
        _tail_step(seqs, gated, swa, x_ref, om_ref, gate_refs, qm_ref, mk_ref, mv_ref, mqg_ref,
                   wo_ref, gf_ref, wgu_ref, wd_ref, y_ref, mix_w, mix_r)

    @pl.when(i % 2 == 0)
    def _():
        step(mix0_ref, mix1_ref)

    @pl.when(i % 2 == 1)
    def _():
        step(mix1_ref, mix0_ref)


def _tail_call(x, gdn, swa, qm, mk, mv, mem_q_gain, w_out, g_ffn, w_gate_up, w_down, layer, tm,
               seqs):
    rows = x.shape[0]
    ns = mk.shape[1]
    n_tiles = rows // tm
    cur = lambda i: jnp.minimum(i, n_tiles - 1)
    prev = lambda i: jnp.maximum(i - 1, 0)
    tile = lambda width, which: pl.BlockSpec((tm, width), lambda i: (which(i), 0))
    tiles_per_seq = rows // (tm * ns) if seqs == 1 else 1
    per_seq = lambda *dims: pl.BlockSpec((seqs,) + dims,
                                         lambda i: (cur(i) // tiles_per_seq,) + (0,) * len(dims))
    mem = pl.BlockSpec((None, seqs, N_MEM, MEM_W),
                       lambda i: (layer, cur(i) // tiles_per_seq, 0, 0))
    if gdn:
        o_raw, z, o_gain = gdn
        hist_valid = None
        mixer_specs = [tile(GDN_VW, cur), tile(GDN_VW, cur), _const_spec((1, GDN_DV)),
                       _const_spec((tm, tm))]
        mixer_args = [o_raw, z, o_gain.reshape(1, GDN_DV), _chunk_perm(tm).T]
        scratch = []
    else:
        *mixer_args, sinks, hist_valid = swa
        dup_w = SWA_KV_HEADS * LANES
        mixer_specs = [tile(SWA_QW, cur), tile(SWA_QW, cur), tile(dup_w, cur), tile(dup_w, cur),
                       per_seq(WINDOW, dup_w), per_seq(WINDOW, dup_w),
                       pl.BlockSpec(memory_space=pltpu.SMEM)]
        mixer_args.append(sinks.reshape(1, SWA_HEADS))
        scratch = [pltpu.VMEM((seqs, WINDOW + tm // seqs, dup_w), BF16)] * 2
    return pl.pallas_call(
        functools.partial(_tail_body, seqs, tiles_per_seq, hist_valid),
        grid=(n_tiles + 1,),
        in_specs=[tile(D_MODEL, prev)] + mixer_specs + [
            tile(MEM_W, cur), mem, mem,
            _const_spec((1, MEM_HD)), _slab_spec(w_out, 0, 0, D_MODEL), _const_spec((1, D_MODEL)),
            _slab_spec(w_gate_up, layer, 0, w_gate_up.shape[2]),
            _slab_spec(w_down, layer, 0, D_MODEL),
        ],
        out_specs=tile(D_MODEL, prev),
        out_shape=jax.ShapeDtypeStruct((rows, D_MODEL), F32),
        scratch_shapes=[pltpu.VMEM((tm, w_out.shape[1]), BF16)] * 2 + scratch,
        compiler_params=_params(("arbitrary",)),
        name="tail",
    )(x, *mixer_args, qm, mk, mv, mem_q_gain.reshape(1, MEM_HD), w_out,
      g_ffn.reshape(1, D_MODEL), w_gate_up, w_down)


def _lane_row(vals, offset):
    return jnp.zeros((1, LANES), F32).at[0, offset:offset + vals.shape[0]].set(vals)


def _dup_heads(a):
    lead = a.shape[:-1]
    a = a.reshape(lead + (SWA_KV_HEADS, 1, SWA_HD))
    return jnp.broadcast_to(a, lead + (SWA_KV_HEADS, 2, SWA_HD)).reshape(lead + (SWA_KV_HEADS * LANES,))


def _undup_heads(a):
    lead = a.shape[:-1]
    return a.reshape(lead + (SWA_KV_HEADS, 2, SWA_HD))[..., 0, :]


def _trunk(x, mem_k, mem_v, gdn_conv, gdn_state, swa_k_hist, swa_v_hist, hist_valid, w):
    ns, t, _ = x.shape
    rows = ns * t
    tm = ROW_TILE
    seqs = max(1, tm // t)
    x2 = x.reshape(rows, D_MODEL)

    conv0 = jnp.pad(gdn_conv[:, :, None, :], ((0, 0), (0, 0), (SUBLANES - 1, 0), (0, 0)))
    conv0 = conv0.reshape(ns, (GDN_CONV - 1) * SUBLANES, GDN_QKV)
    z, qm, *gdn_ops, conv_new = _proj_a_call(x2, w["norm_mix"][0], w, conv0, tm, seqs)
    conv_new = conv_new[:, SUBLANES - 1::SUBLANES, :]
    o_raw, s_new = _gdn_call(gdn_ops, gdn_state, ns, min(t, GDN_TILE_ROWS))
    x2 = _tail_call(x2, (o_raw.reshape(rows, GDN_VW), z, w["o_norm_a"]), None, qm, mem_k, mem_v,
                    w["mem_q_norm"][0], w["out_a"], w["norm_ffn"][0], w["gate_up"], w["down"], 0,
                    tm, seqs)

    qlo, qhi, kd, vd, k_cache, v_cache, qm = _proj_b_call(x2, w["norm_mix"][1], w, PROJ_B_TILE,
                                                          ns > 1)
    swa = (qlo, qhi, kd, vd, _dup_heads(swa_k_hist).astype(BF16),
           _dup_heads(swa_v_hist).astype(BF16), w["sinks_b"], hist_valid)
    x2 = _tail_call(x2, None, swa, qm, mem_k, mem_v, w["mem_q_norm"][1], w["out_b"],
                    w["norm_ffn"][1], w["gate_up"], w["down"], 1, tm, seqs)

    keep = min(WINDOW, t) if ns == 1 else t
    k_new = _undup_heads(k_cache).reshape(ns, -1, SWA_KVW)[:, -keep:]
    v_new = _undup_heads(v_cache).reshape(ns, -1, SWA_KVW)[:, -keep:]
    return x2.reshape(ns, t, D_MODEL), conv_new, s_new, k_new, v_new


def kernel(x_prompt, x_sample, mem_prompt, cache_mem_k, cache_mem_v, state_gdn, state_gdn_conv, cache_swa_k, cache_swa_v, norm_mix, norm_ffn, mem_norm, w_mem_kv, mem_q_norm, mem_k_norm, w_in_a, conv_w_a, a_log, dt_bias, o_norm_a, w_out_a, w_in_b, q_norm_b, k_norm_b, sinks_b, w_out_b, w_gate_up, w_down):
    bsz = x_prompt.shape[0]
    dec = x_sample.shape[0]
    wa = w_in_a[0]
    ba_lo = GDN_QKV + GDN_VW
    qm_lo = ba_lo + 2 * GDN_HEADS
    wb = w_in_b[0]
    w = {
        "norm_mix": norm_mix, "norm_ffn": norm_ffn, "mem_q_norm": mem_q_norm,
        "in_a": w_in_a.astype(BF16), "in_b": w_in_b.astype(BF16),
        "a_ba": jnp.pad(wa[:, ba_lo:qm_lo], ((0, 0), (0, LANES - 2 * GDN_HEADS))).astype(BF16),
        "a_qm": wa[:, qm_lo:].astype(BF16),
        "b_kv": jnp.concatenate([_dup_heads(wb[:, SWA_QW:SWA_QW + SWA_KVW]),
                                 _dup_heads(wb[:, SWA_QW + SWA_KVW:SWA_QW + 2 * SWA_KVW])],
                                axis=1).astype(BF16),
        "q_gain2": jnp.tile(q_norm_b[0], 2).reshape(1, LANES),
        "k_gain2": jnp.tile(k_norm_b[0], 2).reshape(1, LANES),
        "conv_w_a": conv_w_a[0], "o_norm_a": o_norm_a[0],
        "alog_row": _lane_row(a_log[0], GDN_HEADS), "dtb_row": _lane_row(dt_bias[0], GDN_HEADS),
        "out_a": w_out_a.astype(BF16), "out_b": w_out_b.astype(BF16),
        "sinks_b": sinks_b[0],
        "gate_up": w_gate_up.astype(BF16), "down": w_down.astype(BF16),
    }

    mk, mv, mk_bf, mv_bf = _memkv_call(mem_prompt[0], mem_norm, w_mem_kv, mem_k_norm)
    depth = mk.shape[0]
    new_mem_k = mk.reshape(depth, bsz, N_MEM, MEM_HEADS, MEM_HD)
    new_mem_v = mv.reshape(depth, bsz, N_MEM, MEM_HEADS, MEM_HD)

    zero_conv = jnp.zeros((bsz, GDN_CONV - 1, GDN_QKV), F32)
    zero_state = jnp.zeros((bsz, GDN_HEADS, GDN_DK, GDN_DV), F32)
    zero_hist = jnp.zeros((bsz, WINDOW, SWA_KVW), F32)
    y_p, conv_p, state_p, k_p, v_p = _trunk(
        x_prompt, mk_bf.reshape(depth, bsz, N_MEM, MEM_W), mv_bf.reshape(depth, bsz, N_MEM, MEM_W),
        zero_conv, zero_state, zero_hist, zero_hist, False, w)

    y_s, conv_s, state_s, k_s, v_s = _trunk(
        x_sample, cache_mem_k.reshape(depth, dec, N_MEM, MEM_W).astype(BF16),
        cache_mem_v.reshape(depth, dec, N_MEM, MEM_W).astype(BF16), state_gdn_conv[0], state_gdn[0],
        cache_swa_k[0].reshape(dec, WINDOW, SWA_KVW), cache_swa_v[0].reshape(dec, WINDOW, SWA_KVW),
        True, w)

    kv_shape = lambda a: a.reshape(a.shape[0], a.shape[1], SWA_KV_HEADS, SWA_HD)[None]
    return (y_p, y_s, state_p[None], conv_p[None], state_s[None], conv_s[None],
            kv_shape(k_p), kv_shape(v_p), kv_shape(k_s), kv_shape(v_s), new_mem_k, new_mem_v)
```

```python
import functools

import jax
import jax.numpy as jnp
import numpy as np
from jax import lax
from jax.experimental import pallas as pl
from jax.experimental.pallas import tpu as pltpu

F32 = jnp.float32
BF16 = jnp.bfloat16

D_MODEL = 1024
CHUNK = 64
EPS = 1e-6
GDN_HEADS = 8
GDN_DK = 128
GDN_DV = 128
GDN_CONV = 4
GDN_QKV = GDN_HEADS * (2 * GDN_DK + GDN_DV)
GDN_VW = GDN_HEADS * GDN_DV
SWA_HEADS = 16
SWA_KV_HEADS = 4
SWA_HD = 64
SWA_GROUP = SWA_HEADS // SWA_KV_HEADS
SWA_QW = SWA_HEADS * SWA_HD
SWA_KVW = SWA_KV_HEADS * SWA_HD
WINDOW = 128
N_MEM = 256
MEM_HEADS = 4
MEM_HD = 128
MEM_W = MEM_HEADS * MEM_HD
LANES = 128
SUBLANES = 8
VMEM_LIMIT = 56 * 1024 * 1024
ROW_TILE = 256
PROJ_B_TILE = 512
GDN_TILE_ROWS = 4 * CHUNK
TAIL_MIN_SKEW_TILES = 2


def _bf16_pieces(x):
    hi = x.astype(BF16)
    rest = x - hi.astype(F32)
    mid = rest.astype(BF16)
    return hi, mid, (rest - mid.astype(F32)).astype(BF16)


def _mm(a, b):
    return jnp.dot(a.astype(BF16), b.astype(BF16), preferred_element_type=F32)


def _mm_nt(a, b):
    return lax.dot_general(a.astype(BF16), b.astype(BF16), (((1,), (1,)), ((), ())),
                           preferred_element_type=F32)


def _mm_tn(a, b):
    return lax.dot_general(a.astype(BF16), b.astype(BF16), (((0,), (0,)), ((), ())),
                           preferred_element_type=F32)


def _rms(x, g):
    return x * lax.rsqrt(jnp.mean(x * x, axis=-1, keepdims=True) + EPS) * g


def _sigmoid(x):
    return 1.0 / (1.0 + jnp.exp(-x))


def _silu(x):
    hx = 0.5 * x
    return hx * jnp.tanh(hx) + hx


def _softplus(x):
    return jnp.maximum(x, 0.0) + jnp.log1p(jnp.exp(-jnp.abs(x)))


def _const_spec(shape):
    nd = len(shape)
    return pl.BlockSpec(shape, lambda *_: (0,) * nd, pipeline_mode=pl.Buffered(1))


def _slab_spec(arr, layer, col_block, width):
    return pl.BlockSpec((None, arr.shape[1], width), lambda *_: (layer, 0, col_block),
                        pipeline_mode=pl.Buffered(1))


def _params(sem):
    return pltpu.CompilerParams(dimension_semantics=sem, vmem_limit_bytes=VMEM_LIMIT)


def _memkv_body(mem_ref, g_ref, w_ref, kg_ref, mk_ref, mv_ref, mkb_ref, mvb_ref):
    h = _rms(mem_ref[...], g_ref[...])
    kv = _mm(h, w_ref[...])
    kg = kg_ref[...]
    for hd in range(MEM_HEADS):
        sl = slice(hd * MEM_HD, (hd + 1) * MEM_HD)
        mk = _rms(kv[:, sl], kg)
        mk_ref[:, sl] = mk
        mkb_ref[:, sl] = mk.astype(BF16)
    mv_ref[...] = kv[:, MEM_W:]
    mvb_ref[...] = kv[:, MEM_W:].astype(BF16)


def _memkv_call(mem, mem_norm, w_mem_kv, mem_k_norm):
    depth = w_mem_kv.shape[0]
    out = jax.ShapeDtypeStruct((depth, N_MEM, MEM_W), F32)
    out_bf = jax.ShapeDtypeStruct((depth, N_MEM, MEM_W), BF16)
    return pl.pallas_call(
        _memkv_body,
        grid=(depth,),
        in_specs=[
            pl.BlockSpec((N_MEM, D_MODEL), lambda i: (0, 0)),
            pl.BlockSpec((None, 1, D_MODEL), lambda i: (i, 0, 0)),
            pl.BlockSpec((None, D_MODEL, 2 * MEM_W), lambda i: (i, 0, 0)),
            pl.BlockSpec((None, 1, MEM_HD), lambda i: (i, 0, 0)),
        ],
        out_specs=[pl.BlockSpec((None, N_MEM, MEM_W), lambda i: (i, 0, 0))] * 4,
        out_shape=[out, out, out_bf, out_bf],
        compiler_params=_params(("arbitrary",)),
        name="memkv",
    )(mem, mem_norm.reshape(depth, 1, D_MODEL), w_mem_kv, mem_k_norm.reshape(depth, 1, MEM_HD))


def _chunk_time(pos):
    return lax.shift_right_logical(pos, 3) + SUBLANES * (pos & (SUBLANES - 1))


def _chunk_perm(tm):
    pos = np.arange(tm)
    src = (pos // CHUNK) * CHUNK + (pos % CHUNK) // SUBLANES + SUBLANES * (pos % SUBLANES)
    return jnp.asarray(src[:, None] == pos[None, :], BF16)


def _proj_a_matmuls(x_ref, g_ref, perm_ref, wqkv_ref, wz_ref, wba_ref, wqm_ref,
                    z_ref, qm_ref, raw_ref, ba_ref):
    h = _rms(x_ref[...], g_ref[...]).astype(BF16)
    h = jnp.dot(perm_ref[...], h, preferred_element_type=F32).astype(BF16)
    z_ref[...] = jnp.dot(h, wz_ref[...], preferred_element_type=F32)
    qm_ref[...] = jnp.dot(h, wqm_ref[...], preferred_element_type=F32)
    ba_ref[...] = jnp.dot(h, wba_ref[...], preferred_element_type=F32)
    raw_ref[...] = jnp.dot(h, wqkv_ref[...], preferred_element_type=F32)


def _proj_a_rows(seqs, raw_ref, ba_ref, cw_ref, alog_ref, dtb_ref,
                 q_ref, k_ref, kb_ref, qd_ref, kd_ref, vb_ref, kbe_ref, cum_ref, cumt_ref,
                 convn_ref, hist_ref):
    c = CHUNK
    tm = raw_ref.shape[0]
    rows = tm // seqs
    n_hist = (GDN_CONV - 1) * SUBLANES

    t_row = _chunk_time(lax.broadcasted_iota(jnp.int32, (c, c), 0))
    t_col = _chunk_time(lax.broadcasted_iota(jnp.int32, (c, c), 1))
    tril = (t_row >= t_col).astype(BF16)
    eye_l = (lax.broadcasted_iota(jnp.int32, (LANES, LANES), 0)
             == lax.broadcasted_iota(jnp.int32, (LANES, LANES), 1)).astype(BF16)
    sub0 = lax.broadcasted_iota(jnp.int32, (SUBLANES, LANES), 0) == 0

    ba = ba_ref[...]
    beta_all = _sigmoid(ba)
    g_all = -jnp.exp(alog_ref[...]) * _softplus(ba + dtb_ref[...])

    for j in range(tm // c):
        rs = slice(j * c, (j + 1) * c)
        s, lr = (j * c) // rows, (j * c) % rows
        cum = sum(jnp.dot(tril, piece, preferred_element_type=F32)
                  for piece in _bf16_pieces(g_all[rs]))
        cum_ref[rs, :] = cum
        cumt_ref[j] = sum(_mm_nt(eye_l, piece) for piece in _bf16_pieces(cum))
        e_cum = jnp.exp(cum)
        e_rest = jnp.exp(cum[c - 1:c, :] - cum)
        beta = beta_all[rs]

        def conv_act(lo):
            sl = slice(lo, lo + LANES)
            x = raw_ref[rs, sl]
            prev = (hist_ref[s, :, sl] if lr == 0
                    else raw_ref[j * c - n_hist:j * c, sl])
            shifted = [jnp.where(sub0,
                                 pltpu.roll(prev[i * SUBLANES:(i + 1) * SUBLANES], 1, 0),
                                 pltpu.roll(x[c - n_hist + i * SUBLANES:c - n_hist + (i + 1) * SUBLANES],
                                            1, 0)) for i in range(GDN_CONV - 1)]
            acc = x * cw_ref[GDN_CONV - 1:GDN_CONV, sl]
            for d in range(1, GDN_CONV):
                xd = jnp.concatenate(shifted[GDN_CONV - 1 - d:] + [x[0:c - d * SUBLANES]], axis=0)
                acc = acc + xd * cw_ref[GDN_CONV - 1 - d:GDN_CONV - d, sl]
            return _silu(acc)

        for hd in range(GDN_HEADS):
            hs = slice(hd * GDN_DK, (hd + 1) * GDN_DK)
            b_col = beta[:, hd:hd + 1]
            ec_col = e_cum[:, GDN_HEADS + hd:GDN_HEADS + hd + 1]
            er_col = e_rest[:, GDN_HEADS + hd:GDN_HEADS + hd + 1]
            q = conv_act(hd * GDN_DK)
            q = q * (lax.rsqrt(jnp.sum(q * q, axis=-1, keepdims=True) + EPS) * (GDN_DK ** -0.5))
            k = conv_act(GDN_HEADS * GDN_DK + hd * GDN_DK)
            k = k * lax.rsqrt(jnp.sum(k * k, axis=-1, keepdims=True) + EPS)
            v = conv_act(2 * GDN_HEADS * GDN_DK + hd * GDN_DV)
            kb = k * b_col
            q_ref[rs, hs] = q.astype(BF16)
            qd_ref[rs, hs] = (q * ec_col).astype(BF16)
            k_ref[rs, hs] = k.astype(BF16)
            kb_ref[rs, hs] = kb.astype(BF16)
            kd_ref[rs, hs] = (k * er_col).astype(BF16)
            kbe_ref[rs, hs] = kb * ec_col
            vb_ref[rs, hs] = v * b_col

    for s in range(seqs):
        tail = raw_ref[(s + 1) * rows - n_hist:(s + 1) * rows, :]
        hist_ref[s] = tail
        convn_ref[s] = tail


def _proj_a_body(seqs, tiles_per_seq, x_ref, g_ref, perm_ref, wqkv_ref, wz_ref, wba_ref, wqm_ref,
                 conv0_ref, cw_ref, alog_ref, dtb_ref,
                 z_ref, qm_ref, q_ref, k_ref, kb_ref, qd_ref, kd_ref, vb_ref, kbe_ref, cum_ref,
                 cumt_ref, convn_ref, raw0_ref, raw1_ref, ba0_ref, ba1_ref, hist_ref):
    i = pl.program_id(0)

    @pl.when(i == 0)
    def _():
        raw1_ref[...] = jnp.zeros_like(raw1_ref)
        ba1_ref[...] = jnp.zeros_like(ba1_ref)
        hist_ref[...] = jnp.zeros_like(hist_ref)

    @pl.when((i + tiles_per_seq - 1) % tiles_per_seq == 0)
    def _():
        hist_ref[...] = conv0_ref[...]

    def step(raw_w, ba_w, raw_r, ba_r):
        _proj_a_matmuls(x_ref, g_ref, perm_ref, wqkv_ref, wz_ref, wba_ref, wqm_ref,
                        z_ref, qm_ref, raw_w, ba_w)
        _proj_a_rows(seqs, raw_r, ba_r, cw_ref, alog_ref, dtb_ref,
                     q_ref, k_ref, kb_ref, qd_ref, kd_ref, vb_ref, kbe_ref, cum_ref, cumt_ref,
                     convn_ref, hist_ref)

    @pl.when(i % 2 == 0)
    def _():
        step(raw0_ref, ba0_ref, raw1_ref, ba1_ref)

    @pl.when(i % 2 == 1)
    def _():
        step(raw1_ref, ba1_ref, raw0_ref, ba0_ref)


def _proj_a_call(x, g, w, conv0, tm, seqs):
    rows = x.shape[0]
    ns = conv0.shape[0]
    n_tiles = rows // tm
    n_hist = (GDN_CONV - 1) * SUBLANES
    tiles_per_seq = rows // (ns * tm) if seqs == 1 else 1
    seq_of = (lambda i: i // tiles_per_seq) if seqs == 1 else (lambda i: i)
    cur = lambda i: jnp.minimum(i, n_tiles - 1)
    prev = lambda i: jnp.maximum(i - 1, 0)
    tile = lambda width, which: pl.BlockSpec((tm, width), lambda i: (which(i), 0))
    hist = pl.BlockSpec((seqs, n_hist, GDN_QKV), lambda i: (seq_of(prev(i)), 0, 0))
    wide_bf = jax.ShapeDtypeStruct((rows, GDN_VW), BF16)
    wide_f32 = jax.ShapeDtypeStruct((rows, GDN_VW), F32)
    perm = _chunk_perm(tm)
    return pl.pallas_call(
        functools.partial(_proj_a_body, seqs, tiles_per_seq),
        grid=(n_tiles + 1,),
        in_specs=[tile(D_MODEL, cur), _const_spec((1, D_MODEL)), _const_spec(perm.shape)]
        + [_slab_spec(w["in_a"], 0, 0, GDN_QKV), _slab_spec(w["in_a"], 0, GDN_QKV // GDN_VW, GDN_VW),
           _const_spec(w["a_ba"].shape), _const_spec(w["a_qm"].shape)]
        + [hist, _const_spec((GDN_CONV, GDN_QKV)), _const_spec((1, LANES)), _const_spec((1, LANES))],
        out_specs=[tile(GDN_VW, cur), tile(MEM_W, cur)] + [tile(GDN_VW, prev)] * 7
        + [tile(LANES, prev),
           pl.BlockSpec((tm // CHUNK, LANES, CHUNK), lambda i: (prev(i), 0, 0)), hist],
        out_shape=[wide_f32, jax.ShapeDtypeStruct((rows, MEM_W), F32)]
        + [wide_bf] * 5 + [wide_f32] * 2
        + [jax.ShapeDtypeStruct((rows, LANES), F32),
           jax.ShapeDtypeStruct((rows // CHUNK, LANES, CHUNK), F32),
           jax.ShapeDtypeStruct((ns, n_hist, GDN_QKV), F32)],
        scratch_shapes=[pltpu.VMEM((tm, GDN_QKV), F32)] * 2 + [pltpu.VMEM((tm, LANES), F32)] * 2
        + [pltpu.VMEM((seqs, n_hist, GDN_QKV), F32)],
        compiler_params=_params(("arbitrary",)),
        name="proj_a",
    )(x, g.reshape(1, D_MODEL), perm, w["in_a"], w["in_a"], w["a_ba"], w["a_qm"], conv0,
      w["conv_w_a"], w["alog_row"], w["dtb_row"])


def _gdn_chunk_solves(q_ref, k_ref, kb_ref, vb_ref, kbe_ref, cum_ref, cumt_ref, sol_ref, qk_ref):
    c = CHUNK
    nc = q_ref.shape[0] // c

    row = _chunk_time(lax.broadcasted_iota(jnp.int32, (c, c), 0))
    col = _chunk_time(lax.broadcasted_iota(jnp.int32, (c, c), 1))
    causal = row >= col
    strict = row > col
    lane = lax.broadcasted_iota(jnp.int32, (c, LANES), 1)
    low = lane < c
    eye2 = (lax.broadcasted_iota(jnp.int32, (c, LANES), 0) == (lane & (c - 1))).astype(F32)
    pad = jnp.zeros((c, LANES - c), F32)

    pairs = [(j, hd) for j in range(nc) for hd in range(GDN_HEADS)]
    rs = lambda j: slice(j * c, (j + 1) * c)
    hs = lambda hd: slice(hd * GDN_DK, (hd + 1) * GDN_DK)
    gl = lambda hd: slice(GDN_HEADS + hd, GDN_HEADS + hd + 1)

    decays = [jnp.exp(jnp.where(causal, cum_ref[rs(j), gl(hd)] - cumt_ref[j, gl(hd), :],
                                -jnp.inf)) for j, hd in pairs]
    grams = [_mm_nt(jnp.concatenate([kb_ref[rs(j), hs(hd)], q_ref[rs(j), hs(hd)]], axis=0),
                    k_ref[rs(j), hs(hd)]) for j, hd in pairs]
    yield
    ps = [jnp.concatenate([jnp.where(strict, -(g[:c] * d), 0.0), pad], axis=1)
          for g, d in zip(grams, decays)]
    for i, (g, d) in enumerate(zip(grams, decays)):
        qk_ref[i] = g[c:] * d

    ts = [eye2] * len(pairs)
    for _ in range(6):
        prods = [_mm(p[:, :c], jnp.where(low, p, t)) for t, p in zip(ts, ps)]
        yield
        ts = [t + r for t, r in zip(ts, prods)]
        ps = prods
    ns = [pltpu.roll(t - eye2, c, 1)[:, :c] for t in ts]

    rhss = [jnp.concatenate([vb_ref[rs(j), hs(hd)], kbe_ref[rs(j), hs(hd)]], axis=1)
            for j, hd in pairs]
    yield
    for i, (n, rhs) in enumerate(zip(ns, rhss)):
        sol_ref[i] = rhs + _mm(n, rhs)


def _gdn_recurrence(qd_ref, kd_ref, cum_ref, sol_ref, qk_ref, o_ref, s_ref):
    c = CHUNK
    nc = qd_ref.shape[0] // c
    heads = range(GDN_HEADS)
    rs = lambda j: slice(j * c, (j + 1) * c)
    hs = lambda hd: slice(hd * GDN_DK, (hd + 1) * GDN_DK)
    ss = [s_ref[hd] for hd in heads]
    for j in range(nc):
        sols = [sol_ref[j * GDN_HEADS + hd] for hd in heads]
        r1s = [_mm(jnp.concatenate([sols[hd][:, GDN_DV:].astype(BF16), qd_ref[rs(j), hs(hd)]],
                                   axis=0), ss[hd]) for hd in heads]
        yield
        us = [sols[hd][:, :GDN_DV] - r1s[hd][:c] for hd in heads]
        for hd in heads:
            o_ref[rs(j), hs(hd)] = r1s[hd][c:] + _mm(qk_ref[j * GDN_HEADS + hd], us[hd])
        decay = cum_ref[j * c + c - 1:j * c + c, :]
        ss = [ss[hd] * jnp.exp(decay[:, GDN_HEADS + hd:GDN_HEADS + hd + 1])
              + _mm_tn(kd_ref[rs(j), hs(hd)], us[hd]) for hd in heads]
        yield
    for hd in heads:
        s_ref[hd] = ss[hd]


def _gdn_body(skew, q_ref, k_ref, kb_ref, vb_ref, kbe_ref, cum_ref, cumt_ref, qd_ref, kd_ref,
              cumr_ref, s0_ref, o_ref, sn_ref, s_ref, sol0_ref, sol1_ref, qk0_ref, qk1_ref):
    i = pl.program_id(1)
    solve_args = (q_ref, k_ref, kb_ref, vb_ref, kbe_ref, cum_ref, cumt_ref)
    recur_args = (qd_ref, kd_ref, cumr_ref)

    if not skew:
        s_ref[...] = s0_ref[...]
        for _ in _gdn_chunk_solves(*solve_args, sol0_ref, qk0_ref):
            pass
        for _ in _gdn_recurrence(*recur_args, sol0_ref, qk0_ref, o_ref, s_ref):
            pass
        sn_ref[...] = s_ref[...]
        return

    @pl.when(i == 0)
    def _():
        sol1_ref[...] = jnp.zeros_like(sol1_ref)
        qk1_ref[...] = jnp.zeros_like(qk1_ref)
        s_ref[...] = jnp.zeros_like(s_ref)

    @pl.when(i == 1)
    def _():
        s_ref[...] = s0_ref[...]

    def step(sol_w, qk_w, sol_r, qk_r):
        solve = _gdn_chunk_solves(*solve_args, sol_w, qk_w)
        recur = _gdn_recurrence(*recur_args, sol_r, qk_r, o_ref, s_ref)
        live = [solve, recur]
        while live:
            for gen in list(live):
                if next(gen, live) is live:
                    live.remove(gen)
        sn_ref[...] = s_ref[...]

    @pl.when(i % 2 == 0)
    def _():
        step(sol0_ref, qk0_ref, sol1_ref, qk1_ref)

    @pl.when(i % 2 == 1)
    def _():
        step(sol1_ref, qk1_ref, sol0_ref, qk0_ref)


def _gdn_call(ops, s0, ns, tr):
    q, k, kb, qd, kd, vb, kbe, cum, cumt = ops
    t = q.shape[0] // ns
    n_tiles = t // tr
    skew = n_tiles > 1
    cur = (lambda i: jnp.minimum(i, n_tiles - 1)) if skew else (lambda i: i)
    prev = (lambda i: jnp.maximum(i - 1, 0)) if skew else (lambda i: i)
    seq = lambda width, which: pl.BlockSpec((None, tr, width), lambda s, i: (s, which(i), 0))
    state = pl.BlockSpec((None, GDN_HEADS, GDN_DK, GDN_DV), lambda s, i: (s, 0, 0, 0))
    rows3 = lambda a: a.reshape(ns, t, a.shape[-1])
    units = (tr // CHUNK) * GDN_HEADS
    return pl.pallas_call(
        functools.partial(_gdn_body, skew),
        grid=(ns, n_tiles + 1 if skew else n_tiles),
        in_specs=[seq(GDN_VW, cur)] * 5
        + [seq(LANES, cur),
           pl.BlockSpec((None, tr // CHUNK, LANES, CHUNK), lambda s, i: (s, cur(i), 0, 0)),
           seq(GDN_VW, prev), seq(GDN_VW, prev), seq(LANES, prev), state],
        out_specs=[seq(GDN_VW, prev), state],
        out_shape=[jax.ShapeDtypeStruct((ns, t, GDN_VW), F32),
                   jax.ShapeDtypeStruct((ns, GDN_HEADS, GDN_DK, GDN_DV), F32)],
        scratch_shapes=[pltpu.VMEM((GDN_HEADS, GDN_DK, GDN_DV), F32)]
        + [pltpu.VMEM((units, CHUNK, GDN_DV + GDN_DK), F32)] * 2
        + [pltpu.VMEM((units, CHUNK, CHUNK), F32)] * 2,
        compiler_params=_params(("arbitrary", "arbitrary")),
        name="gdn",
    )(rows3(q), rows3(k), rows3(kb), rows3(vb), rows3(kbe), rows3(cum),
      cumt.reshape(ns, t // CHUNK, LANES, CHUNK), rows3(qd), rows3(kd), rows3(cum), s0)


def _proj_b_body(x_ref, g_ref, wq_ref, wkv_ref, wqm_ref, qg_ref, kg_ref,
                 qlo_ref, qhi_ref, kd_ref, vd_ref, kc_ref, vc_ref, qm_ref):
    h = _rms(x_ref[...], g_ref[...]).astype(BF16)
    qm_ref[...] = jnp.dot(h, wqm_ref[...], preferred_element_type=F32)
    q = jnp.dot(h, wq_ref[...], preferred_element_type=F32)
    kv = jnp.dot(h, wkv_ref[...], preferred_element_type=F32)
    lo = lax.broadcasted_iota(jnp.int32, (1, LANES), 1) < SWA_HD
    qg = qg_ref[...] * (SWA_HD ** -0.5)
    for p in range(SWA_QW // LANES):
        sl = slice(p * LANES, (p + 1) * LANES)
        x = q[:, sl]
        x2 = x * x
        m_lo = jnp.sum(jnp.where(lo, x2, 0.0), axis=-1, keepdims=True) * (1.0 / SWA_HD)
        m_hi = jnp.sum(jnp.where(lo, 0.0, x2), axis=-1, keepdims=True) * (1.0 / SWA_HD)
        qn = x * jnp.where(lo, lax.rsqrt(m_lo + EPS), lax.rsqrt(m_hi + EPS)) * qg
        qlo_ref[:, sl] = jnp.where(lo, qn, 0.0).astype(BF16)
        qhi_ref[:, sl] = jnp.where(lo, 0.0, qn).astype(BF16)
    kg = kg_ref[...]
    for p in range(SWA_KV_HEADS):
        sl = slice(p * LANES, (p + 1) * LANES)
        kn = _rms(kv[:, sl], kg)
        kc_ref[:, sl] = kn
        kd_ref[:, sl] = kn.astype(BF16)
    v = kv[:, SWA_KV_HEADS * LANES:]
    vc_ref[...] = v
    vd_ref[...] = v.astype(BF16)


def _proj_b_call(x, g, w, tm, keep_all):
    rows = x.shape[0]
    dup_w = SWA_KV_HEADS * LANES
    tile = lambda width: pl.BlockSpec((tm, width), lambda i: (i, 0))
    cache = tile(dup_w) if keep_all else pl.BlockSpec((tm, dup_w), lambda i: (0, 0))
    cache_shape = jax.ShapeDtypeStruct((rows if keep_all else tm, dup_w), F32)
    return pl.pallas_call(
        _proj_b_body,
        grid=(rows // tm,),
        in_specs=[tile(D_MODEL), _const_spec((1, D_MODEL))]
        + [_slab_spec(w["in_b"], 0, 0, SWA_QW), _const_spec(w["b_kv"].shape),
           _slab_spec(w["in_b"], 0, (SWA_QW + 2 * SWA_KVW) // MEM_W, MEM_W)]
        + [_const_spec((1, LANES))] * 2,
        out_specs=[tile(SWA_QW)] * 2 + [tile(dup_w)] * 2 + [cache] * 2 + [tile(MEM_W)],
        out_shape=[jax.ShapeDtypeStruct((rows, SWA_QW), BF16)] * 2
        + [jax.ShapeDtypeStruct((rows, dup_w), BF16)] * 2 + [cache_shape] * 2
        + [jax.ShapeDtypeStruct((rows, MEM_W), F32)],
        compiler_params=_params(("arbitrary",)),
        name="proj_b",
    )(x, g.reshape(1, D_MODEL), w["in_b"], w["b_kv"], w["in_b"], w["q_gain2"], w["k_gain2"])


class _SwaJob:
    def __init__(self, hist_valid, seqs, tile_in_seq, refs, kwin_ref, vwin_ref):
        (self.qlo_ref, self.qhi_ref, self.k_ref, self.v_ref, self.kh_ref, self.vh_ref,
         self.sink_ref) = refs
        self.hist_valid, self.seqs, self.tile_in_seq = hist_valid, seqs, tile_in_seq
        self.kwin_ref, self.vwin_ref = kwin_ref, vwin_ref
        self.tm = self.qlo_ref.shape[0]
        self.rows = self.tm // seqs
        self.units = [(sq, j, h) for sq in range(seqs) for j in range(self.rows // CHUNK)
                      for h in range(SWA_KV_HEADS)]

    def load_history(self):
        self.kwin_ref[:, 0:WINDOW, :] = self.kh_ref[...]
        self.vwin_ref[:, 0:WINDOW, :] = self.vh_ref[...]

    def scores(self):
        c, rows = CHUNK, self.rows
        slab = lambda p: slice(p * LANES, (p + 1) * LANES)
        for sq in range(self.seqs):
            self.kwin_ref[sq, WINDOW:WINDOW + rows, :] = self.k_ref[sq * rows:(sq + 1) * rows, :]
            self.vwin_ref[sq, WINDOW:WINDOW + rows, :] = self.v_ref[sq * rows:(sq + 1) * rows, :]
        out = []
        for sq, j, h in self.units:
            r = slice(sq * rows + j * c, sq * rows + (j + 1) * c)
            qs = jnp.concatenate([self.qlo_ref[r, slab(2 * h)], self.qhi_ref[r, slab(2 * h)],
                                  self.qlo_ref[r, slab(2 * h + 1)], self.qhi_ref[r, slab(2 * h + 1)]],
                                 axis=0)
            s = _mm_nt(qs, self.kwin_ref[sq, j * c:j * c + WINDOW + c, slab(h)])
            if not self.hist_valid and j * c < WINDOW:
                key_col = lax.broadcasted_iota(jnp.int32, (SWA_GROUP * c, WINDOW + c), 1)
                first_key = self.tile_in_seq * rows + j * c - WINDOW
                s = jnp.where(key_col + first_key >= 0, s, -jnp.inf)
            out.append(s)
        return out

    def softmax(self, scores):
        c = CHUNK
        sinks = [jnp.concatenate([jnp.full((c, 1), self.sink_ref[0, h * SWA_GROUP + g], F32)
                                  for g in range(SWA_GROUP)], axis=0)
                 for h in range(SWA_KV_HEADS)]
        out = []
        for s, (sq, j, h) in zip(scores, self.units):
            m = jnp.maximum(jnp.max(s, axis=-1, keepdims=True), sinks[h])
            p = jnp.exp(s - m)
            out.append((p / (jnp.sum(p, axis=-1, keepdims=True) + jnp.exp(sinks[h] - m)))
                       .astype(BF16))
        return out

    def values(self, probs):
        c, rows = CHUNK, self.rows
        slab = lambda p: slice(p * LANES, (p + 1) * LANES)
        lo = lax.broadcasted_iota(jnp.int32, (1, LANES), 1) < SWA_HD
        slabs = {}
        for p, (sq, j, h) in zip(probs, self.units):
            o = _mm(p, self.vwin_ref[sq, j * c:j * c + WINDOW + c, slab(h)])
            for half in range(2):
                slabs[sq, j, 2 * h + half] = jnp.where(
                    lo, o[2 * half * c:(2 * half + 1) * c],
                    o[(2 * half + 1) * c:(2 * half + 2) * c]).astype(BF16)
        for ref in (self.kwin_ref, self.vwin_ref):
            ref[:, 0:WINDOW, :] = ref[:, rows:rows + WINDOW, :]
        return jnp.concatenate(
            [jnp.concatenate([slabs[sq, j, p] for p in range(SWA_QW // LANES)], axis=1)
             for sq in range(self.seqs) for j in range(rows // c)], axis=0)


def _tail_step(seqs, gated, swa, x_ref, om_ref, gate_refs, qm_ref, mk_ref, mv_ref, mqg_ref, wo_ref,
               gf_ref, wgu_ref, wd_ref, y_ref, mix_w, mix_r):
    tm = x_ref.shape[0]
    rows = tm // seqs
    d_ff = wd_ref.shape[0]
    half = d_ff // 2
    heads = [(sq, hd) for sq in range(seqs) for hd in range(MEM_HEADS)]
    rs = lambda sq: slice(sq * rows, (sq + 1) * rows)
    hs = lambda hd: slice(hd * MEM_HD, (hd + 1) * MEM_HD)

    def ffn():
        x = x_ref[...] + jnp.dot(mix_r[...], wo_ref[...], preferred_element_type=F32)
        yield
        h = _rms(x, gf_ref[...]).astype(BF16)
        g0 = jnp.dot(h, wgu_ref[:, 0:half], preferred_element_type=F32)
        u0 = jnp.dot(h, wgu_ref[:, d_ff:d_ff + half], preferred_element_type=F32)
        yield
        act0 = (_silu(g0) * u0).astype(BF16)
        g1 = jnp.dot(h, wgu_ref[:, half:d_ff], preferred_element_type=F32)
        u1 = jnp.dot(h, wgu_ref[:, d_ff + half:], preferred_element_type=F32)
        yield
        y = x + jnp.dot(act0, wd_ref[0:half, :], preferred_element_type=F32)
        act1 = (_silu(g1) * u1).astype(BF16)
        y_ref[...] = y + jnp.dot(act1, wd_ref[half:, :], preferred_element_type=F32)

    def operand():
        if swa:
            swa_scores = swa.scores()
        mqg = mqg_ref[...] * (MEM_HD ** -0.5)
        scores = [_mm_nt(_rms(qm_ref[rs(sq), hs(hd)], mqg), mk_ref[sq, :, hs(hd)])
                  for sq, hd in heads]
        yield
        probs = []
        for s in scores:
            p = jnp.exp(s - jnp.max(s, axis=-1, keepdims=True))
            probs.append((p / jnp.sum(p, axis=-1, keepdims=True)).astype(BF16))
        o_heads = [_mm(p, mv_ref[sq, :, hs(hd)]).astype(BF16)
                   for p, (sq, hd) in zip(probs, heads)]
        o_mem = jnp.concatenate(
            [jnp.concatenate(o_heads[sq * MEM_HEADS:(sq + 1) * MEM_HEADS], axis=1)
             for sq in range(seqs)], axis=0)
        yield
        if gated:
            z_ref, og_ref, unperm_ref = gate_refs
            og = og_ref[...]
            parts = [(_rms(om_ref[:, hd * GDN_DV:(hd + 1) * GDN_DV], og)
                      * _silu(z_ref[:, hd * GDN_DV:(hd + 1) * GDN_DV])).astype(BF16)
                     for hd in range(GDN_HEADS)]
            mixed = jnp.concatenate(parts + [o_mem], axis=1)
            mix_w[...] = jnp.dot(unperm_ref[...], mixed, preferred_element_type=F32).astype(BF16)
        else:
            mix_w[...] = jnp.concatenate([swa.values(swa.softmax(swa_scores)), o_mem], axis=1)

    if mix_w is mix_r:
        for job in (operand(), ffn()):
            for _ in job:
                pass
        return
    live = [ffn(), operand()]
    while live:
        for gen in list(live):
            if next(gen, live) is live:
                live.remove(gen)


def _tail_body(seqs, tiles_per_seq, skew, hist_valid, x_ref, *refs):
    gated = hist_valid is None
    n_mixer = 4 if gated else 7
    mixer_refs, refs = refs[:n_mixer], refs[n_mixer:]
    qm_ref, mk_ref, mv_ref, mqg_ref, wo_ref, gf_ref, wgu_ref, wd_ref, y_ref = refs[:9]
    mix0_ref, mix1_ref = refs[9:11]
    i = pl.program_id(0)

    if skew:
        @pl.when(i == 0)
        def _():
            mix1_ref[...] = jnp.zeros_like(mix1_ref)

    if gated:
        om_ref, gate_refs, swa = mixer_refs[0], mixer_refs[1:], None
    else:
        om_ref, gate_refs = None, ()
        swa = _SwaJob(hist_valid, seqs, i % tiles_per_seq, mixer_refs, *refs[11:13])

        @pl.when(i % tiles_per_seq == 0)
        def _():
            swa.load_history()

    def step(mix_w, mix_r):
        _tail_step(seqs, gated, swa, x_ref, om_ref, gate_refs, qm_ref, mk_ref, mv_ref, mqg_ref,
                   wo_ref, gf_ref, wgu_ref, wd_ref, y_ref, mix_w, mix_r)

    if not skew:
        step(mix0_ref, mix0_ref)
        return

    @pl.when(i % 2 == 0)
    def _():
        step(mix0_ref, mix1_ref)

    @pl.when(i % 2 == 1)
    def _():
        step(mix1_ref, mix0_ref)


def _tail_call(x, gdn, swa, qm, mk, mv, mem_q_gain, w_out, g_ffn, w_gate_up, w_down, layer, tm,
               seqs):
    rows = x.shape[0]
    ns = mk.shape[1]
    n_tiles = rows // tm
    skew = n_tiles > TAIL_MIN_SKEW_TILES
    cur = (lambda i: jnp.minimum(i, n_tiles - 1)) if skew else (lambda i: i)
    prev = (lambda i: jnp.maximum(i - 1, 0)) if skew else (lambda i: i)
    tile = lambda width, which: pl.BlockSpec((tm, width), lambda i: (which(i), 0))
    tiles_per_seq = rows // (tm * ns) if seqs == 1 else 1
    per_seq = lambda *dims: pl.BlockSpec((seqs,) + dims,
                                         lambda i: (cur(i) // tiles_per_seq,) + (0,) * len(dims))
    mem = pl.BlockSpec((None, seqs, N_MEM, MEM_W),
                       lambda i: (layer, cur(i) // tiles_per_seq, 0, 0))
    if gdn:
        o_raw, z, o_gain = gdn
        hist_valid = None
        mixer_specs = [tile(GDN_VW, cur), tile(GDN_VW, cur), _const_spec((1, GDN_DV)),
                       _const_spec((tm, tm))]
        mixer_args = [o_raw, z, o_gain.reshape(1, GDN_DV), _chunk_perm(tm).T]
        scratch = []
    else:
        *mixer_args, sinks, hist_valid = swa
        dup_w = SWA_KV_HEADS * LANES
        mixer_specs = [tile(SWA_QW, cur), tile(SWA_QW, cur), tile(dup_w, cur), tile(dup_w, cur),
                       per_seq(WINDOW, dup_w), per_seq(WINDOW, dup_w),
                       pl.BlockSpec(memory_space=pltpu.SMEM)]
        mixer_args.append(sinks.reshape(1, SWA_HEADS))
        scratch = [pltpu.VMEM((seqs, WINDOW + tm // seqs, dup_w), BF16)] * 2
    return pl.pallas_call(
        functools.partial(_tail_body, seqs, tiles_per_seq, skew, hist_valid),
        grid=(n_tiles + 1 if skew else n_tiles,),
        in_specs=[tile(D_MODEL, prev)] + mixer_specs + [
            tile(MEM_W, cur), mem, mem,
            _const_spec((1, MEM_HD)), _slab_spec(w_out, 0, 0, D_MODEL), _const_spec((1, D_MODEL)),
            _slab_spec(w_gate_up, layer, 0, w_gate_up.shape[2]),
            _slab_spec(w_down, layer, 0, D_MODEL),
        ],
        out_specs=tile(D_MODEL, prev),
        out_shape=jax.ShapeDtypeStruct((rows, D_MODEL), F32),
        scratch_shapes=[pltpu.VMEM((tm, w_out.shape[1]), BF16)] * 2 + scratch,
        compiler_params=_params(("arbitrary",)),
        name="tail",
    )(x, *mixer_args, qm, mk, mv, mem_q_gain.reshape(1, MEM_HD), w_out,
      g_ffn.reshape(1, D_MODEL), w_gate_up, w_down)


def _lane_row(vals, offset):
    return jnp.zeros((1, LANES), F32).at[0, offset:offset + vals.shape[0]].set(vals)


def _dup_heads(a):
    lead = a.shape[:-1]
    a = a.reshape(lead + (SWA_KV_HEADS, 1, SWA_HD))
    return jnp.broadcast_to(a, lead + (SWA_KV_HEADS, 2, SWA_HD)).reshape(lead + (SWA_KV_HEADS * LANES,))


def _undup_heads(a):
    lead = a.shape[:-1]
    return a.reshape(lead + (SWA_KV_HEADS, 2, SWA_HD))[..., 0, :]


def _trunk(x, mem_k, mem_v, gdn_conv, gdn_state, swa_k_hist, swa_v_hist, hist_valid, w):
    ns, t, _ = x.shape
    rows = ns * t
    tm = ROW_TILE
    seqs = max(1, tm // t)
    x2 = x.reshape(rows, D_MODEL)

    conv0 = jnp.pad(gdn_conv[:, :, None, :], ((0, 0), (0, 0), (SUBLANES - 1, 0), (0, 0)))
    conv0 = conv0.reshape(ns, (GDN_CONV - 1) * SUBLANES, GDN_QKV)
    z, qm, *gdn_ops, conv_new = _proj_a_call(x2, w["norm_mix"][0], w, conv0, tm, seqs)
    conv_new = conv_new[:, SUBLANES - 1::SUBLANES, :]
    o_raw, s_new = _gdn_call(gdn_ops, gdn_state, ns, min(t, GDN_TILE_ROWS))
    x2 = _tail_call(x2, (o_raw.reshape(rows, GDN_VW), z, w["o_norm_a"]), None, qm, mem_k, mem_v,
                    w["mem_q_norm"][0], w["out_a"], w["norm_ffn"][0], w["gate_up"], w["down"], 0,
                    tm, seqs)

    qlo, qhi, kd, vd, k_cache, v_cache, qm = _proj_b_call(x2, w["norm_mix"][1], w, PROJ_B_TILE,
                                                          ns > 1)
    swa = (qlo, qhi, kd, vd, _dup_heads(swa_k_hist).astype(BF16),
           _dup_heads(swa_v_hist).astype(BF16), w["sinks_b"], hist_valid)
    x2 = _tail_call(x2, None, swa, qm, mem_k, mem_v, w["mem_q_norm"][1], w["out_b"],
                    w["norm_ffn"][1], w["gate_up"], w["down"], 1, tm, seqs)

    keep = min(WINDOW, t) if ns == 1 else t
    k_new = _undup_heads(k_cache).reshape(ns, -1, SWA_KVW)[:, -keep:]
    v_new = _undup_heads(v_cache).reshape(ns, -1, SWA_KVW)[:, -keep:]
    return x2.reshape(ns, t, D_MODEL), conv_new, s_new, k_new, v_new


def kernel(x_prompt, x_sample, mem_prompt, cache_mem_k, cache_mem_v, state_gdn, state_gdn_conv, cache_swa_k, cache_swa_v, norm_mix, norm_ffn, mem_norm, w_mem_kv, mem_q_norm, mem_k_norm, w_in_a, conv_w_a, a_log, dt_bias, o_norm_a, w_out_a, w_in_b, q_norm_b, k_norm_b, sinks_b, w_out_b, w_gate_up, w_down):
    bsz = x_prompt.shape[0]
    dec = x_sample.shape[0]
    wa = w_in_a[0]
    ba_lo = GDN_QKV + GDN_VW
    qm_lo = ba_lo + 2 * GDN_HEADS
    wb = w_in_b[0]
    w = {
        "norm_mix": norm_mix, "norm_ffn": norm_ffn, "mem_q_norm": mem_q_norm,
        "in_a": w_in_a.astype(BF16), "in_b": w_in_b.astype(BF16),
        "a_ba": jnp.pad(wa[:, ba_lo:qm_lo], ((0, 0), (0, LANES - 2 * GDN_HEADS))).astype(BF16),
        "a_qm": wa[:, qm_lo:].astype(BF16),
        "b_kv": jnp.concatenate([_dup_heads(wb[:, SWA_QW:SWA_QW + SWA_KVW]),
                                 _dup_heads(wb[:, SWA_QW + SWA_KVW:SWA_QW + 2 * SWA_KVW])],
                                axis=1).astype(BF16),
        "q_gain2": jnp.tile(q_norm_b[0], 2).reshape(1, LANES),
        "k_gain2": jnp.tile(k_norm_b[0], 2).reshape(1, LANES),
        "conv_w_a": conv_w_a[0], "o_norm_a": o_norm_a[0],
        "alog_row": _lane_row(a_log[0], GDN_HEADS), "dtb_row": _lane_row(dt_bias[0], GDN_HEADS),
        "out_a": w_out_a.astype(BF16), "out_b": w_out_b.astype(BF16),
        "sinks_b": sinks_b[0],
        "gate_up": w_gate_up.astype(BF16), "down": w_down.astype(BF16),
    }

    mk, mv, mk_bf, mv_bf = _memkv_call(mem_prompt[0], mem_norm, w_mem_kv, mem_k_norm)
    depth = mk.shape[0]
    new_mem_k = mk.reshape(depth, bsz, N_MEM, MEM_HEADS, MEM_HD)
    new_mem_v = mv.reshape(depth, bsz, N_MEM, MEM_HEADS, MEM_HD)

    zero_conv = jnp.zeros((bsz, GDN_CONV - 1, GDN_QKV), F32)
    zero_state = jnp.zeros((bsz, GDN_HEADS, GDN_DK, GDN_DV), F32)
    zero_hist = jnp.zeros((bsz, WINDOW, SWA_KVW), F32)
    y_p, conv_p, state_p, k_p, v_p = _trunk(
        x_prompt, mk_bf.reshape(depth, bsz, N_MEM, MEM_W), mv_bf.reshape(depth, bsz, N_MEM, MEM_W),
        zero_conv, zero_state, zero_hist, zero_hist, False, w)

    y_s, conv_s, state_s, k_s, v_s = _trunk(
        x_sample, cache_mem_k.reshape(depth, dec, N_MEM, MEM_W).astype(BF16),
        cache_mem_v.reshape(depth, dec, N_MEM, MEM_W).astype(BF16), state_gdn_conv[0], state_gdn[0],
        cache_swa_k[0].reshape(dec, WINDOW, SWA_KVW), cache_swa_v[0].reshape(dec, WINDOW, SWA_KVW),
        True, w)

    kv_shape = lambda a: a.reshape(a.shape[0], a.shape[1], SWA_KV_HEADS, SWA_HD)[None]
    return (y_p, y_s, state_p[None], conv_p[None], state_s[None], conv_s[None],
            kv_shape(k_p), kv_shape(v_p), kv_shape(k_s), kv_shape(v_s), new_mem_k, new_mem_v)
```

```python
import functools

import jax
import jax.numpy as jnp
import numpy as np
from jax import lax
from jax.experimental import pallas as pl
from jax.experimental.pallas import tpu as pltpu

F32 = jnp.float32
BF16 = jnp.bfloat16

D_MODEL = 1024
CHUNK = 64
EPS = 1e-6
GDN_HEADS = 8
GDN_DK = 128
GDN_DV = 128
GDN_CONV = 4
GDN_QKV = GDN_HEADS * (2 * GDN_DK + GDN_DV)
GDN_VW = GDN_HEADS * GDN_DV
SWA_HEADS = 16
SWA_KV_HEADS = 4
SWA_HD = 64
SWA_GROUP = SWA_HEADS // SWA_KV_HEADS
SWA_QW = SWA_HEADS * SWA_HD
SWA_KVW = SWA_KV_HEADS * SWA_HD
WINDOW = 128
N_MEM = 256
MEM_HEADS = 4
MEM_HD = 128
MEM_W = MEM_HEADS * MEM_HD
LANES = 128
SUBLANES = 8
VMEM_LIMIT = 56 * 1024 * 1024
ROW_TILE = 256
PROJ_B_TILE = 512
GDN_TILE_ROWS = 4 * CHUNK
GATE_W = 3 * LANES
TAIL_MIN_SKEW_TILES = 2


def _bf16_pieces(x):
    hi = x.astype(BF16)
    rest = x - hi.astype(F32)
    mid = rest.astype(BF16)
    return hi, mid, (rest - mid.astype(F32)).astype(BF16)


def _mm(a, b):
    return jnp.dot(a.astype(BF16), b.astype(BF16), preferred_element_type=F32)


def _mm_nt(a, b):
    return lax.dot_general(a.astype(BF16), b.astype(BF16), (((1,), (1,)), ((), ())),
                           preferred_element_type=F32)


def _mm_tn(a, b):
    return lax.dot_general(a.astype(BF16), b.astype(BF16), (((0,), (0,)), ((), ())),
                           preferred_element_type=F32)


def _rms(x, g):
    return x * lax.rsqrt(jnp.mean(x * x, axis=-1, keepdims=True) + EPS) * g


def _sigmoid(x):
    return 1.0 / (1.0 + jnp.exp(-x))


def _silu(x):
    hx = 0.5 * x
    return hx * jnp.tanh(hx) + hx


def _softplus(x):
    return jnp.maximum(x, 0.0) + jnp.log1p(jnp.exp(-jnp.abs(x)))


def _const_spec(shape):
    nd = len(shape)
    return pl.BlockSpec(shape, lambda *_: (0,) * nd, pipeline_mode=pl.Buffered(1))


def _slab_spec(arr, layer, col_block, width):
    return pl.BlockSpec((None, arr.shape[1], width), lambda *_: (layer, 0, col_block),
                        pipeline_mode=pl.Buffered(1))


def _params(sem):
    return pltpu.CompilerParams(dimension_semantics=sem, vmem_limit_bytes=VMEM_LIMIT)


def _memkv_body(mem_ref, g_ref, w_ref, kg_ref, mk_ref, mv_ref, mkb_ref, mvb_ref):
    h = _rms(mem_ref[...], g_ref[...])
    kv = _mm(h, w_ref[...])
    kg = kg_ref[...]
    for hd in range(MEM_HEADS):
        sl = slice(hd * MEM_HD, (hd + 1) * MEM_HD)
        mk = _rms(kv[:, sl], kg)
        mk_ref[:, sl] = mk
        mkb_ref[:, sl] = mk.astype(BF16)
    mv_ref[...] = kv[:, MEM_W:]
    mvb_ref[...] = kv[:, MEM_W:].astype(BF16)


def _memkv_call(mem, mem_norm, w_mem_kv, mem_k_norm):
    depth = w_mem_kv.shape[0]
    out = jax.ShapeDtypeStruct((depth, N_MEM, MEM_W), F32)
    out_bf = jax.ShapeDtypeStruct((depth, N_MEM, MEM_W), BF16)
    return pl.pallas_call(
        _memkv_body,
        grid=(depth,),
        in_specs=[
            pl.BlockSpec((N_MEM, D_MODEL), lambda i: (0, 0)),
            pl.BlockSpec((None, 1, D_MODEL), lambda i: (i, 0, 0)),
            pl.BlockSpec((None, D_MODEL, 2 * MEM_W), lambda i: (i, 0, 0)),
            pl.BlockSpec((None, 1, MEM_HD), lambda i: (i, 0, 0)),
        ],
        out_specs=[pl.BlockSpec((None, N_MEM, MEM_W), lambda i: (i, 0, 0))] * 4,
        out_shape=[out, out, out_bf, out_bf],
        compiler_params=_params(("arbitrary",)),
        name="memkv",
    )(mem, mem_norm.reshape(depth, 1, D_MODEL), w_mem_kv, mem_k_norm.reshape(depth, 1, MEM_HD))


def _chunk_time(pos):
    return lax.shift_right_logical(pos, 3) + SUBLANES * (pos & (SUBLANES - 1))


def _chunk_perm(tm):
    pos = np.arange(tm)
    src = (pos // CHUNK) * CHUNK + (pos % CHUNK) // SUBLANES + SUBLANES * (pos % SUBLANES)
    return jnp.asarray(src[:, None] == pos[None, :], BF16)


def _proj_a_matmuls(x_ref, g_ref, perm_ref, wqkv_ref, wz_ref, wba_ref, wqm_ref,
                    z_ref, qm_ref, raw_ref, ba_ref):
    h = _rms(x_ref[...], g_ref[...]).astype(BF16)
    h = jnp.dot(perm_ref[...], h, preferred_element_type=F32).astype(BF16)
    z_ref[...] = jnp.dot(h, wz_ref[...], preferred_element_type=F32)
    qm_ref[...] = jnp.dot(h, wqm_ref[...], preferred_element_type=F32)
    ba_ref[...] = jnp.dot(h, wba_ref[...], preferred_element_type=F32)
    raw_ref[...] = jnp.dot(h, wqkv_ref[...], preferred_element_type=F32)


def _proj_a_rows(seqs, raw_ref, ba_ref, cw_ref, alog_ref, dtb_ref,
                 q_ref, k_ref, vb_ref, kbe_ref, cum_ref, cumt_ref, convn_ref, hist_ref):
    c = CHUNK
    tm = raw_ref.shape[0]
    rows = tm // seqs
    n_hist = (GDN_CONV - 1) * SUBLANES

    t_row = _chunk_time(lax.broadcasted_iota(jnp.int32, (c, c), 0))
    t_col = _chunk_time(lax.broadcasted_iota(jnp.int32, (c, c), 1))
    tril = (t_row >= t_col).astype(BF16)
    eye_l = (lax.broadcasted_iota(jnp.int32, (LANES, LANES), 0)
             == lax.broadcasted_iota(jnp.int32, (LANES, LANES), 1)).astype(BF16)
    sub0 = lax.broadcasted_iota(jnp.int32, (SUBLANES, LANES), 0) == 0
    is_beta_lane = lax.broadcasted_iota(jnp.int32, (c, LANES), 1) < GDN_HEADS

    ba = ba_ref[...]
    beta_all = _sigmoid(ba)
    g_all = -jnp.exp(alog_ref[...]) * _softplus(ba + dtb_ref[...])

    for j in range(tm // c):
        rs = slice(j * c, (j + 1) * c)
        s, lr = (j * c) // rows, (j * c) % rows
        cum = sum(jnp.dot(tril, piece, preferred_element_type=F32)
                  for piece in _bf16_pieces(g_all[rs]))
        beta = beta_all[rs]
        e_cum = jnp.exp(cum)
        cum_ref[rs, :] = jnp.concatenate(
            [jnp.where(is_beta_lane, beta, cum), e_cum, jnp.exp(cum[c - 1:c, :] - cum)], axis=1)
        cumt_ref[j] = sum(_mm_nt(eye_l, piece) for piece in _bf16_pieces(cum))

        def conv_act(lo):
            sl = slice(lo, lo + LANES)
            x = raw_ref[rs, sl]
            prev = (hist_ref[s, :, sl] if lr == 0
                    else raw_ref[j * c - n_hist:j * c, sl])
            shifted = [jnp.where(sub0,
                                 pltpu.roll(prev[i * SUBLANES:(i + 1) * SUBLANES], 1, 0),
                                 pltpu.roll(x[c - n_hist + i * SUBLANES:c - n_hist + (i + 1) * SUBLANES],
                                            1, 0)) for i in range(GDN_CONV - 1)]
            acc = x * cw_ref[GDN_CONV - 1:GDN_CONV, sl]
            for d in range(1, GDN_CONV):
                xd = jnp.concatenate(shifted[GDN_CONV - 1 - d:] + [x[0:c - d * SUBLANES]], axis=0)
                acc = acc + xd * cw_ref[GDN_CONV - 1 - d:GDN_CONV - d, sl]
            return _silu(acc)

        for hd in range(GDN_HEADS):
            hs = slice(hd * GDN_DK, (hd + 1) * GDN_DK)
            b_col = beta[:, hd:hd + 1]
            ec_col = e_cum[:, GDN_HEADS + hd:GDN_HEADS + hd + 1]
            q = conv_act(hd * GDN_DK)
            q = q * (lax.rsqrt(jnp.sum(q * q, axis=-1, keepdims=True) + EPS) * (GDN_DK ** -0.5))
            k = conv_act(GDN_HEADS * GDN_DK + hd * GDN_DK)
            k = k * lax.rsqrt(jnp.sum(k * k, axis=-1, keepdims=True) + EPS)
            v = conv_act(2 * GDN_HEADS * GDN_DK + hd * GDN_DV)
            q_ref[rs, hs] = q.astype(BF16)
            k_ref[rs, hs] = k.astype(BF16)
            kbe_ref[rs, hs] = k * b_col * ec_col
            vb_ref[rs, hs] = v * b_col

    for s in range(seqs):
        tail = raw_ref[(s + 1) * rows - n_hist:(s + 1) * rows, :]
        hist_ref[s] = tail
        convn_ref[s] = tail


def _proj_a_body(seqs, tiles_per_seq, x_ref, g_ref, perm_ref, wqkv_ref, wz_ref, wba_ref, wqm_ref,
                 conv0_ref, cw_ref, alog_ref, dtb_ref,
                 z_ref, qm_ref, q_ref, k_ref, vb_ref, kbe_ref, cum_ref, cumt_ref, convn_ref,
                 raw0_ref, raw1_ref, ba0_ref, ba1_ref, hist_ref):
    i = pl.program_id(0)

    @pl.when(i == 0)
    def _():
        raw1_ref[...] = jnp.zeros_like(raw1_ref)
        ba1_ref[...] = jnp.zeros_like(ba1_ref)
        hist_ref[...] = jnp.zeros_like(hist_ref)

    @pl.when((i + tiles_per_seq - 1) % tiles_per_seq == 0)
    def _():
        hist_ref[...] = conv0_ref[...]

    def step(raw_w, ba_w, raw_r, ba_r):
        _proj_a_matmuls(x_ref, g_ref, perm_ref, wqkv_ref, wz_ref, wba_ref, wqm_ref,
                        z_ref, qm_ref, raw_w, ba_w)
        _proj_a_rows(seqs, raw_r, ba_r, cw_ref, alog_ref, dtb_ref,
                     q_ref, k_ref, vb_ref, kbe_ref, cum_ref, cumt_ref, convn_ref, hist_ref)

    @pl.when(i % 2 == 0)
    def _():
        step(raw0_ref, ba0_ref, raw1_ref, ba1_ref)

    @pl.when(i % 2 == 1)
    def _():
        step(raw1_ref, ba1_ref, raw0_ref, ba0_ref)


def _proj_a_call(x, g, w, conv0, tm, seqs):
    rows = x.shape[0]
    ns = conv0.shape[0]
    n_tiles = rows // tm
    n_hist = (GDN_CONV - 1) * SUBLANES
    tiles_per_seq = rows // (ns * tm) if seqs == 1 else 1
    seq_of = (lambda i: i // tiles_per_seq) if seqs == 1 else (lambda i: i)
    cur = lambda i: jnp.minimum(i, n_tiles - 1)
    prev = lambda i: jnp.maximum(i - 1, 0)
    tile = lambda width, which: pl.BlockSpec((tm, width), lambda i: (which(i), 0))
    hist = pl.BlockSpec((seqs, n_hist, GDN_QKV), lambda i: (seq_of(prev(i)), 0, 0))
    wide_bf = jax.ShapeDtypeStruct((rows, GDN_VW), BF16)
    wide_f32 = jax.ShapeDtypeStruct((rows, GDN_VW), F32)
    perm = _chunk_perm(tm)
    return pl.pallas_call(
        functools.partial(_proj_a_body, seqs, tiles_per_seq),
        grid=(n_tiles + 1,),
        in_specs=[tile(D_MODEL, cur), _const_spec((1, D_MODEL)), _const_spec(perm.shape)]
        + [_slab_spec(w["in_a"], 0, 0, GDN_QKV), _slab_spec(w["in_a"], 0, GDN_QKV // GDN_VW, GDN_VW),
           _const_spec(w["a_ba"].shape), _const_spec(w["a_qm"].shape)]
        + [hist, _const_spec((GDN_CONV, GDN_QKV)), _const_spec((1, LANES)), _const_spec((1, LANES))],
        out_specs=[tile(GDN_VW, cur), tile(MEM_W, cur)] + [tile(GDN_VW, prev)] * 4
        + [tile(GATE_W, prev),
           pl.BlockSpec((tm // CHUNK, LANES, CHUNK), lambda i: (prev(i), 0, 0)), hist],
        out_shape=[wide_f32, jax.ShapeDtypeStruct((rows, MEM_W), F32)]
        + [wide_bf] * 2 + [wide_f32] * 2
        + [jax.ShapeDtypeStruct((rows, GATE_W), F32),
           jax.ShapeDtypeStruct((rows // CHUNK, LANES, CHUNK), F32),
           jax.ShapeDtypeStruct((ns, n_hist, GDN_QKV), F32)],
        scratch_shapes=[pltpu.VMEM((tm, GDN_QKV), F32)] * 2 + [pltpu.VMEM((tm, LANES), F32)] * 2
        + [pltpu.VMEM((seqs, n_hist, GDN_QKV), F32)],
        compiler_params=_params(("arbitrary",)),
        name="proj_a",
    )(x, g.reshape(1, D_MODEL), perm, w["in_a"], w["in_a"], w["a_ba"], w["a_qm"], conv0,
      w["conv_w_a"], w["alog_row"], w["dtb_row"])


def _gdn_chunk_solves(q_ref, k_ref, vb_ref, kbe_ref, cum_ref, cumt_ref, sol_ref, qk_ref):
    c = CHUNK
    nc = q_ref.shape[0] // c

    row = _chunk_time(lax.broadcasted_iota(jnp.int32, (c, c), 0))
    col = _chunk_time(lax.broadcasted_iota(jnp.int32, (c, c), 1))
    causal = row >= col
    strict = row > col
    lane = lax.broadcasted_iota(jnp.int32, (c, LANES), 1)
    low = lane < c
    eye2 = (lax.broadcasted_iota(jnp.int32, (c, LANES), 0) == (lane & (c - 1))).astype(F32)
    pad = jnp.zeros((c, LANES - c), F32)

    pairs = [(j, hd) for j in range(nc) for hd in range(GDN_HEADS)]
    rs = lambda j: slice(j * c, (j + 1) * c)
    hs = lambda hd: slice(hd * GDN_DK, (hd + 1) * GDN_DK)
    gl = lambda hd: slice(GDN_HEADS + hd, GDN_HEADS + hd + 1)

    decays = [jnp.exp(jnp.where(causal, cum_ref[rs(j), gl(hd)] - cumt_ref[j, gl(hd), :],
                                -jnp.inf)) for j, hd in pairs]
    grams = [_mm_nt(jnp.concatenate([k_ref[rs(j), hs(hd)], q_ref[rs(j), hs(hd)]], axis=0),
                    k_ref[rs(j), hs(hd)]) for j, hd in pairs]
    yield
    ps = [jnp.concatenate([jnp.where(strict, -(g[:c] * d * cum_ref[rs(j), hd:hd + 1]), 0.0), pad],
                          axis=1) for g, d, (j, hd) in zip(grams, decays, pairs)]
    for i, (g, d) in enumerate(zip(grams, decays)):
        qk_ref[i] = g[c:] * d

    ts = [eye2] * len(pairs)
    for _ in range(6):
        prods = [_mm(p[:, :c], jnp.where(low, p, t)) for t, p in zip(ts, ps)]
        yield
        ts = [t + r for t, r in zip(ts, prods)]
        ps = prods
    ns = [pltpu.roll(t - eye2, c, 1)[:, :c] for t in ts]

    rhss = [jnp.concatenate([vb_ref[rs(j), hs(hd)], kbe_ref[rs(j), hs(hd)]], axis=1)
            for j, hd in pairs]
    yield
    for i, (n, rhs) in enumerate(zip(ns, rhss)):
        sol_ref[i] = rhs + _mm(n, rhs)


def _gdn_recurrence(q_ref, k_ref, cum_ref, sol_ref, qk_ref, o_ref, s_ref):
    c = CHUNK
    nc = q_ref.shape[0] // c
    heads = range(GDN_HEADS)
    rs = lambda j: slice(j * c, (j + 1) * c)
    hs = lambda hd: slice(hd * GDN_DK, (hd + 1) * GDN_DK)
    ss = [s_ref[hd] for hd in heads]
    for j in range(nc):
        sols = [sol_ref[j * GDN_HEADS + hd] for hd in heads]
        col = lambda field, hd: slice(field * LANES + GDN_HEADS + hd,
                                      field * LANES + GDN_HEADS + hd + 1)
        e_cums = [cum_ref[rs(j), col(1, hd)] for hd in heads]
        e_rests = [cum_ref[rs(j), col(2, hd)] for hd in heads]
        e_totals = [cum_ref[j * c + c - 1:j * c + c, col(1, hd)] for hd in heads]
        r1s = [_mm(jnp.concatenate([sols[hd][:, GDN_DV:].astype(BF16), q_ref[rs(j), hs(hd)]],
                                   axis=0), ss[hd]) for hd in heads]
        yield
        us = [sols[hd][:, :GDN_DV] - r1s[hd][:c] for hd in heads]
        for hd in heads:
            o_ref[rs(j), hs(hd)] = (r1s[hd][c:] * e_cums[hd]
                                    + _mm(qk_ref[j * GDN_HEADS + hd], us[hd]))
        ss = [ss[hd] * e_totals[hd] + _mm_tn(k_ref[rs(j), hs(hd)], us[hd] * e_rests[hd])
              for hd in heads]
        yield
    for hd in heads:
        s_ref[hd] = ss[hd]


def _gdn_body(skew, q_ref, k_ref, vb_ref, kbe_ref, cum_ref, cumt_ref, qr_ref, kr_ref,
              cumr_ref, s0_ref, o_ref, sn_ref, s_ref, sol0_ref, sol1_ref, qk0_ref, qk1_ref):
    i = pl.program_id(1)
    solve_args = (q_ref, k_ref, vb_ref, kbe_ref, cum_ref, cumt_ref)
    recur_args = (qr_ref, kr_ref, cumr_ref)

    if not skew:
        s_ref[...] = s0_ref[...]
        for _ in _gdn_chunk_solves(*solve_args, sol0_ref, qk0_ref):
            pass
        for _ in _gdn_recurrence(*recur_args, sol0_ref, qk0_ref, o_ref, s_ref):
            pass
        sn_ref[...] = s_ref[...]
        return

    @pl.when(i == 0)
    def _():
        sol1_ref[...] = jnp.zeros_like(sol1_ref)
        qk1_ref[...] = jnp.zeros_like(qk1_ref)
        s_ref[...] = jnp.zeros_like(s_ref)

    @pl.when(i == 1)
    def _():
        s_ref[...] = s0_ref[...]

    def step(sol_w, qk_w, sol_r, qk_r):
        solve = _gdn_chunk_solves(*solve_args, sol_w, qk_w)
        recur = _gdn_recurrence(*recur_args, sol_r, qk_r, o_ref, s_ref)
        live = [solve, recur]
        while live:
            for gen in list(live):
                if next(gen, live) is live:
                    live.remove(gen)
        sn_ref[...] = s_ref[...]

    @pl.when(i % 2 == 0)
    def _():
        step(sol0_ref, qk0_ref, sol1_ref, qk1_ref)

    @pl.when(i % 2 == 1)
    def _():
        step(sol1_ref, qk1_ref, sol0_ref, qk0_ref)


def _gdn_call(ops, s0, ns, tr):
    q, k, vb, kbe, cum, cumt = ops
    t = q.shape[0] // ns
    n_tiles = t // tr
    skew = n_tiles > 1
    cur = (lambda i: jnp.minimum(i, n_tiles - 1)) if skew else (lambda i: i)
    prev = (lambda i: jnp.maximum(i - 1, 0)) if skew else (lambda i: i)
    seq = lambda width, which: pl.BlockSpec((None, tr, width), lambda s, i: (s, which(i), 0))
    state = pl.BlockSpec((None, GDN_HEADS, GDN_DK, GDN_DV), lambda s, i: (s, 0, 0, 0))
    rows3 = lambda a: a.reshape(ns, t, a.shape[-1])
    units = (tr // CHUNK) * GDN_HEADS
    return pl.pallas_call(
        functools.partial(_gdn_body, skew),
        grid=(ns, n_tiles + 1 if skew else n_tiles),
        in_specs=[seq(GDN_VW, cur)] * 4
        + [seq(GATE_W, cur),
           pl.BlockSpec((None, tr // CHUNK, LANES, CHUNK), lambda s, i: (s, cur(i), 0, 0)),
           seq(GDN_VW, prev), seq(GDN_VW, prev), seq(GATE_W, prev), state],
        out_specs=[seq(GDN_VW, prev), state],
        out_shape=[jax.ShapeDtypeStruct((ns, t, GDN_VW), F32),
                   jax.ShapeDtypeStruct((ns, GDN_HEADS, GDN_DK, GDN_DV), F32)],
        scratch_shapes=[pltpu.VMEM((GDN_HEADS, GDN_DK, GDN_DV), F32)]
        + [pltpu.VMEM((units, CHUNK, GDN_DV + GDN_DK), F32)] * 2
        + [pltpu.VMEM((units, CHUNK, CHUNK), F32)] * 2,
        compiler_params=_params(("arbitrary", "arbitrary")),
        name="gdn",
    )(rows3(q), rows3(k), rows3(vb), rows3(kbe), rows3(cum),
      cumt.reshape(ns, t // CHUNK, LANES, CHUNK), rows3(q), rows3(k), rows3(cum), s0)


def _proj_b_body(x_ref, g_ref, wq_ref, wkv_ref, wqm_ref, qg_ref, kg_ref,
                 qlo_ref, qhi_ref, kd_ref, vd_ref, kc_ref, vc_ref, qm_ref):
    h = _rms(x_ref[...], g_ref[...]).astype(BF16)
    qm_ref[...] = jnp.dot(h, wqm_ref[...], preferred_element_type=F32)
    q = jnp.dot(h, wq_ref[...], preferred_element_type=F32)
    kv = jnp.dot(h, wkv_ref[...], preferred_element_type=F32)
    lo = lax.broadcasted_iota(jnp.int32, (1, LANES), 1) < SWA_HD
    qg = qg_ref[...] * (SWA_HD ** -0.5)
    for p in range(SWA_QW // LANES):
        sl = slice(p * LANES, (p + 1) * LANES)
        x = q[:, sl]
        x2 = x * x
        m_lo = jnp.sum(jnp.where(lo, x2, 0.0), axis=-1, keepdims=True) * (1.0 / SWA_HD)
        m_hi = jnp.sum(jnp.where(lo, 0.0, x2), axis=-1, keepdims=True) * (1.0 / SWA_HD)
        qn = x * jnp.where(lo, lax.rsqrt(m_lo + EPS), lax.rsqrt(m_hi + EPS)) * qg
        qlo_ref[:, sl] = jnp.where(lo, qn, 0.0).astype(BF16)
        qhi_ref[:, sl] = jnp.where(lo, 0.0, qn).astype(BF16)
    kg = kg_ref[...]
    for p in range(SWA_KV_HEADS):
        sl = slice(p * LANES, (p + 1) * LANES)
        kn = _rms(kv[:, sl], kg)
        kc_ref[:, sl] = kn
        kd_ref[:, sl] = kn.astype(BF16)
    v = kv[:, SWA_KV_HEADS * LANES:]
    vc_ref[...] = v
    vd_ref[...] = v.astype(BF16)


def _proj_b_call(x, g, w, tm, keep_all):
    rows = x.shape[0]
    dup_w = SWA_KV_HEADS * LANES
    tile = lambda width: pl.BlockSpec((tm, width), lambda i: (i, 0))
    cache = tile(dup_w) if keep_all else pl.BlockSpec((tm, dup_w), lambda i: (0, 0))
    cache_shape = jax.ShapeDtypeStruct((rows if keep_all else tm, dup_w), F32)
    return pl.pallas_call(
        _proj_b_body,
        grid=(rows // tm,),
        in_specs=[tile(D_MODEL), _const_spec((1, D_MODEL))]
        + [_slab_spec(w["in_b"], 0, 0, SWA_QW), _const_spec(w["b_kv"].shape),
           _slab_spec(w["in_b"], 0, (SWA_QW + 2 * SWA_KVW) // MEM_W, MEM_W)]
        + [_const_spec((1, LANES))] * 2,
        out_specs=[tile(SWA_QW)] * 2 + [tile(dup_w)] * 2 + [cache] * 2 + [tile(MEM_W)],
        out_shape=[jax.ShapeDtypeStruct((rows, SWA_QW), BF16)] * 2
        + [jax.ShapeDtypeStruct((rows, dup_w), BF16)] * 2 + [cache_shape] * 2
        + [jax.ShapeDtypeStruct((rows, MEM_W), F32)],
        compiler_params=_params(("arbitrary",)),
        name="proj_b",
    )(x, g.reshape(1, D_MODEL), w["in_b"], w["b_kv"], w["in_b"], w["q_gain2"], w["k_gain2"])


class _SwaJob:
    def __init__(self, hist_valid, seqs, tile_in_seq, refs, kwin_ref, vwin_ref):
        (self.qlo_ref, self.qhi_ref, self.k_ref, self.v_ref, self.kh_ref, self.vh_ref,
         self.sink_ref) = refs
        self.hist_valid, self.seqs, self.tile_in_seq = hist_valid, seqs, tile_in_seq
        self.kwin_ref, self.vwin_ref = kwin_ref, vwin_ref
        self.tm = self.qlo_ref.shape[0]
        self.rows = self.tm // seqs
        self.units = [(sq, j, h) for sq in range(seqs) for j in range(self.rows // CHUNK)
                      for h in range(SWA_KV_HEADS)]

    def load_history(self):
        self.kwin_ref[:, 0:WINDOW, :] = self.kh_ref[...]
        self.vwin_ref[:, 0:WINDOW, :] = self.vh_ref[...]

    def scores(self):
        c, rows = CHUNK, self.rows
        slab = lambda p: slice(p * LANES, (p + 1) * LANES)
        for sq in range(self.seqs):
            self.kwin_ref[sq, WINDOW:WINDOW + rows, :] = self.k_ref[sq * rows:(sq + 1) * rows, :]
            self.vwin_ref[sq, WINDOW:WINDOW + rows, :] = self.v_ref[sq * rows:(sq + 1) * rows, :]
        out = []
        for sq, j, h in self.units:
            r = slice(sq * rows + j * c, sq * rows + (j + 1) * c)
            qs = jnp.concatenate([self.qlo_ref[r, slab(2 * h)], self.qhi_ref[r, slab(2 * h)],
                                  self.qlo_ref[r, slab(2 * h + 1)], self.qhi_ref[r, slab(2 * h + 1)]],
                                 axis=0)
            s = _mm_nt(qs, self.kwin_ref[sq, j * c:j * c + WINDOW + c, slab(h)])
            if not self.hist_valid and j * c < WINDOW:
                key_col = lax.broadcasted_iota(jnp.int32, (SWA_GROUP * c, WINDOW + c), 1)
                first_key = self.tile_in_seq * rows + j * c - WINDOW
                s = jnp.where(key_col + first_key >= 0, s, -jnp.inf)
            out.append(s)
        return out

    def softmax(self, scores):
        c = CHUNK
        sinks = [jnp.concatenate([jnp.full((c, 1), self.sink_ref[0, h * SWA_GROUP + g], F32)
                                  for g in range(SWA_GROUP)], axis=0)
                 for h in range(SWA_KV_HEADS)]
        out = []
        for s, (sq, j, h) in zip(scores, self.units):
            m = jnp.maximum(jnp.max(s, axis=-1, keepdims=True), sinks[h])
            p = jnp.exp(s - m)
            out.append((p / (jnp.sum(p, axis=-1, keepdims=True) + jnp.exp(sinks[h] - m)))
                       .astype(BF16))
        return out

    def values(self, probs):
        c, rows = CHUNK, self.rows
        slab = lambda p: slice(p * LANES, (p + 1) * LANES)
        lo = lax.broadcasted_iota(jnp.int32, (1, LANES), 1) < SWA_HD
        slabs = {}
        for p, (sq, j, h) in zip(probs, self.units):
            o = _mm(p, self.vwin_ref[sq, j * c:j * c + WINDOW + c, slab(h)])
            for half in range(2):
                slabs[sq, j, 2 * h + half] = jnp.where(
                    lo, o[2 * half * c:(2 * half + 1) * c],
                    o[(2 * half + 1) * c:(2 * half + 2) * c]).astype(BF16)
        for ref in (self.kwin_ref, self.vwin_ref):
            ref[:, 0:WINDOW, :] = ref[:, rows:rows + WINDOW, :]
        return jnp.concatenate(
            [jnp.concatenate([slabs[sq, j, p] for p in range(SWA_QW // LANES)], axis=1)
             for sq in range(self.seqs) for j in range(rows // c)], axis=0)


def _tail_step(seqs, gated, swa, x_ref, om_ref, gate_refs, qm_ref, mk_ref, mv_ref, mqg_ref, wo_ref,
               gf_ref, wgu_ref, wd_ref, y_ref, mix_w, mix_r):
    tm = x_ref.shape[0]
    rows = tm // seqs
    d_ff = wd_ref.shape[0]
    half = d_ff // 2
    heads = [(sq, hd) for sq in range(seqs) for hd in range(MEM_HEADS)]
    rs = lambda sq: slice(sq * rows, (sq + 1) * rows)
    hs = lambda hd: slice(hd * MEM_HD, (hd + 1) * MEM_HD)

    def ffn():
        x = x_ref[...] + jnp.dot(mix_r[...], wo_ref[...], preferred_element_type=F32)
        yield
        h = _rms(x, gf_ref[...]).astype(BF16)
        g0 = jnp.dot(h, wgu_ref[:, 0:half], preferred_element_type=F32)
        u0 = jnp.dot(h, wgu_ref[:, d_ff:d_ff + half], preferred_element_type=F32)
        yield
        act0 = (_silu(g0) * u0).astype(BF16)
        g1 = jnp.dot(h, wgu_ref[:, half:d_ff], preferred_element_type=F32)
        u1 = jnp.dot(h, wgu_ref[:, d_ff + half:], preferred_element_type=F32)
        yield
        y = x + jnp.dot(act0, wd_ref[0:half, :], preferred_element_type=F32)
        act1 = (_silu(g1) * u1).astype(BF16)
        y_ref[...] = y + jnp.dot(act1, wd_ref[half:, :], preferred_element_type=F32)

    def operand():
        if swa:
            swa_scores = swa.scores()
        mqg = mqg_ref[...] * (MEM_HD ** -0.5)
        scores = [_mm_nt(_rms(qm_ref[rs(sq), hs(hd)], mqg), mk_ref[sq, :, hs(hd)])
                  for sq, hd in heads]
        yield
        probs = []
        for s in scores:
            p = jnp.exp(s - jnp.max(s, axis=-1, keepdims=True))
            probs.append((p / jnp.sum(p, axis=-1, keepdims=True)).astype(BF16))
        o_heads = [_mm(p, mv_ref[sq, :, hs(hd)]).astype(BF16)
                   for p, (sq, hd) in zip(probs, heads)]
        o_mem = jnp.concatenate(
            [jnp.concatenate(o_heads[sq * MEM_HEADS:(sq + 1) * MEM_HEADS], axis=1)
             for sq in range(seqs)], axis=0)
        yield
        if gated:
            z_ref, og_ref, unperm_ref = gate_refs
            og = og_ref[...]
            parts = [(_rms(om_ref[:, hd * GDN_DV:(hd + 1) * GDN_DV], og)
                      * _silu(z_ref[:, hd * GDN_DV:(hd + 1) * GDN_DV])).astype(BF16)
                     for hd in range(GDN_HEADS)]
            mixed = jnp.concatenate(parts + [o_mem], axis=1)
            mix_w[...] = jnp.dot(unperm_ref[...], mixed, preferred_element_type=F32).astype(BF16)
        else:
            mix_w[...] = jnp.concatenate([swa.values(swa.softmax(swa_scores)), o_mem], axis=1)

    if mix_w is mix_r:
        for job in (operand(), ffn()):
            for _ in job:
                pass
        return
    live = [ffn(), operand()]
    while live:
        for gen in list(live):
            if next(gen, live) is live:
                live.remove(gen)


def _tail_body(seqs, tiles_per_seq, skew, hist_valid, x_ref, *refs):
    gated = hist_valid is None
    n_mixer = 4 if gated else 7
    mixer_refs, refs = refs[:n_mixer], refs[n_mixer:]
    qm_ref, mk_ref, mv_ref, mqg_ref, wo_ref, gf_ref, wgu_ref, wd_ref, y_ref = refs[:9]
    mix0_ref, mix1_ref = refs[9:11]
    i = pl.program_id(0)

    if skew:
        @pl.when(i == 0)
        def _():
            mix1_ref[...] = jnp.zeros_like(mix1_ref)

    if gated:
        om_ref, gate_refs, swa = mixer_refs[0], mixer_refs[1:], None
    else:
        om_ref, gate_refs = None, ()
        swa = _SwaJob(hist_valid, seqs, i % tiles_per_seq, mixer_refs, *refs[11:13])

        @pl.when(i % tiles_per_seq == 0)
        def _():
            swa.load_history()

    def step(mix_w, mix_r):
        _tail_step(seqs, gated, swa, x_ref, om_ref, gate_refs, qm_ref, mk_ref, mv_ref, mqg_ref,
                   wo_ref, gf_ref, wgu_ref, wd_ref, y_ref, mix_w, mix_r)

    if not skew:
        step(mix0_ref, mix0_ref)
        return

    @pl.when(i % 2 == 0)
    def _():
        step(mix0_ref, mix1_ref)

    @pl.when(i % 2 == 1)
    def _():
        step(mix1_ref, mix0_ref)


def _tail_call(x, gdn, swa, qm, mk, mv, mem_q_gain, w_out, g_ffn, w_gate_up, w_down, layer, tm,
               seqs):
    rows = x.shape[0]
    ns = mk.shape[1]
    n_tiles = rows // tm
    skew = n_tiles > TAIL_MIN_SKEW_TILES
    cur = (lambda i: jnp.minimum(i, n_tiles - 1)) if skew else (lambda i: i)
    prev = (lambda i: jnp.maximum(i - 1, 0)) if skew else (lambda i: i)
    tile = lambda width, which: pl.BlockSpec((tm, width), lambda i: (which(i), 0))
    tiles_per_seq = rows // (tm * ns) if seqs == 1 else 1
    per_seq = lambda *dims: pl.BlockSpec((seqs,) + dims,
                                         lambda i: (cur(i) // tiles_per_seq,) + (0,) * len(dims))
    mem = pl.BlockSpec((None, seqs, N_MEM, MEM_W),
                       lambda i: (layer, cur(i) // tiles_per_seq, 0, 0))
    if gdn:
        o_raw, z, o_gain = gdn
        hist_valid = None
        mixer_specs = [tile(GDN_VW, cur), tile(GDN_VW, cur), _const_spec((1, GDN_DV)),
                       _const_spec((tm, tm))]
        mixer_args = [o_raw, z, o_gain.reshape(1, GDN_DV), _chunk_perm(tm).T]
        scratch = []
    else:
        *mixer_args, sinks, hist_valid = swa
        dup_w = SWA_KV_HEADS * LANES
        mixer_specs = [tile(SWA_QW, cur), tile(SWA_QW, cur), tile(dup_w, cur), tile(dup_w, cur),
                       per_seq(WINDOW, dup_w), per_seq(WINDOW, dup_w),
                       pl.BlockSpec(memory_space=pltpu.SMEM)]
        mixer_args.append(sinks.reshape(1, SWA_HEADS))
        scratch = [pltpu.VMEM((seqs, WINDOW + tm // seqs, dup_w), BF16)] * 2
    return pl.pallas_call(
        functools.partial(_tail_body, seqs, tiles_per_seq, skew, hist_valid),
        grid=(n_tiles + 1 if skew else n_tiles,),
        in_specs=[tile(D_MODEL, prev)] + mixer_specs + [
            tile(MEM_W, cur), mem, mem,
            _const_spec((1, MEM_HD)), _slab_spec(w_out, 0, 0, D_MODEL), _const_spec((1, D_MODEL)),
            _slab_spec(w_gate_up, layer, 0, w_gate_up.shape[2]),
            _slab_spec(w_down, layer, 0, D_MODEL),
        ],
        out_specs=tile(D_MODEL, prev),
        out_shape=jax.ShapeDtypeStruct((rows, D_MODEL), F32),
        scratch_shapes=[pltpu.VMEM((tm, w_out.shape[1]), BF16)] * 2 + scratch,
        compiler_params=_params(("arbitrary",)),
        name="tail",
    )(x, *mixer_args, qm, mk, mv, mem_q_gain.reshape(1, MEM_HD), w_out,
      g_ffn.reshape(1, D_MODEL), w_gate_up, w_down)


def _lane_row(vals, offset):
    return jnp.zeros((1, LANES), F32).at[0, offset:offset + vals.shape[0]].set(vals)


def _dup_heads(a):
    lead = a.shape[:-1]
    a = a.reshape(lead + (SWA_KV_HEADS, 1, SWA_HD))
    return jnp.broadcast_to(a, lead + (SWA_KV_HEADS, 2, SWA_HD)).reshape(lead + (SWA_KV_HEADS * LANES,))


def _undup_heads(a):
    lead = a.shape[:-1]
    return a.reshape(lead + (SWA_KV_HEADS, 2, SWA_HD))[..., 0, :]


def _trunk(x, mem_k, mem_v, gdn_conv, gdn_state, swa_k_hist, swa_v_hist, hist_valid, w):
    ns, t, _ = x.shape
    rows = ns * t
    tm = ROW_TILE
    seqs = max(1, tm // t)
    x2 = x.reshape(rows, D_MODEL)

    conv0 = jnp.pad(gdn_conv[:, :, None, :], ((0, 0), (0, 0), (SUBLANES - 1, 0), (0, 0)))
    conv0 = conv0.reshape(ns, (GDN_CONV - 1) * SUBLANES, GDN_QKV)
    z, qm, *gdn_ops, conv_new = _proj_a_call(x2, w["norm_mix"][0], w, conv0, tm, seqs)
    conv_new = conv_new[:, SUBLANES - 1::SUBLANES, :]
    o_raw, s_new = _gdn_call(gdn_ops, gdn_state, ns, min(t, GDN_TILE_ROWS))
    x2 = _tail_call(x2, (o_raw.reshape(rows, GDN_VW), z, w["o_norm_a"]), None, qm, mem_k, mem_v,
                    w["mem_q_norm"][0], w["out_a"], w["norm_ffn"][0], w["gate_up"], w["down"], 0,
                    tm, seqs)

    qlo, qhi, kd, vd, k_cache, v_cache, qm = _proj_b_call(x2, w["norm_mix"][1], w, PROJ_B_TILE,
                                                          ns > 1)
    swa = (qlo, qhi, kd, vd, _dup_heads(swa_k_hist).astype(BF16),
           _dup_heads(swa_v_hist).astype(BF16), w["sinks_b"], hist_valid)
    x2 = _tail_call(x2, None, swa, qm, mem_k, mem_v, w["mem_q_norm"][1], w["out_b"],
                    w["norm_ffn"][1], w["gate_up"], w["down"], 1, tm, seqs)

    keep = min(WINDOW, t) if ns == 1 else t
    k_new = _undup_heads(k_cache).reshape(ns, -1, SWA_KVW)[:, -keep:]
    v_new = _undup_heads(v_cache).reshape(ns, -1, SWA_KVW)[:, -keep:]
    return x2.reshape(ns, t, D_MODEL), conv_new, s_new, k_new, v_new


def kernel(x_prompt, x_sample, mem_prompt, cache_mem_k, cache_mem_v, state_gdn, state_gdn_conv, cache_swa_k, cache_swa_v, norm_mix, norm_ffn, mem_norm, w_mem_kv, mem_q_norm, mem_k_norm, w_in_a, conv_w_a, a_log, dt_bias, o_norm_a, w_out_a, w_in_b, q_norm_b, k_norm_b, sinks_b, w_out_b, w_gate_up, w_down):
    bsz = x_prompt.shape[0]
    dec = x_sample.shape[0]
    wa = w_in_a[0]
    ba_lo = GDN_QKV + GDN_VW
    qm_lo = ba_lo + 2 * GDN_HEADS
    wb = w_in_b[0]
    w = {
        "norm_mix": norm_mix, "norm_ffn": norm_ffn, "mem_q_norm": mem_q_norm,
        "in_a": w_in_a.astype(BF16), "in_b": w_in_b.astype(BF16),
        "a_ba": jnp.pad(wa[:, ba_lo:qm_lo], ((0, 0), (0, LANES - 2 * GDN_HEADS))).astype(BF16),
        "a_qm": wa[:, qm_lo:].astype(BF16),
        "b_kv": jnp.concatenate([_dup_heads(wb[:, SWA_QW:SWA_QW + SWA_KVW]),
                                 _dup_heads(wb[:, SWA_QW + SWA_KVW:SWA_QW + 2 * SWA_KVW])],
                                axis=1).astype(BF16),
        "q_gain2": jnp.tile(q_norm_b[0], 2).reshape(1, LANES),
        "k_gain2": jnp.tile(k_norm_b[0], 2).reshape(1, LANES),
        "conv_w_a": conv_w_a[0], "o_norm_a": o_norm_a[0],
        "alog_row": _lane_row(a_log[0], GDN_HEADS), "dtb_row": _lane_row(dt_bias[0], GDN_HEADS),
        "out_a": w_out_a.astype(BF16), "out_b": w_out_b.astype(BF16),
        "sinks_b": sinks_b[0],
        "gate_up": w_gate_up.astype(BF16), "down": w_down.astype(BF16),
    }

    mk, mv, mk_bf, mv_bf = _memkv_call(mem_prompt[0], mem_norm, w_mem_kv, mem_k_norm)
    depth = mk.shape[0]
    new_mem_k = mk.reshape(depth, bsz, N_MEM, MEM_HEADS, MEM_HD)
    new_mem_v = mv.reshape(depth, bsz, N_MEM, MEM_HEADS, MEM_HD)

    zero_conv = jnp.zeros((bsz, GDN_CONV - 1, GDN_QKV), F32)
    zero_state = jnp.zeros((bsz, GDN_HEADS, GDN_DK, GDN_DV), F32)
    zero_hist = jnp.zeros((bsz, WINDOW, SWA_KVW), F32)
    y_p, conv_p, state_p, k_p, v_p = _trunk(
        x_prompt, mk_bf.reshape(depth, bsz, N_MEM, MEM_W), mv_bf.reshape(depth, bsz, N_MEM, MEM_W),
        zero_conv, zero_state, zero_hist, zero_hist, False, w)

    y_s, conv_s, state_s, k_s, v_s = _trunk(
        x_sample, cache_mem_k.reshape(depth, dec, N_MEM, MEM_W).astype(BF16),
        cache_mem_v.reshape(depth, dec, N_MEM, MEM_W).astype(BF16), state_gdn_conv[0], state_gdn[0],
        cache_swa_k[0].reshape(dec, WINDOW, SWA_KVW), cache_swa_v[0].reshape(dec, WINDOW, SWA_KVW),
        True, w)

    kv_shape = lambda a: a.reshape(a.shape[0], a.shape[1], SWA_KV_HEADS, SWA_HD)[None]
    return (y_p, y_s, state_p[None], conv_p[None], state_s[None], conv_s[None],
            kv_shape(k_p), kv_shape(v_p), kv_shape(k_s), kv_shape(v_s), new_mem_k, new_mem_v)
```
